```python
import math
import jax, jax.numpy as jnp
from jax import lax
import numpy as np

D_MODEL = 1024
BATCH = 4
SEQ = 8192
DEPTH = 2

GRID_W = 64
CTX_LEN = 256
EPS = 1e-6

HG_HEADS = 8
HG_DK = 64
HG_DV = 64
HG_W = HG_HEADS * HG_DV
HG_CHUNK = 64

MLA_HEADS = 8
MLA_Q_RANK = 256
MLA_KV_RANK = 128
MLA_NOPE = 64
MLA_ROPE = 32
MLA_V = 64
MLA_QK = MLA_NOPE + MLA_ROPE
MLA_W = MLA_HEADS * MLA_V
Q_BLOCK = 128
ROPE_BASE = 10000.0

LRU_W = 512
LRU_BLOCKS = 8
LRU_BD = LRU_W // LRU_BLOCKS
CONV_W = 4
LRU_C = 8.0

N_GROUPS = 4
EXP_PER_GROUP = 4
N_EXPERTS = N_GROUPS * EXP_PER_GROUP
D_EXPERT = 256
TOP_K_IN_GROUP = 2

IN_SPLITS = (HG_W, HG_W, HG_W, HG_W, HG_W, MLA_Q_RANK, MLA_KV_RANK, MLA_ROPE, LRU_W, LRU_W, D_MODEL, D_MODEL, D_MODEL)
IN_WIDTH = sum(IN_SPLITS)

kernel_name = 'hybrid_flow_backbone'


def _rmsnorm(t, g):
    tf = t.astype(jnp.float32)
    tf = tf * lax.rsqrt(jnp.mean(tf * tf, axis=-1, keepdims=True) + EPS)
    return (tf * g.astype(jnp.float32)).astype(t.dtype)


def _modulate(t, shift, scale):
    return t * (1.0 + scale) + shift


def _split_in(p):
    offsets = np.cumsum(IN_SPLITS)[:-1].tolist()
    return jnp.split(p, offsets, axis=-1)


def _axial_rope(rows):
    half = MLA_ROPE // 2
    pos_row = jnp.repeat(jnp.arange(rows, dtype=jnp.float32), GRID_W)
    pos_col = jnp.tile(jnp.arange(GRID_W, dtype=jnp.float32), rows)
    inv = ROPE_BASE ** (-jnp.arange(0, half, 2, dtype=jnp.float32) / half)
    ang = jnp.concatenate([pos_row[:, None] * inv, pos_col[:, None] * inv], axis=-1)
    return jnp.cos(ang), jnp.sin(ang)


def _apply_rope(t, cos, sin):
    half = MLA_ROPE // 2
    nope, rp = t[..., :MLA_NOPE], t[..., MLA_NOPE:]
    r1, r2 = rp[..., :half], rp[..., half:]
    cs = cos[None, :, None, :].astype(t.dtype)
    sn = sin[None, :, None, :].astype(t.dtype)
    return jnp.concatenate([nope, r1 * cs - r2 * sn, r2 * cs + r1 * sn], axis=-1)


def _gla_chunked(q, k, v, logf, s0):
    b_, h_, L, _ = q.shape
    n = L // HG_CHUNK

    def chunks(t):
        return t.reshape(b_, h_, n, HG_CHUNK, t.shape[-1]).transpose(2, 0, 1, 3, 4)

    mask = jnp.tril(jnp.ones((HG_CHUNK, HG_CHUNK), dtype=bool))[..., None]

    def step(S, inp):
        qb, kb, vb, fb = inp
        bcum = jnp.cumsum(fb, axis=2)
        diff = bcum[:, :, :, None, :] - bcum[:, :, None, :, :]
        decay = jnp.exp(jnp.where(mask, diff, -jnp.inf))
        A = jnp.einsum('bhtd,bhsd,bhtsd->bhts', qb, kb, decay)
        o = jnp.einsum('bhts,bhsv->bhtv', A, vb) + jnp.einsum('bhtd,bhdv->bhtv', qb * jnp.exp(bcum), S)
        btot = bcum[:, :, -1:, :]
        S_new = S * jnp.exp(btot[:, :, 0, :, None]) + jnp.einsum('bhsd,bhsv->bhdv', kb * jnp.exp(btot - bcum), vb)
        return S_new, o

    S_fin, oc = lax.scan(step, s0, tuple(chunks(t) for t in (q, k, v, logf)))
    o = oc.transpose(1, 2, 0, 3, 4).reshape(b_, h_, L, v.shape[-1])
    return o, S_fin


def _hgrn2_branch(px, pc, lb, gain, need_ctx):
    qx, fxf, fxb, ix, gx = px
    qc, fcf, fcb, ic, gc = pc
    dt = qx.dtype
    bsz = qx.shape[0]

    def heads(t):
        b_, L, _ = t.shape
        return t.astype(jnp.float32).reshape(b_, L, HG_HEADS, -1).transpose(0, 2, 1, 3)

    def decay(z, lbd):
        z = z.astype(jnp.float32)
        logf = jnp.logaddexp(jnp.log(lbd), jnp.log1p(-lbd) + jax.nn.log_sigmoid(z))
        key = (1.0 - lbd) * jax.nn.sigmoid(-z)
        return heads(logf), heads(key)

    qxh, vxh, qch, vch = heads(qx), heads(ix), heads(qc), heads(ic)
    s0 = jnp.zeros((bsz, HG_HEADS, HG_DK, HG_DV), jnp.float32)
    ox_dirs, oc_dirs = [], []
    for d, (zx, zc) in enumerate(((fxf, fcf), (fxb, fcb))):
        lfx, kx = decay(zx, lb[d])
        lfc, kc = decay(zc, lb[d])
        tx = (qxh, kx, vxh, lfx)
        tc = (qch, kc, vch, lfc)
        if d == 1:
            tx = tuple(jnp.flip(t, axis=2) for t in tx)
            tc = tuple(jnp.flip(t, axis=2) for t in tc)
        oc, sc = _gla_chunked(*tc, s0)
        ox, _ = _gla_chunked(*tx, sc)
        if d == 1:
            ox, oc = jnp.flip(ox, axis=2), jnp.flip(oc, axis=2)
        ox_dirs.append(ox)
        oc_dirs.append(oc)

    def readout(dirs, g):
        o = (dirs[0] + dirs[1]).transpose(0, 2, 1, 3)
        b_, L = o.shape[0], o.shape[1]
        o = _rmsnorm(o, gain).reshape(b_, L, HG_W)
        return (o * jax.nn.sigmoid(g.astype(jnp.float32))).astype(dt)

    out_x = readout(ox_dirs, gx)
    out_c = readout(oc_dirs, gc) if need_ctx else None
    return out_x, out_c


def _mla_qkv(dq, dkv, kr, q_norm, kv_norm, w_uq, w_ukv, gq, gk):
    b_, L, _ = dq.shape
    q = (_rmsnorm(dq, q_norm) @ w_uq).reshape(b_, L, MLA_HEADS, MLA_QK)
    kv = (_rmsnorm(dkv, kv_norm) @ w_ukv).reshape(b_, L, MLA_HEADS, MLA_NOPE + MLA_V)
    k_nope, v = kv[..., :MLA_NOPE], kv[..., MLA_NOPE:]
    k_rope = jnp.broadcast_to(kr[:, :, None, :], (b_, L, MLA_HEADS, MLA_ROPE))
    k = jnp.concatenate([k_nope, k_rope], axis=-1)
    return _rmsnorm(q, gq), _rmsnorm(k, gk), v


def _mla_branch(px, pc, cos, sin, q_norm, kv_norm, w_uq, w_ukv, gq, gk, need_ctx):
    dqx, dkvx, krx = px
    dqc, dkvc, krc = pc
    scale = MLA_QK ** -0.5
    qx, kx, vx = _mla_qkv(dqx, dkvx, krx, q_norm, kv_norm, w_uq, w_ukv, gq, gk)
    qx, kx = _apply_rope(qx, cos, sin), _apply_rope(kx, cos, sin)
    qc, kc, vc = _mla_qkv(dqc, dkvc, krc, q_norm, kv_norm, w_uq, w_ukv, gq, gk)
    k_all = jnp.concatenate([kx, kc], axis=1)
    v_all = jnp.concatenate([vx, vc], axis=1)
    b_, S = qx.shape[0], qx.shape[1]
    nb = S // Q_BLOCK
    qb = qx.reshape(b_, nb, Q_BLOCK, MLA_HEADS, MLA_QK).transpose(1, 0, 2, 3, 4)

    def attend(qblk):
        s = jnp.einsum('bqhd,bkhd->bhqk', qblk, k_all).astype(jnp.float32) * scale
        p = jax.nn.softmax(s, axis=-1).astype(v_all.dtype)
        return jnp.einsum('bhqk,bkhd->bqhd', p, v_all)

    ox = lax.map(attend, qb).transpose(1, 0, 2, 3, 4).reshape(b_, S, MLA_W)
    oc = None
    if need_ctx:
        s = jnp.einsum('bqhd,bkhd->bhqk', qc, kc).astype(jnp.float32) * scale
        p = jax.nn.softmax(s, axis=-1).astype(vc.dtype)
        oc = jnp.einsum('bhqk,bkhd->bqhd', p, vc).reshape(b_, qc.shape[1], MLA_W)
    return ox, oc


def _conv_centred(t, w, b):
    L = t.shape[1]
    left = CONV_W // 2
    tp = jnp.pad(t, ((0, 0), (left, CONV_W - 1 - left), (0, 0)))
    out = jnp.broadcast_to(b, t.shape).astype(t.dtype)
    for j in range(CONV_W):
        out = out + tp[:, j:j + L] * w[j]
    return out


def _rglru_gates(u, wa, ba, wx, bx, lam):
    b_, L, _ = u.shape
    ub = u.reshape(b_, L, LRU_BLOCKS, LRU_BD)
    r = jax.nn.sigmoid(jnp.einsum('blnd,nde->blne', ub, wa.astype(jnp.float32)).reshape(b_, L, LRU_W) + ba)
    ig = jax.nn.sigmoid(jnp.einsum('blnd,nde->blne', ub, wx.astype(jnp.float32)).reshape(b_, L, LRU_W) + bx)
    log_a = -LRU_C * r * jax.nn.softplus(-lam.astype(jnp.float32))
    a = jnp.exp(log_a)
    inp = jnp.sqrt(-jnp.expm1(2.0 * log_a)) * (ig * u)
    return a, inp


def _linear_scan(a, b, h0):
    def comb(l, r):
        return l[0] * r[0], r[0] * l[1] + r[1]
    A, H = lax.associative_scan(comb, (a, b), axis=1)
    return H + A * h0[:, None, :]


def _rglru_branch(px, pc, conv_w, conv_b, wa, ba, wx, bx, lam, need_ctx):
    xx, yx = px
    xc, yc = pc
    dt = xx.dtype
    ux = _conv_centred(xx, conv_w, conv_b).astype(jnp.float32)
    uc = _conv_centred(xc, conv_w, conv_b).astype(jnp.float32)
    hx_dirs, hc_dirs = [], []
    for d in range(2):
        ax, bxs = _rglru_gates(ux, wa[d], ba[d], wx[d], bx[d], lam[d])
        ac, bcs = _rglru_gates(uc, wa[d], ba[d], wx[d], bx[d], lam[d])
        if d == 1:
            ax, bxs, ac, bcs = (jnp.flip(t, axis=1) for t in (ax, bxs, ac, bcs))
        hc = _linear_scan(ac, bcs, jnp.zeros_like(uc[:, 0]))
        hx = _linear_scan(ax, bxs, hc[:, -1])
        if d == 1:
            hx, hc = jnp.flip(hx, axis=1), jnp.flip(hc, axis=1)
        hx_dirs.append(hx)
        hc_dirs.append(hc)
    out_x = ((hx_dirs[0] + hx_dirs[1]) * jax.nn.gelu(yx.astype(jnp.float32))).astype(dt)
    out_c = None
    if need_ctx:
        out_c = ((hc_dirs[0] + hc_dirs[1]) * jax.nn.gelu(yc.astype(jnp.float32))).astype(dt)
    return out_x, out_c


def _merge(o_hg, o_mla, o_lru, g_hg, g_mla, g_lru, w_br_hg, w_br_mla, w_br_lru, w_out):
    y = (jax.nn.sigmoid(g_hg) * (o_hg @ w_br_hg)
         + jax.nn.sigmoid(g_mla) * (o_mla @ w_br_mla)
         + jax.nn.sigmoid(g_lru) * (o_lru @ w_br_lru))
    return y @ w_out


def _hier_moe(h, w_rg, b_rg, w_re, b_re, w1, w3, w2):
    shp = h.shape
    t = h.reshape(-1, shp[-1])
    n = t.shape[0]
    lg = (t @ w_rg).astype(jnp.float32)
    g_sel = jnp.argmax(lg + b_rg.astype(jnp.float32), axis=-1)
    g_oh = jax.nn.one_hot(g_sel, N_GROUPS, dtype=jnp.float32)
    p_g = jnp.sum(jax.nn.softmax(lg, axis=-1) * g_oh, axis=-1, keepdims=True)
    le = (t @ w_re).astype(jnp.float32).reshape(n, N_GROUPS, EXP_PER_GROUP)
    le_g = jnp.einsum('ng,nge->ne', g_oh, le)
    bias_g = jnp.einsum('ng,ge->ne', g_oh, b_re.astype(jnp.float32).reshape(N_GROUPS, EXP_PER_GROUP))
    _, idx = lax.top_k(le_g + bias_g, TOP_K_IN_GROUP)
    w_sel = jax.nn.softmax(jnp.take_along_axis(le_g, idx, axis=1), axis=-1) * p_g
    e_sel = g_sel[:, None] * EXP_PER_GROUP + idx
    combine = jnp.einsum('nk,nke->ne', w_sel, jax.nn.one_hot(e_sel, N_EXPERTS, dtype=jnp.float32)).astype(t.dtype)
    out = jnp.zeros_like(t)
    for e in range(N_EXPERTS):
        hid = jax.nn.silu(t @ w1[e]) * (t @ w3[e])
        out = out + combine[:, e:e + 1] * (hid @ w2[e])
    return out.reshape(shp)


def setup_inputs(seed: int = 0) -> dict:
    key = jax.random.key(seed)
    ks = iter(jax.random.split(key, 48))

    def nrm(shape, scale):
        return jax.random.normal(next(ks), shape, jnp.float32) * scale

    def gain(shape):
        return 1.0 + nrm(shape, 0.02)

    L = DEPTH
    a0 = jax.random.uniform(next(ks), (L, 2, LRU_W), jnp.float32, 0.9, 0.999)
    sig = a0 ** (1.0 / LRU_C)
    return {
        'x': nrm((BATCH, SEQ, D_MODEL), 1.0),
        'c': nrm((BATCH, D_MODEL), 1.0),
        'ctx': nrm((BATCH, CTX_LEN, D_MODEL), 1.0),
        'c_ctx': nrm((D_MODEL,), 1.0),
        'w_mod': nrm((L, D_MODEL, 6 * D_MODEL), 0.5 * D_MODEL ** -0.5),
        'b_mod': nrm((L, 6 * D_MODEL), 0.02),
        'norm_mix': gain((L, D_MODEL)),
        'norm_ffn': gain((L, D_MODEL)),
        'w_in': nrm((L, D_MODEL, IN_WIDTH), D_MODEL ** -0.5),
        'hg_lb': nrm((L, 2, HG_W), 1.0),
        'hg_norm': gain((L, HG_DV)),
        'mla_q_norm': gain((L, MLA_Q_RANK)),
        'mla_kv_norm': gain((L, MLA_KV_RANK)),
        'mla_w_uq': nrm((L, MLA_Q_RANK, MLA_HEADS * MLA_QK), MLA_Q_RANK ** -0.5),
        'mla_w_ukv': nrm((L, MLA_KV_RANK, MLA_HEADS * (MLA_NOPE + MLA_V)), MLA_KV_RANK ** -0.5),
        'mla_qk_gain_q': gain((L, MLA_QK)),
        'mla_qk_gain_k': gain((L, MLA_QK)),
        'lru_conv_w': nrm((L, CONV_W, LRU_W), CONV_W ** -0.5),
        'lru_conv_b': nrm((L, LRU_W), 0.02),
        'lru_wa': nrm((L, 2, LRU_BLOCKS, LRU_BD, LRU_BD), LRU_BD ** -0.5),
        'lru_ba': nrm((L, 2, LRU_W), 0.02),
        'lru_wx': nrm((L, 2, LRU_BLOCKS, LRU_BD, LRU_BD), LRU_BD ** -0.5),
        'lru_bx': nrm((L, 2, LRU_W), 0.02),
        'lru_lambda': jnp.log(sig) - jnp.log1p(-sig),
        'w_br_hg': nrm((L, HG_W, D_MODEL), HG_W ** -0.5),
        'w_br_mla': nrm((L, MLA_W, D_MODEL), MLA_W ** -0.5),
        'w_br_lru': nrm((L, LRU_W, D_MODEL), LRU_W ** -0.5),
        'w_out': nrm((L, D_MODEL, D_MODEL), D_MODEL ** -0.5),
        'moe_w_rg': nrm((L, D_MODEL, N_GROUPS), D_MODEL ** -0.5),
        'moe_b_rg': nrm((L, N_GROUPS), 0.01),
        'moe_w_re': nrm((L, D_MODEL, N_EXPERTS), D_MODEL ** -0.5),
        'moe_b_re': nrm((L, N_EXPERTS), 0.01),
        'moe_w1': nrm((L, N_EXPERTS, D_MODEL, D_EXPERT), D_MODEL ** -0.5),
        'moe_w3': nrm((L, N_EXPERTS, D_MODEL, D_EXPERT), D_MODEL ** -0.5),
        'moe_w2': nrm((L, N_EXPERTS, D_EXPERT, D_MODEL), D_EXPERT ** -0.5),
    }


def reference(x, c, ctx, c_ctx, w_mod, b_mod, norm_mix, norm_ffn, w_in, hg_lb, hg_norm,
              mla_q_norm, mla_kv_norm, mla_w_uq, mla_w_ukv, mla_qk_gain_q, mla_qk_gain_k,
              lru_conv_w, lru_conv_b, lru_wa, lru_ba, lru_wx, lru_bx, lru_lambda,
              w_br_hg, w_br_mla, w_br_lru, w_out,
              moe_w_rg, moe_b_rg, moe_w_re, moe_b_re, moe_w1, moe_w3, moe_w2):
    rows = x.shape[1] // GRID_W
    cos, sin = _axial_rope(rows)
    lb_cs = jnp.cumsum(jax.nn.softmax(hg_lb.astype(jnp.float32), axis=0), axis=0)
    lb_all = lb_cs - lb_cs[0:1]
    xc = ctx
    for l in range(DEPTH):
        need_ctx = l < DEPTH - 1
        mx = jax.nn.silu(c) @ w_mod[l] + b_mod[l]
        mc = jax.nn.silu(c_ctx) @ w_mod[l] + b_mod[l]
        sh1x, sc1x, g1x, sh2x, sc2x, g2x = (m[:, None, :] for m in jnp.split(mx, 6, axis=-1))
        sh1c, sc1c, g1c, sh2c, sc2c, g2c = jnp.split(mc, 6, axis=-1)

        hx = _modulate(_rmsnorm(x, norm_mix[l]), sh1x, sc1x)
        hc = _modulate(_rmsnorm(xc, norm_mix[l]), sh1c, sc1c)
        px = _split_in(hx @ w_in[l])
        pc = _split_in(hc @ w_in[l])

        o_hg_x, o_hg_c = _hgrn2_branch(px[0:5], pc[0:5], lb_all[l], hg_norm[l], need_ctx)
        o_mla_x, o_mla_c = _mla_branch(px[5:8], pc[5:8], cos, sin, mla_q_norm[l], mla_kv_norm[l],
                                       mla_w_uq[l], mla_w_ukv[l], mla_qk_gain_q[l], mla_qk_gain_k[l], need_ctx)
        o_lru_x, o_lru_c = _rglru_branch(px[8:10], pc[8:10], lru_conv_w[l], lru_conv_b[l], lru_wa[l], lru_ba[l],
                                         lru_wx[l], lru_bx[l], lru_lambda[l], need_ctx)

        yx = _merge(o_hg_x, o_mla_x, o_lru_x, px[10], px[11], px[12], w_br_hg[l], w_br_mla[l], w_br_lru[l], w_out[l])
        x = x + g1x * yx
        if need_ctx:
            yc = _merge(o_hg_c, o_mla_c, o_lru_c, pc[10], pc[11], pc[12], w_br_hg[l], w_br_mla[l], w_br_lru[l], w_out[l])
            xc = xc + g1c * yc

        hx = _modulate(_rmsnorm(x, norm_ffn[l]), sh2x, sc2x)
        x = x + g2x * _hier_moe(hx, moe_w_rg[l], moe_b_rg[l], moe_w_re[l], moe_b_re[l], moe_w1[l], moe_w3[l], moe_w2[l])
        if need_ctx:
            hc = _modulate(_rmsnorm(xc, norm_ffn[l]), sh2c, sc2c)
            xc = xc + g2c * _hier_moe(hc, moe_w_rg[l], moe_b_rg[l], moe_w_re[l], moe_b_re[l], moe_w1[l], moe_w3[l], moe_w2[l])
    return x
```

```python
import functools

import numpy as np
import jax
import jax.numpy as jnp
from jax import lax
from jax.experimental import pallas as pl
from jax.experimental.pallas import tpu as pltpu

F32 = jnp.float32
BF16 = jnp.bfloat16
HIGHEST = lax.Precision.HIGHEST

D_MODEL = 1024
GRID_W = 64
EPS = 1e-6

HG_HEADS = 8
HG_DK = 64
HG_W = 512
HG_CHUNK = 64
HG_MID = HG_CHUNK // 2
HG_MAX_LOG_RANGE = 60.0

MLA_HEADS = 8
MLA_Q_RANK = 256
MLA_KV_RANK = 128
MLA_NOPE = 64
MLA_ROPE = 32
MLA_V = 64
MLA_QK = MLA_NOPE + MLA_ROPE
MLA_W = MLA_HEADS * MLA_V
HEAD_PAD = 128
ROPE_BASE = 10000.0

LRU_W = 512
LRU_BLOCKS = 8
LRU_BD = LRU_W // LRU_BLOCKS
CONV_W = 4
LRU_C = 8.0

N_GROUPS = 4
EXP_PER_GROUP = 4
N_EXPERTS = N_GROUPS * EXP_PER_GROUP
D_EXPERT = 256
EXPERT_STEPS = 4
ROUTER_PAD = 128
MOE_TM = 768

TB = 256
HALO = 16
P_WIDTH = 6144
F_WIDTH = 1024
VMEM_LIMIT = 56 * 1024 * 1024

NT_DIMS = (((1,), (1,)), ((), ()))


def _cparams(sem):
    return pltpu.CompilerParams(dimension_semantics=sem, vmem_limit_bytes=VMEM_LIMIT)


def _const_spec(shape):
    nd = len(shape)
    return pl.BlockSpec(shape, lambda *_: (0,) * nd, pipeline_mode=pl.Buffered(1))


def _sigmoid(t):
    return 1.0 / (1.0 + jnp.exp(-t))


def _scan_block(j, d, nctx, nblk):
    fwd = j
    bwd = jnp.where(j < nctx, nctx - 1 - j, nblk - 1 - (j - nctx))
    return jnp.where(d == 0, fwd, bwd)


def _mod_kernel(c_ref, w_ref, b_ref, o_ref):
    c = c_ref[...]
    s = c * _sigmoid(c)
    o_ref[...] = jnp.dot(s, w_ref[...], precision=HIGHEST, preferred_element_type=F32) + b_ref[...]


def _modulation(cvec, w_mod, b_mod):
    n = w_mod.shape[1]
    tn = 1024
    return pl.pallas_call(
        _mod_kernel,
        grid=(n // tn,),
        in_specs=[pl.BlockSpec((8, D_MODEL), lambda j: (0, 0)),
                  pl.BlockSpec((D_MODEL, tn), lambda j: (0, j)),
                  pl.BlockSpec((1, tn), lambda j: (0, j))],
        out_specs=pl.BlockSpec((8, tn), lambda j: (0, j)),
        out_shape=jax.ShapeDtypeStruct((8, n), F32),
        compiler_params=_cparams(("arbitrary",)),
        name="modulation",
    )(cvec, w_mod, b_mod.reshape(1, n))


def _row_mods(i, tm, ctx_len, mctx_ref, mb_ref, k):
    row = i * tm + lax.broadcasted_iota(jnp.int32, (tm, 1), 0)
    return jnp.where(row < ctx_len, mctx_ref[k:k + 1, :], mb_ref[0, k:k + 1, :])


def _norm_modulate(x, gain, shift, scale):
    ms = jnp.mean(x * x, axis=-1, keepdims=True)
    xn = x * lax.rsqrt(ms + EPS) * gain
    return xn * (1.0 + scale) + shift


def _inproj_kernel(x_ref, mctx_ref, mb_ref, gain_ref, w_ref, f_ref, p_ref, *, ctx_len):
    i = pl.program_id(1)
    shift = _row_mods(i, TB, ctx_len, mctx_ref, mb_ref, 0)
    scale = _row_mods(i, TB, ctx_len, mctx_ref, mb_ref, 1)
    h = _norm_modulate(x_ref[0], gain_ref[...], shift, scale).astype(BF16)
    f_ref[0] = jnp.dot(h, w_ref[:, 0:F_WIDTH], preferred_element_type=F32)
    cw = 512
    for j in range(P_WIDTH // cw):
        lo = F_WIDTH + j * cw
        p_ref[0, :, j * cw:(j + 1) * cw] = jnp.dot(
            h, w_ref[:, lo:lo + cw], preferred_element_type=F32).astype(BF16)


def _inproj(xs, mctx, mb, gain, w_all, ctx_len):
    b, lt, _ = xs.shape
    return pl.pallas_call(
        functools.partial(_inproj_kernel, ctx_len=ctx_len),
        grid=(b, lt // TB),
        in_specs=[pl.BlockSpec((1, TB, D_MODEL), lambda bi, i: (bi, i, 0)),
                  _const_spec((6, D_MODEL)),
                  pl.BlockSpec((1, 6, D_MODEL), lambda bi, i: (bi, 0, 0)),
                  _const_spec((1, D_MODEL)),
                  _const_spec((D_MODEL, F_WIDTH + P_WIDTH))],
        out_specs=[pl.BlockSpec((1, TB, F_WIDTH), lambda bi, i: (bi, i, 0)),
                   pl.BlockSpec((1, TB, P_WIDTH), lambda bi, i: (bi, i, 0))],
        out_shape=[jax.ShapeDtypeStruct((b, lt, F_WIDTH), F32),
                   jax.ShapeDtypeStruct((b, lt, P_WIDTH), BF16)],
        compiler_params=_cparams(("parallel", "arbitrary")),
        name="inproj",
    )(xs, mctx, mb, gain, w_all)


def _hgrn_kernel(q_ref, v_ref, f_ref, tri_ref, lbc_ref, bd_ref, o_ref, st_ref, c_scr, k_scr, v_scr):
    d = pl.program_id(1)
    j = pl.program_id(2)
    nch = TB // HG_CHUNK

    @pl.when(j == 0)
    def _():
        st_ref[...] = jnp.zeros_like(st_ref)

    tri = tri_ref[0]
    log_lb = lbc_ref[0, 0:1, :]
    log_1mlb = lbc_ref[0, 1:2, :]
    one_mlb = lbc_ref[0, 2:3, :]
    ri = lax.broadcasted_iota(jnp.int32, (HG_CHUNK, HG_CHUNK), 0)
    ci = lax.broadcasted_iota(jnp.int32, (HG_CHUNK, HG_CHUNK), 1)
    mask = jnp.where(d == 0, ri - ci, ci - ri) >= 0
    rowi = lax.broadcasted_iota(jnp.int32, (HG_CHUNK, 1), 0)

    def chunk_body(cidx, carry):
        cc = jnp.where(d == 0, cidx, nch - 1 - cidx)
        r0 = pl.multiple_of(cc * HG_CHUNK, HG_CHUNK)
        rows = pl.ds(r0, HG_CHUNK)
        z = f_ref[0, rows, :]
        q = q_ref[0, rows, :].astype(F32)
        v = v_ref[0, rows, :]
        e = jnp.exp(-jnp.abs(z))
        log_sig = jnp.minimum(z, 0.0) - jnp.log1p(e)
        t = log_1mlb + log_sig
        logf = jnp.maximum(log_lb, t) + jnp.log1p(jnp.exp(-jnp.abs(log_lb - t)))
        k = one_mlb * (jnp.where(z >= 0, e, 1.0) / (1.0 + e))
        c = jnp.dot(tri, logf, precision=HIGHEST, preferred_element_type=F32)
        c_mid = c[HG_MID:HG_MID + 1, :]
        c_first = jnp.where(d == 0, c[0:1, :], c[HG_CHUNK - 1:HG_CHUNK, :])
        c_end = jnp.where(d == 0, c[HG_CHUNK - 1:HG_CHUNK, :], c[0:1, :])
        span = jnp.max(jnp.maximum(c_first - c_mid, c_mid - c_end))

        qs = (q * jnp.exp(c)).astype(BF16)
        ks = (k * jnp.exp(c_end - c)).astype(BF16)
        dec = jnp.exp(c_end)
        outs = []
        for h in range(HG_HEADS):
            sl = slice(HG_DK * h, HG_DK * (h + 1))
            st = st_ref[h]
            outs.append(lax.dot_general(qs[:, sl], st.astype(BF16), NT_DIMS, preferred_element_type=F32))
            vt = v[:, sl].astype(F32).T.astype(BF16)
            st_ref[h] = st * dec[:, sl] + jnp.dot(vt, ks[:, sl], preferred_element_type=F32)
        o_ref[0, 0, rows, :] = jnp.concatenate(outs, axis=-1)

        @pl.when(span <= HG_MAX_LOG_RANGE)
        def _matmul_form():
            qd = (q * jnp.exp(c - c_mid)).astype(BF16)
            kd = (k * jnp.exp(c_mid - c)).astype(BF16)
            intra = []
            for h in range(HG_HEADS):
                sl = slice(HG_DK * h, HG_DK * (h + 1))
                a = lax.dot_general(qd[:, sl], kd[:, sl], NT_DIMS, preferred_element_type=F32)
                a = jnp.where(mask, a, 0.0).astype(BF16)
                intra.append(jnp.dot(a, v[:, sl], preferred_element_type=F32))
            o_ref[0, 0, rows, :] += jnp.concatenate(intra, axis=-1)

        @pl.when(span > HG_MAX_LOG_RANGE)
        def _exact_form():
            c_scr[...] = c
            k_scr[...] = k
            v_scr[...] = v.astype(F32)
            bd = bd_ref[...]

            def key_row(s, acc):
                c_s = c_scr[pl.ds(s, 1), :]
                w = q * jnp.exp(jnp.minimum(c - c_s, 0.0)) * k_scr[pl.ds(s, 1), :]
                ws = jnp.dot(w, bd, precision=HIGHEST, preferred_element_type=F32)
                valid = jnp.where(d == 0, rowi - s, s - rowi) >= 0
                return acc + jnp.where(valid, ws, 0.0) * v_scr[pl.ds(s, 1), :]

            o_ref[0, 0, rows, :] += lax.fori_loop(0, HG_CHUNK, key_row, jnp.zeros((HG_CHUNK, HG_W), F32))

        return carry

    lax.fori_loop(0, nch, chunk_body, 0)


def _hgrn(p, f, tri, lbc, bd, nctx):
    b, lt, _ = p.shape
    nblk = lt // TB

    def blk(bi, d, j):
        return _scan_block(j, d, nctx, nblk)

    return pl.pallas_call(
        _hgrn_kernel,
        grid=(b, 2, nblk),
        in_specs=[pl.BlockSpec((1, TB, HG_W), lambda bi, d, j: (bi, blk(bi, d, j), 0)),
                  pl.BlockSpec((1, TB, HG_W), lambda bi, d, j: (bi, blk(bi, d, j), 1)),
                  pl.BlockSpec((1, TB, HG_W), lambda bi, d, j: (bi, blk(bi, d, j), d)),
                  pl.BlockSpec((1, HG_CHUNK, HG_CHUNK), lambda bi, d, j: (d, 0, 0)),
                  pl.BlockSpec((1, 8, HG_W), lambda bi, d, j: (d, 0, 0)),
                  _const_spec((HG_W, HG_W))],
        out_specs=pl.BlockSpec((1, 1, TB, HG_W), lambda bi, d, j: (d, bi, blk(bi, d, j), 0)),
        out_shape=jax.ShapeDtypeStruct((2, b, lt, HG_W), F32),
        scratch_shapes=[pltpu.VMEM((HG_HEADS, HG_DK, HG_DK), F32),
                        pltpu.VMEM((HG_CHUNK, HG_W), F32),
                        pltpu.VMEM((HG_CHUNK, HG_W), F32),
                        pltpu.VMEM((HG_CHUNK, HG_W), F32)],
        compiler_params=_cparams(("parallel", "arbitrary", "arbitrary")),
        name="hgrn2",
    )(p, p, f, tri, lbc, bd)


def _rms_rows(t, width):
    return lax.rsqrt(jnp.sum(t * t, axis=-1, keepdims=True) * (1.0 / width) + EPS)


def _mla_prep_kernel(pm_ref, cos_ref, sin_ref, qn_ref, kvn_ref, gq_ref, gk_ref,
                     wq1_ref, wq2_ref, wk1_ref, sk1_ref, sk2_ref, wv_ref,
                     q_ref, k_ref, v_ref):
    pm = pm_ref[0]
    dq = pm[:, 0:MLA_Q_RANK].astype(F32)
    dkv = pm[:, MLA_Q_RANK:MLA_Q_RANK + MLA_KV_RANK].astype(F32)
    dqn = (dq * _rms_rows(dq, MLA_Q_RANK) * qn_ref[...]).astype(BF16)
    dkvn = (dkv * _rms_rows(dkv, MLA_KV_RANK) * kvn_ref[...]).astype(BF16)
    q1 = jnp.dot(dqn, wq1_ref[...], preferred_element_type=F32)
    q2 = jnp.dot(dqn, wq2_ref[...], preferred_element_type=F32)
    k1 = (jnp.dot(dkvn, wk1_ref[...], preferred_element_type=F32)
          + jnp.dot(pm, sk1_ref[...], preferred_element_type=F32))
    k2 = jnp.dot(pm, sk2_ref[...], preferred_element_type=F32)
    vv = jnp.dot(dkvn, wv_ref[...], preferred_element_type=F32)
    cos = cos_ref[...]
    sin = sin_ref[...]
    gq_c = gq_ref[0:1, :] * cos
    gq_s = gq_ref[1:2, :] * sin
    gk_c = gk_ref[0:1, :] * cos
    gk_s = gk_ref[1:2, :] * sin
    scale = MLA_QK ** -0.5
    for h in range(MLA_HEADS):
        sl = slice(HEAD_PAD * h, HEAD_PAD * (h + 1))
        qh = q1[:, sl]
        q_ref[0, h] = ((qh * gq_c + q2[:, sl] * gq_s) * (_rms_rows(qh, MLA_QK) * scale)).astype(BF16)
        kh = k1[:, sl]
        k_ref[0, h] = ((kh * gk_c + k2[:, sl] * gk_s) * _rms_rows(kh, MLA_QK)).astype(BF16)
        v_ref[0, h] = vv[:, sl].astype(BF16)


def _mla_prep(p, cos_t, sin_t, qn, kvn, gq, gk, wq1, wq2, wk1, sk1, sk2, wv):
    b, lt, _ = p.shape
    hw = MLA_HEADS * HEAD_PAD
    head_spec = pl.BlockSpec((1, MLA_HEADS, TB, HEAD_PAD), lambda bi, i: (bi, 0, i, 0))
    head_shape = jax.ShapeDtypeStruct((b, MLA_HEADS, lt, HEAD_PAD), BF16)
    return pl.pallas_call(
        _mla_prep_kernel,
        grid=(b, lt // TB),
        in_specs=[pl.BlockSpec((1, TB, 512), lambda bi, i: (bi, i, 3)),
                  pl.BlockSpec((TB, HEAD_PAD), lambda bi, i: (i, 0)),
                  pl.BlockSpec((TB, HEAD_PAD), lambda bi, i: (i, 0)),
                  _const_spec((1, MLA_Q_RANK)), _const_spec((1, MLA_KV_RANK)),
                  _const_spec((2, HEAD_PAD)), _const_spec((2, HEAD_PAD)),
                  _const_spec((MLA_Q_RANK, hw)), _const_spec((MLA_Q_RANK, hw)),
                  _const_spec((MLA_KV_RANK, hw)), _const_spec((512, hw)), _const_spec((512, hw)),
                  _const_spec((MLA_KV_RANK, hw))],
        out_specs=[head_spec, head_spec, head_spec],
        out_shape=[head_shape, head_shape, head_shape],
        compiler_params=_cparams(("parallel", "arbitrary")),
        name="mla_prep",
    )(p, cos_t, sin_t, qn, kvn, gq, gk, wq1, wq2, wk1, sk1, sk2, wv)


def _attend(q, k_ref, v_ref, hh, nkeys, tk):
    tq = q.shape[0]

    def kv_step(kb, carry):
        m, l, acc = carry
        r0 = pl.multiple_of(kb * tk, tk)
        kblk = k_ref[0, hh, pl.ds(r0, tk), :]
        vblk = v_ref[0, hh, pl.ds(r0, tk), :]
        s = lax.dot_general(q, kblk, NT_DIMS, preferred_element_type=F32)
        m_new = jnp.maximum(m, jnp.max(s, axis=-1, keepdims=True))
        alpha = jnp.exp(m - m_new)
        p = jnp.exp(s - m_new)
        l = alpha * l + jnp.sum(p, axis=-1, keepdims=True)
        acc = alpha * acc + jnp.dot(p.astype(BF16), vblk, preferred_element_type=F32)
        return m_new, l, acc

    init = (jnp.full((tq, 1), -jnp.inf, F32), jnp.zeros((tq, 1), F32), jnp.zeros((tq, HEAD_PAD), F32))
    m, l, acc = lax.fori_loop(0, nkeys // tk, kv_step, init)
    return acc / l


def _flash_kernel(q_ref, k_ref, v_ref, o_ref, *, ctx_len, lt, tk):
    i = pl.program_id(2)
    nctx = ctx_len // TB

    def run(nkeys, tkk):
        o = _attend(q_ref[0, 0], k_ref, v_ref, 0, nkeys, tkk) + _attend(q_ref[0, 1], k_ref, v_ref, 1, nkeys, tkk)
        o_ref[0] = o.astype(BF16)

    @pl.when(i < nctx)
    def _():
        run(ctx_len, TB)

    @pl.when(i >= nctx)
    def _():
        run(lt, tk)


def _flash(q, k, v, ctx_len):
    b, _, lt, _ = q.shape
    tk = next(t for t in (768, 512, 256) if lt % t == 0)
    kv_spec = pl.BlockSpec((1, 2, lt, HEAD_PAD), lambda bi, g, i: (bi, g, 0, 0))
    return pl.pallas_call(
        functools.partial(_flash_kernel, ctx_len=ctx_len, lt=lt, tk=tk),
        grid=(b, MLA_HEADS // 2, lt // TB),
        in_specs=[pl.BlockSpec((1, 2, TB, HEAD_PAD), lambda bi, g, i: (bi, g, i, 0)), kv_spec, kv_spec],
        out_specs=pl.BlockSpec((1, TB, HEAD_PAD), lambda bi, g, i: (bi, i, g)),
        out_shape=jax.ShapeDtypeStruct((b, lt, MLA_W), BF16),
        compiler_params=_cparams(("parallel", "parallel", "arbitrary")),
        name="mla_attention",
    )(q, k, v)


def _lru_kernel(x_ref, xp_ref, xn_ref, cw_ref, cb_ref, wg_ref, bg_ref, lam_ref, o_ref, xe_scr, h_scr, *, nctx, nblk):
    d = pl.program_id(1)
    j = pl.program_id(2)
    blk = _scan_block(j, d, nctx, nblk)

    @pl.when(j == 0)
    def _():
        h_scr[...] = jnp.zeros_like(h_scr)

    keep_prev = jnp.logical_and(blk != 0, blk != nctx)
    keep_next = jnp.logical_and(blk != nctx - 1, blk != nblk - 1)
    xe_scr[0:HALO, :] = jnp.where(keep_prev, xp_ref[0].astype(F32), 0.0)
    xe_scr[HALO:HALO + TB, :] = x_ref[0].astype(F32)
    xe_scr[HALO + TB:2 * HALO + TB, :] = jnp.where(keep_next, xn_ref[0].astype(F32), 0.0)
    left = CONV_W // 2
    u = jnp.broadcast_to(cb_ref[...], (TB, LRU_W))
    for tap in range(CONV_W):
        u = u + xe_scr[pl.ds(HALO - left + tap, TB), :] * cw_ref[tap:tap + 1, :]

    g = jnp.dot(u.astype(BF16), wg_ref[0], preferred_element_type=F32) + bg_ref[0, 0:1, :]
    r = _sigmoid(g[:, 0:LRU_W])
    ig = _sigmoid(g[:, LRU_W:2 * LRU_W])
    neg_lam = -lam_ref[0, 0:1, :]
    softplus = jnp.maximum(neg_lam, 0.0) + jnp.log1p(jnp.exp(-jnp.abs(neg_lam)))
    a = jnp.exp(-LRU_C * r * softplus)
    bb = jnp.sqrt(1.0 - a * a) * (ig * u)
    rowi = lax.broadcasted_iota(jnp.int32, (TB, 1), 0)

    def scan(forward):
        aa, hh = a, bb
        k = 1
        while k < TB:
            shift = k if forward else TB - k
            valid = (rowi >= k) if forward else (rowi < TB - k)
            a_sh = jnp.where(valid, pltpu.roll(aa, shift, 0), 1.0)
            h_sh = jnp.where(valid, pltpu.roll(hh, shift, 0), 0.0)
            hh = aa * h_sh + hh
            aa = aa * a_sh
            k *= 2
        hh = hh + aa * h_scr[...]
        o_ref[0, 0] = hh
        last = TB - 1 if forward else 0
        h_scr[...] = hh[last:last + 1, :]

    @pl.when(d == 0)
    def _():
        scan(True)

    @pl.when(d == 1)
    def _():
        scan(False)


def _lru(p, conv_w, conv_b, wg, bg, lam, nctx):
    b, lt, _ = p.shape
    nblk = lt // TB
    hpb = TB // HALO
    nh = lt // HALO

    def blk(d, j):
        return _scan_block(j, d, nctx, nblk)

    return pl.pallas_call(
        functools.partial(_lru_kernel, nctx=nctx, nblk=nblk),
        grid=(b, 2, nblk),
        in_specs=[pl.BlockSpec((1, TB, LRU_W), lambda bi, d, j: (bi, blk(d, j), 4)),
                  pl.BlockSpec((1, HALO, LRU_W), lambda bi, d, j: (bi, jnp.maximum(blk(d, j) * hpb - 1, 0), 4)),
                  pl.BlockSpec((1, HALO, LRU_W), lambda bi, d, j: (bi, jnp.minimum((blk(d, j) + 1) * hpb, nh - 1), 4)),
                  _const_spec((8, LRU_W)), _const_spec((1, LRU_W)),
                  pl.BlockSpec((1, LRU_W, 2 * LRU_W), lambda bi, d, j: (d, 0, 0)),
                  pl.BlockSpec((1, 8, 2 * LRU_W), lambda bi, d, j: (d, 0, 0)),
                  pl.BlockSpec((1, 8, LRU_W), lambda bi, d, j: (d, 0, 0))],
        out_specs=pl.BlockSpec((1, 1, TB, LRU_W), lambda bi, d, j: (d, bi, blk(d, j), 0)),
        out_shape=jax.ShapeDtypeStruct((2, b, lt, LRU_W), F32),
        scratch_shapes=[pltpu.VMEM((TB + 2 * HALO, LRU_W), F32), pltpu.VMEM((1, LRU_W), F32)],
        compiler_params=_cparams(("parallel", "arbitrary", "arbitrary")),
        name="rglru",
    )(p, p, p, conv_w, conv_b, wg, bg, lam)


def _gelu_tanh(t):
    return 0.5 * t * (1.0 + jnp.tanh(0.7978845608028654 * (t + 0.044715 * t * t * t)))


def _merge_kernel(x_ref, mctx_ref, mb_ref, hgf_ref, hgb_ref, hgg_ref, hgn_ref, hm_ref, mla_ref,
                  lf_ref, lb_ref, ly_ref, g1_ref, g2_ref, g3_ref, wbr_ref, wout_ref, o_ref, *, ctx_len):
    i = pl.program_id(1)
    o = hgf_ref[0, 0] + hgb_ref[0, 0]
    ms = jnp.dot((o * o).astype(BF16), hm_ref[...], preferred_element_type=F32)
    o_hg = o * lax.rsqrt(ms + EPS) * hgn_ref[...] * _sigmoid(hgg_ref[0].astype(F32))
    o_lru = (lf_ref[0, 0] + lb_ref[0, 0]) * _gelu_tanh(ly_ref[0].astype(F32))
    y = (_sigmoid(g1_ref[0].astype(F32)) * jnp.dot(o_hg.astype(BF16), wbr_ref[0], preferred_element_type=F32)
         + _sigmoid(g2_ref[0].astype(F32)) * jnp.dot(mla_ref[0], wbr_ref[1], preferred_element_type=F32)
         + _sigmoid(g3_ref[0].astype(F32)) * jnp.dot(o_lru.astype(BF16), wbr_ref[2], preferred_element_type=F32))
    gate = _row_mods(i, TB, ctx_len, mctx_ref, mb_ref, 2)
    o_ref[0] = x_ref[0] + gate * jnp.dot(y.astype(BF16), wout_ref[...], preferred_element_type=F32)


def _merge(xs, mctx, mb, o_hg, p, hg_gain, head_mean, o_mla, h_lru, w_br, w_out, ctx_len):
    b, lt, _ = xs.shape

    def pcol(width, c):
        return pl.BlockSpec((1, TB, width), lambda bi, i: (bi, i, c))

    def dirspec(d):
        return pl.BlockSpec((1, 1, TB, 512), lambda bi, i: (d, bi, i, 0))

    return pl.pallas_call(
        functools.partial(_merge_kernel, ctx_len=ctx_len),
        grid=(b, lt // TB),
        in_specs=[pl.BlockSpec((1, TB, D_MODEL), lambda bi, i: (bi, i, 0)),
                  _const_spec((6, D_MODEL)),
                  pl.BlockSpec((1, 6, D_MODEL), lambda bi, i: (bi, 0, 0)),
                  dirspec(0), dirspec(1), pcol(512, 2),
                  _const_spec((1, HG_W)), _const_spec((HG_W, HG_W)),
                  pl.BlockSpec((1, TB, MLA_W), lambda bi, i: (bi, i, 0)),
                  dirspec(0), dirspec(1), pcol(512, 5),
                  pcol(D_MODEL, 3), pcol(D_MODEL, 4), pcol(D_MODEL, 5),
                  _const_spec((3, 512, D_MODEL)), _const_spec((D_MODEL, D_MODEL))],
        out_specs=pl.BlockSpec((1, TB, D_MODEL), lambda bi, i: (bi, i, 0)),
        out_shape=jax.ShapeDtypeStruct((b, lt, D_MODEL), F32),
        compiler_params=_cparams(("parallel", "arbitrary")),
        name="merge",
    )(xs, mctx, mb, o_hg, o_hg, p, hg_gain, head_mean, o_mla, h_lru, h_lru, p, p, p, p, w_br, w_out)


def _first_index_of_max(vals, lane, valid):
    masked = jnp.where(valid, vals, -jnp.inf)
    m = jnp.max(masked, axis=-1, keepdims=True)
    idx = jnp.min(jnp.where(masked == m, lane, ROUTER_PAD), axis=-1, keepdims=True)
    return m, idx


def _router(h, wr, br):
    logits = jnp.dot(h, wr, precision=HIGHEST, preferred_element_type=F32)
    lane = lax.broadcasted_iota(jnp.int32, logits.shape, 1)
    biased = logits + br
    is_group = jnp.logical_and(lane >= N_EXPERTS, lane < N_EXPERTS + N_GROUPS)
    _, g_lane = _first_index_of_max(biased, lane, is_group)
    g_max, _ = _first_index_of_max(logits, lane, is_group)
    g_exp = jnp.where(is_group, jnp.exp(logits - g_max), 0.0)
    g_sel_logit = jnp.sum(jnp.where(lane == g_lane, logits, 0.0), axis=-1, keepdims=True)
    p_g = jnp.exp(g_sel_logit - g_max) / jnp.sum(g_exp, axis=-1, keepdims=True)
    in_group = jnp.right_shift(lane, 2) == (g_lane - N_EXPERTS)
    _, i1 = _first_index_of_max(biased, lane, in_group)
    _, i2 = _first_index_of_max(biased, lane, jnp.logical_and(in_group, lane != i1))
    l1 = jnp.sum(jnp.where(lane == i1, logits, 0.0), axis=-1, keepdims=True)
    l2 = jnp.sum(jnp.where(lane == i2, logits, 0.0), axis=-1, keepdims=True)
    lm = jnp.maximum(l1, l2)
    e1 = jnp.exp(l1 - lm)
    e2 = jnp.exp(l2 - lm)
    inv = p_g / (e1 + e2)
    return jnp.where(lane == i1, e1 * inv, 0.0) + jnp.where(lane == i2, e2 * inv, 0.0)


def _moe_kernel(x_ref, mctx_ref, mb_ref, gain_ref, wr_ref, br_ref, w13_ref, w2_ref, o_ref,
                h_scr, comb_scr, acc_scr, *, ctx_len, tm):
    i = pl.program_id(1)
    e = pl.program_id(2)
    epb = N_EXPERTS // EXPERT_STEPS
    hw = epb * D_EXPERT

    @pl.when(e == 0)
    def _():
        shift = _row_mods(i, tm, ctx_len, mctx_ref, mb_ref, 3)
        scale = _row_mods(i, tm, ctx_len, mctx_ref, mb_ref, 4)
        h = _norm_modulate(x_ref[0], gain_ref[...], shift, scale)
        h_scr[...] = h.astype(BF16)
        comb_scr[...] = _router(h, wr_ref[...], br_ref[...])
        acc_scr[...] = jnp.zeros_like(acc_scr)

    h = h_scr[...]
    h13 = jnp.dot(h, w13_ref[0], preferred_element_type=F32)
    h1 = h13[:, 0:hw]
    hid = h1 * _sigmoid(h1) * h13[:, hw:2 * hw]
    comb = comb_scr[...]
    lane = lax.broadcasted_iota(jnp.int32, comb.shape, 1)
    parts = []
    for k in range(epb):
        wk = jnp.sum(jnp.where(lane == e * epb + k, comb, 0.0), axis=-1, keepdims=True)
        parts.append(hid[:, k * D_EXPERT:(k + 1) * D_EXPERT] * wk)
    hid = jnp.concatenate(parts, axis=-1).astype(BF16)
    acc_scr[...] += jnp.dot(hid, w2_ref[0], preferred_element_type=F32)

    @pl.when(e == EXPERT_STEPS - 1)
    def _():
        gate = _row_mods(i, tm, ctx_len, mctx_ref, mb_ref, 5)
        o_ref[0] = x_ref[0] + gate * acc_scr[...]


def _moe(xs, mctx, mb, gain, wr, br, w13, w2, ctx_len):
    b, lt, _ = xs.shape
    tm = MOE_TM if lt % MOE_TM == 0 else TB
    epb = N_EXPERTS // EXPERT_STEPS
    return pl.pallas_call(
        functools.partial(_moe_kernel, ctx_len=ctx_len, tm=tm),
        grid=(b, lt // tm, EXPERT_STEPS),
        in_specs=[pl.BlockSpec((1, tm, D_MODEL), lambda bi, i, e: (bi, i, 0)),
                  _const_spec((6, D_MODEL)),
                  pl.BlockSpec((1, 6, D_MODEL), lambda bi, i, e: (bi, 0, 0)),
                  _const_spec((1, D_MODEL)),
                  _const_spec((D_MODEL, ROUTER_PAD)), _const_spec((1, ROUTER_PAD)),
                  pl.BlockSpec((1, D_MODEL, 2 * epb * D_EXPERT), lambda bi, i, e: (e, 0, 0)),
                  pl.BlockSpec((1, epb * D_EXPERT, D_MODEL), lambda bi, i, e: (e, 0, 0))],
        out_specs=pl.BlockSpec((1, tm, D_MODEL), lambda bi, i, e: (bi, i, 0)),
        out_shape=jax.ShapeDtypeStruct((b, lt, D_MODEL), F32),
        scratch_shapes=[pltpu.VMEM((tm, D_MODEL), BF16), pltpu.VMEM((tm, ROUTER_PAD), F32),
                        pltpu.VMEM((tm, D_MODEL), F32)],
        compiler_params=_cparams(("parallel", "arbitrary", "arbitrary")),
        name="moe",
    )(xs, mctx, mb, gain, wr, br, w13, w2)


def _rope_tables(ctx_len, seq):
    half = MLA_ROPE // 2
    rows = seq // GRID_W
    pos_row = jnp.repeat(jnp.arange(rows, dtype=F32), GRID_W)
    pos_col = jnp.tile(jnp.arange(GRID_W, dtype=F32), rows)
    inv = ROPE_BASE ** (-jnp.arange(0, half, 2, dtype=F32) / half)
    ang = jnp.concatenate([pos_row[:, None] * inv, pos_col[:, None] * inv], axis=-1)
    cos, sin = jnp.cos(ang), jnp.sin(ang)
    ones = jnp.ones((seq, MLA_NOPE), F32)
    zpad = jnp.zeros((seq, HEAD_PAD - MLA_QK), F32)
    cos_x = jnp.concatenate([ones, cos, cos, zpad], axis=-1)
    sin_x = jnp.concatenate([jnp.zeros((seq, MLA_NOPE), F32), -sin, sin, zpad], axis=-1)
    cos_c = jnp.concatenate([jnp.ones((ctx_len, MLA_QK), F32), jnp.zeros((ctx_len, HEAD_PAD - MLA_QK), F32)], axis=-1)
    sin_c = jnp.zeros((ctx_len, HEAD_PAD), F32)
    return jnp.concatenate([cos_c, cos_x], axis=0), jnp.concatenate([sin_c, sin_x], axis=0)


def _swap_rope_halves(t):
    half = MLA_ROPE // 2
    return jnp.concatenate([jnp.zeros_like(t[..., :MLA_NOPE]), t[..., MLA_NOPE + half:], t[..., MLA_NOPE:MLA_NOPE + half]],
                           axis=-1)


def _pad_heads(t):
    pad = [(0, 0)] * (t.ndim - 1) + [(0, HEAD_PAD - t.shape[-1])]
    t = jnp.pad(t, pad)
    return t.reshape(t.shape[:-2] + (t.shape[-2] * HEAD_PAD,))


def _mla_weights(w_uq, w_ukv, gq, gk):
    wq = w_uq.reshape(MLA_Q_RANK, MLA_HEADS, MLA_QK)
    wq1 = _pad_heads(wq).astype(BF16)
    wq2 = _pad_heads(_swap_rope_halves(wq)).astype(BF16)
    wkv = w_ukv.reshape(MLA_KV_RANK, MLA_HEADS, MLA_NOPE + MLA_V)
    wk1 = _pad_heads(wkv[..., :MLA_NOPE]).astype(BF16)
    kr0 = MLA_Q_RANK + MLA_KV_RANK
    eye = jnp.eye(MLA_ROPE, dtype=F32)
    place = jnp.concatenate([jnp.zeros((MLA_ROPE, MLA_NOPE), F32), eye], axis=-1)
    place = jnp.broadcast_to(place[:, None, :], (MLA_ROPE, MLA_HEADS, MLA_QK))
    zrow_a = jnp.zeros((kr0, MLA_HEADS * HEAD_PAD), F32)
    zrow_b = jnp.zeros((512 - kr0 - MLA_ROPE, MLA_HEADS * HEAD_PAD), F32)
    sk1 = jnp.concatenate([zrow_a, _pad_heads(place), zrow_b], axis=0).astype(BF16)
    sk2 = jnp.concatenate([zrow_a, _pad_heads(_swap_rope_halves(place)), zrow_b], axis=0).astype(BF16)
    wvv = wkv[..., MLA_NOPE:]
    zero = jnp.zeros_like(wvv)
    odd = (jnp.arange(MLA_HEADS) % 2 == 1)[None, :, None]
    wv = jnp.concatenate([jnp.where(odd, zero, wvv), jnp.where(odd, wvv, zero)], axis=-1)
    wv = wv.reshape(MLA_KV_RANK, MLA_HEADS * HEAD_PAD).astype(BF16)

    def gains(g):
        g1 = jnp.pad(g, (0, HEAD_PAD - MLA_QK))
        g2 = jnp.pad(_swap_rope_halves(g), (0, HEAD_PAD - MLA_QK))
        return jnp.stack([g1, g2], axis=0)

    return wq1, wq2, wk1, sk1, sk2, wv, gains(gq), gains(gk)


def _block_diag(w):
    n, a, bb = w.shape
    eye = jnp.eye(n, dtype=w.dtype)
    return (eye[:, None, :, None] * w[:, :, None, :]).reshape(n * a, n * bb)


def _pad_rows(t, rows=8):
    return jnp.pad(t, ((0, rows - t.shape[0]), (0, 0)))


def kernel(x, c, ctx, c_ctx, w_mod, b_mod, norm_mix, norm_ffn, w_in, hg_lb, hg_norm, mla_q_norm, mla_kv_norm, mla_w_uq, mla_w_ukv, mla_qk_gain_q, mla_qk_gain_k, lru_conv_w, lru_conv_b, lru_wa, lru_ba, lru_wx, lru_bx, lru_lambda, w_br_hg, w_br_mla, w_br_lru, w_out, moe_w_rg, moe_b_rg, moe_w_re, moe_b_re, moe_w1, moe_w3, moe_w2):
    bsz, seq, _ = x.shape
    ctx_len = ctx.shape[1]
    depth = w_in.shape[0]
    assert seq % TB == 0 and ctx_len % TB == 0 and seq % GRID_W == 0 and bsz < 8
    nctx = ctx_len // TB

    xs = jnp.concatenate([ctx, x], axis=1)
    cos_t, sin_t = _rope_tables(ctx_len, seq)
    cvec = jnp.zeros((8, D_MODEL), F32).at[:bsz].set(c).at[bsz].set(c_ctx)

    lb_cs = jnp.cumsum(jax.nn.softmax(hg_lb.astype(F32), axis=0), axis=0)
    lb_all = lb_cs - lb_cs[0:1]
    tri_lo = jnp.tril(jnp.ones((HG_CHUNK, HG_CHUNK), F32))
    tri = jnp.stack([tri_lo, tri_lo.T], axis=0)
    head_ones = _block_diag(jnp.ones((HG_HEADS, HG_DK, HG_DK), F32))
    head_mean = (head_ones / HG_DK).astype(BF16)

    offs = np.cumsum((HG_W,) * 5 + (MLA_Q_RANK, MLA_KV_RANK, MLA_ROPE, LRU_W, LRU_W) + (D_MODEL,) * 3)[:-1].tolist()

    for l in range(depth):
        mods = _modulation(cvec, w_mod[l], b_mod[l]).reshape(8, 6, D_MODEL)
        mctx, mb = mods[bsz], mods[:bsz]

        q_, ff, fb, i_, g_, dq, dkv, kr, lx, ly, g1, g2, g3 = jnp.split(w_in[l], offs, axis=-1)
        kr_pad = jnp.zeros((D_MODEL, 512 - MLA_Q_RANK - MLA_KV_RANK - MLA_ROPE), F32)
        w_all = jnp.concatenate([ff, fb, q_, i_, g_, dq, dkv, kr, kr_pad, lx, ly, g1, g2, g3], axis=-1).astype(BF16)
        f, p = _inproj(xs, mctx, mb, norm_mix[l][None, :], w_all, ctx_len)

        lb = lb_all[l]
        lbc = jnp.stack([jnp.log(lb), jnp.log1p(-lb), 1.0 - lb] + [jnp.zeros_like(lb)] * 5, axis=1)
        o_hg = _hgrn(p, f, tri, lbc, head_ones, nctx)

        wq1, wq2, wk1, sk1, sk2, wv, gq, gk = _mla_weights(mla_w_uq[l], mla_w_ukv[l], mla_qk_gain_q[l], mla_qk_gain_k[l])
        qh, kh, vh = _mla_prep(p, cos_t, sin_t, mla_q_norm[l][None, :], mla_kv_norm[l][None, :], gq, gk,
                               wq1, wq2, wk1, sk1, sk2, wv)
        o_mla = _flash(qh, kh, vh, ctx_len)

        wg = jnp.stack([jnp.concatenate([_block_diag(lru_wa[l, d]), _block_diag(lru_wx[l, d])], axis=-1)
                        for d in range(2)], axis=0).astype(BF16)
        bg = jnp.concatenate([lru_ba[l], lru_bx[l]], axis=-1)[:, None, :] * jnp.ones((1, 8, 1), F32)
        lam = lru_lambda[l][:, None, :] * jnp.ones((1, 8, 1), F32)
        h_lru = _lru(p, _pad_rows(lru_conv_w[l]), lru_conv_b[l][None, :], wg, bg, lam, nctx)

        w_br = jnp.stack([w_br_hg[l], w_br_mla[l], w_br_lru[l]], axis=0).astype(BF16)
        hg_gain = jnp.tile(hg_norm[l], HG_HEADS)[None, :]
        xs = _merge(xs, mctx, mb, o_hg, p, hg_gain, head_mean, o_mla, h_lru, w_br, w_out[l].astype(BF16), ctx_len)

        wr = jnp.pad(jnp.concatenate([moe_w_re[l], moe_w_rg[l]], axis=-1), ((0, 0), (0, ROUTER_PAD - N_EXPERTS - N_GROUPS)))
        br = jnp.pad(jnp.concatenate([moe_b_re[l], moe_b_rg[l]]), (0, ROUTER_PAD - N_EXPERTS - N_GROUPS))[None, :]
        epb = N_EXPERTS // EXPERT_STEPS
        w1 = moe_w1[l].reshape(EXPERT_STEPS, epb, D_MODEL, D_EXPERT).transpose(0, 2, 1, 3).reshape(EXPERT_STEPS, D_MODEL, epb * D_EXPERT)
        w3 = moe_w3[l].reshape(EXPERT_STEPS, epb, D_MODEL, D_EXPERT).transpose(0, 2, 1, 3).reshape(EXPERT_STEPS, D_MODEL, epb * D_EXPERT)
        w13 = jnp.concatenate([w1, w3], axis=-1).astype(BF16)
        w2 = moe_w2[l].reshape(EXPERT_STEPS, epb * D_EXPERT, D_MODEL).astype(BF16)
        xs = _moe(xs, mctx, mb, norm_ffn[l][None, :], wr, br, w13, w2, ctx_len)

    return xs[:, ctx_len:, :]
```

```python
import functools

import numpy as np
import jax
import jax.numpy as jnp
from jax import lax
from jax.experimental import pallas as pl
from jax.experimental.pallas import tpu as pltpu

F32 = jnp.float32
BF16 = jnp.bfloat16
HIGHEST = lax.Precision.HIGHEST

D_MODEL = 1024
GRID_W = 64
EPS = 1e-6

HG_HEADS = 8
HG_DK = 64
HG_W = 512
HG_CHUNK = 64
HG_MID = HG_CHUNK // 2
HG_MAX_LOG_RANGE = 60.0

MLA_HEADS = 8
MLA_Q_RANK = 256
MLA_KV_RANK = 128
MLA_NOPE = 64
MLA_ROPE = 32
MLA_V = 64
MLA_QK = MLA_NOPE + MLA_ROPE
MLA_W = MLA_HEADS * MLA_V
HEAD_PAD = 128
V_ROWS = 80
SCORE_SCALE = MLA_QK ** -0.5 * 1.4426950408889634
MAX_UNSHIFTED_SCORE = 57.0
ROPE_BASE = 10000.0

LRU_W = 512
LRU_BLOCKS = 8
LRU_BD = LRU_W // LRU_BLOCKS
CONV_W = 4
LRU_C = 8.0

N_GROUPS = 4
EXP_PER_GROUP = 4
N_EXPERTS = N_GROUPS * EXP_PER_GROUP
D_EXPERT = 256
EXPERT_STEPS = 4
ROUTER_PAD = 128
MOE_TM = 768

TB = 256
HALO = 16
P_WIDTH = 6144
F_WIDTH = 1024
VMEM_LIMIT = 56 * 1024 * 1024

NT_DIMS = (((1,), (1,)), ((), ()))


def _cparams(sem):
    return pltpu.CompilerParams(dimension_semantics=sem, vmem_limit_bytes=VMEM_LIMIT)


def _const_spec(shape):
    nd = len(shape)
    return pl.BlockSpec(shape, lambda *_: (0,) * nd, pipeline_mode=pl.Buffered(1))


def _sigmoid(t):
    return 1.0 / (1.0 + jnp.exp(-t))


def _scan_block(j, d, nctx, nblk):
    fwd = j
    bwd = jnp.where(j < nctx, nctx - 1 - j, nblk - 1 - (j - nctx))
    return jnp.where(d == 0, fwd, bwd)


def _mod_kernel(c_ref, w_ref, b_ref, o_ref):
    c = c_ref[...]
    s = c * _sigmoid(c)
    o_ref[...] = jnp.dot(s, w_ref[...], precision=HIGHEST, preferred_element_type=F32) + b_ref[...]


def _modulation(cvec, w_mod, b_mod):
    n = w_mod.shape[1]
    tn = 1024
    return pl.pallas_call(
        _mod_kernel,
        grid=(n // tn,),
        in_specs=[pl.BlockSpec((8, D_MODEL), lambda j: (0, 0)),
                  pl.BlockSpec((D_MODEL, tn), lambda j: (0, j)),
                  pl.BlockSpec((1, tn), lambda j: (0, j))],
        out_specs=pl.BlockSpec((8, tn), lambda j: (0, j)),
        out_shape=jax.ShapeDtypeStruct((8, n), F32),
        compiler_params=_cparams(("arbitrary",)),
        name="modulation",
    )(cvec, w_mod, b_mod.reshape(1, n))


def _row_mods(i, tm, ctx_len, mctx_ref, mb_ref, k):
    row = i * tm + lax.broadcasted_iota(jnp.int32, (tm, 1), 0)
    return jnp.where(row < ctx_len, mctx_ref[k:k + 1, :], mb_ref[0, k:k + 1, :])


def _norm_modulate(x, gain, shift, scale):
    ms = jnp.mean(x * x, axis=-1, keepdims=True)
    xn = x * lax.rsqrt(ms + EPS) * gain
    return xn * (1.0 + scale) + shift


def _inproj_kernel(x_ref, mctx_ref, mb_ref, gain_ref, w_ref, f_ref, p_ref, *, ctx_len):
    i = pl.program_id(1)
    shift = _row_mods(i, TB, ctx_len, mctx_ref, mb_ref, 0)
    scale = _row_mods(i, TB, ctx_len, mctx_ref, mb_ref, 1)
    h = _norm_modulate(x_ref[0], gain_ref[...], shift, scale).astype(BF16)
    f_ref[0] = jnp.dot(h, w_ref[:, 0:F_WIDTH], preferred_element_type=F32)
    cw = 512
    for j in range(P_WIDTH // cw):
        lo = F_WIDTH + j * cw
        p_ref[0, :, j * cw:(j + 1) * cw] = jnp.dot(
            h, w_ref[:, lo:lo + cw], preferred_element_type=F32).astype(BF16)


def _inproj(xs, mctx, mb, gain, w_all, ctx_len):
    b, lt, _ = xs.shape
    return pl.pallas_call(
        functools.partial(_inproj_kernel, ctx_len=ctx_len),
        grid=(b, lt // TB),
        in_specs=[pl.BlockSpec((1, TB, D_MODEL), lambda bi, i: (bi, i, 0)),
                  _const_spec((6, D_MODEL)),
                  pl.BlockSpec((1, 6, D_MODEL), lambda bi, i: (bi, 0, 0)),
                  _const_spec((1, D_MODEL)),
                  _const_spec((D_MODEL, F_WIDTH + P_WIDTH))],
        out_specs=[pl.BlockSpec((1, TB, F_WIDTH), lambda bi, i: (bi, i, 0)),
                   pl.BlockSpec((1, TB, P_WIDTH), lambda bi, i: (bi, i, 0))],
        out_shape=[jax.ShapeDtypeStruct((b, lt, F_WIDTH), F32),
                   jax.ShapeDtypeStruct((b, lt, P_WIDTH), BF16)],
        compiler_params=_cparams(("parallel", "arbitrary")),
        name="inproj",
    )(xs, mctx, mb, gain, w_all)


def _hgrn_kernel(q_ref, v_ref, f_ref, tri_ref, lbc_ref, bd_ref, o_ref, st_ref, c_scr, k_scr, v_scr):
    d = pl.program_id(1)
    j = pl.program_id(2)
    nch = TB // HG_CHUNK

    @pl.when(j == 0)
    def _():
        st_ref[...] = jnp.zeros_like(st_ref)

    tri = tri_ref[0]
    log_lb = lbc_ref[0, 0:1, :]
    log_1mlb = lbc_ref[0, 1:2, :]
    one_mlb = lbc_ref[0, 2:3, :]
    ri = lax.broadcasted_iota(jnp.int32, (HG_CHUNK, HG_CHUNK), 0)
    ci = lax.broadcasted_iota(jnp.int32, (HG_CHUNK, HG_CHUNK), 1)
    mask = jnp.where(d == 0, ri - ci, ci - ri) >= 0
    rowi = lax.broadcasted_iota(jnp.int32, (HG_CHUNK, 1), 0)

    def chunk_body(cidx, carry):
        cc = jnp.where(d == 0, cidx, nch - 1 - cidx)
        r0 = pl.multiple_of(cc * HG_CHUNK, HG_CHUNK)
        rows = pl.ds(r0, HG_CHUNK)
        z = f_ref[0, rows, :]
        q = q_ref[0, rows, :].astype(F32)
        v = v_ref[0, rows, :]
        e = jnp.exp(-jnp.abs(z))
        log_sig = jnp.minimum(z, 0.0) - jnp.log1p(e)
        t = log_1mlb + log_sig
        logf = jnp.maximum(log_lb, t) + jnp.log1p(jnp.exp(-jnp.abs(log_lb - t)))
        k = one_mlb * (jnp.where(z >= 0, e, 1.0) / (1.0 + e))
        c = jnp.dot(tri, logf, precision=HIGHEST, preferred_element_type=F32)
        c_mid = c[HG_MID:HG_MID + 1, :]
        c_first = jnp.where(d == 0, c[0:1, :], c[HG_CHUNK - 1:HG_CHUNK, :])
        c_end = jnp.where(d == 0, c[HG_CHUNK - 1:HG_CHUNK, :], c[0:1, :])
        span = jnp.max(jnp.maximum(c_first - c_mid, c_mid - c_end))

        qs = (q * jnp.exp(c)).astype(BF16)
        ks = (k * jnp.exp(c_end - c)).astype(BF16)
        dec = jnp.exp(c_end)
        outs = []
        for h in range(HG_HEADS):
            sl = slice(HG_DK * h, HG_DK * (h + 1))
            st = st_ref[h]
            outs.append(lax.dot_general(qs[:, sl], st.astype(BF16), NT_DIMS, preferred_element_type=F32))
            vt = v[:, sl].astype(F32).T.astype(BF16)
            st_ref[h] = st * dec[:, sl] + jnp.dot(vt, ks[:, sl], preferred_element_type=F32)
        o_ref[0, 0, rows, :] = jnp.concatenate(outs, axis=-1)

        @pl.when(span <= HG_MAX_LOG_RANGE)
        def _matmul_form():
            qd = (q * jnp.exp(c - c_mid)).astype(BF16)
            kd = (k * jnp.exp(c_mid - c)).astype(BF16)
            intra = []
            for h in range(HG_HEADS):
                sl = slice(HG_DK * h, HG_DK * (h + 1))
                a = lax.dot_general(qd[:, sl], kd[:, sl], NT_DIMS, preferred_element_type=F32)
                a = jnp.where(mask, a, 0.0).astype(BF16)
                intra.append(jnp.dot(a, v[:, sl], preferred_element_type=F32))
            o_ref[0, 0, rows, :] += jnp.concatenate(intra, axis=-1)

        @pl.when(span > HG_MAX_LOG_RANGE)
        def _exact_form():
            c_scr[...] = c
            k_scr[...] = k
            v_scr[...] = v.astype(F32)
            bd = bd_ref[...]

            def key_row(s, acc):
                c_s = c_scr[pl.ds(s, 1), :]
                w = q * jnp.exp(jnp.minimum(c - c_s, 0.0)) * k_scr[pl.ds(s, 1), :]
                ws = jnp.dot(w, bd, precision=HIGHEST, preferred_element_type=F32)
                valid = jnp.where(d == 0, rowi - s, s - rowi) >= 0
                return acc + jnp.where(valid, ws, 0.0) * v_scr[pl.ds(s, 1), :]

            o_ref[0, 0, rows, :] += lax.fori_loop(0, HG_CHUNK, key_row, jnp.zeros((HG_CHUNK, HG_W), F32))

        return carry

    lax.fori_loop(0, nch, chunk_body, 0)


def _hgrn(p, f, tri, lbc, bd, nctx):
    b, lt, _ = p.shape
    nblk = lt // TB

    def blk(bi, d, j):
        return _scan_block(j, d, nctx, nblk)

    return pl.pallas_call(
        _hgrn_kernel,
        grid=(b, 2, nblk),
        in_specs=[pl.BlockSpec((1, TB, HG_W), lambda bi, d, j: (bi, blk(bi, d, j), 0)),
                  pl.BlockSpec((1, TB, HG_W), lambda bi, d, j: (bi, blk(bi, d, j), 1)),
                  pl.BlockSpec((1, TB, HG_W), lambda bi, d, j: (bi, blk(bi, d, j), d)),
                  pl.BlockSpec((1, HG_CHUNK, HG_CHUNK), lambda bi, d, j: (d, 0, 0)),
                  pl.BlockSpec((1, 8, HG_W), lambda bi, d, j: (d, 0, 0)),
                  _const_spec((HG_W, HG_W))],
        out_specs=pl.BlockSpec((1, 1, TB, HG_W), lambda bi, d, j: (d, bi, blk(bi, d, j), 0)),
        out_shape=jax.ShapeDtypeStruct((2, b, lt, HG_W), F32),
        scratch_shapes=[pltpu.VMEM((HG_HEADS, HG_DK, HG_DK), F32),
                        pltpu.VMEM((HG_CHUNK, HG_W), F32),
                        pltpu.VMEM((HG_CHUNK, HG_W), F32),
                        pltpu.VMEM((HG_CHUNK, HG_W), F32)],
        compiler_params=_cparams(("parallel", "arbitrary", "arbitrary")),
        name="hgrn2",
    )(p, p, f, tri, lbc, bd)


def _rms_rows(t, width):
    return lax.rsqrt(jnp.sum(t * t, axis=-1, keepdims=True) * (1.0 / width) + EPS)


def _mla_prep_kernel(pm_ref, cos_ref, sin_ref, qn_ref, kvn_ref, gq_ref, gk_ref,
                     wq1_ref, wq2_ref, wk1_ref, sk1_ref, sk2_ref, wv_ref,
                     qt_ref, k_ref, vt_ref, kn_ref):
    pm = pm_ref[0]
    dq = pm[:, 0:MLA_Q_RANK].astype(F32)
    dkv = pm[:, MLA_Q_RANK:MLA_Q_RANK + MLA_KV_RANK].astype(F32)
    dqn = (dq * _rms_rows(dq, MLA_Q_RANK) * qn_ref[...]).astype(BF16)
    dkvn = (dkv * _rms_rows(dkv, MLA_KV_RANK) * kvn_ref[...]).astype(BF16)
    q1 = jnp.dot(dqn, wq1_ref[...], preferred_element_type=F32)
    q2 = jnp.dot(dqn, wq2_ref[...], preferred_element_type=F32)
    k1 = (jnp.dot(dkvn, wk1_ref[...], preferred_element_type=F32)
          + jnp.dot(pm, sk1_ref[...], preferred_element_type=F32))
    k2 = jnp.dot(pm, sk2_ref[...], preferred_element_type=F32)
    vv = jnp.dot(dkvn, wv_ref[...], preferred_element_type=F32)
    cos = cos_ref[...]
    sin = sin_ref[...]
    gq_c = gq_ref[0:1, :] * cos
    gq_s = gq_ref[1:2, :] * sin
    gk_c = gk_ref[0:1, :] * cos
    gk_s = gk_ref[1:2, :] * sin
    vrow = lax.broadcasted_iota(jnp.int32, (V_ROWS, TB), 0)
    kn = []
    for h in range(MLA_HEADS):
        sl = slice(HEAD_PAD * h, HEAD_PAD * (h + 1))
        qh = q1[:, sl]
        qo = (qh * gq_c + q2[:, sl] * gq_s) * (_rms_rows(qh, MLA_QK) * SCORE_SCALE)
        qt_ref[0, h] = qo.T.astype(BF16)
        kh = k1[:, sl]
        ko = (kh * gk_c + k2[:, sl] * gk_s) * _rms_rows(kh, MLA_QK)
        k_ref[0, h] = ko.astype(BF16)
        ksq = jnp.max(jnp.sum(ko * ko, axis=-1, keepdims=True), axis=0, keepdims=True)
        kn.append(jnp.broadcast_to(ksq, (1, HEAD_PAD)))
        vt = vv[:, sl].T[0:V_ROWS, :]
        vt_ref[0, h] = jnp.where(vrow == MLA_V, 1.0, vt).astype(BF16)
    kn_ref[0, 0] = jnp.concatenate(kn, axis=0)


def _mla_prep(p, cos_t, sin_t, qn, kvn, gq, gk, wq1, wq2, wk1, sk1, sk2, wv):
    b, lt, _ = p.shape
    hw = MLA_HEADS * HEAD_PAD
    return pl.pallas_call(
        _mla_prep_kernel,
        grid=(b, lt // TB),
        in_specs=[pl.BlockSpec((1, TB, 512), lambda bi, i: (bi, i, 3)),
                  pl.BlockSpec((TB, HEAD_PAD), lambda bi, i: (i, 0)),
                  pl.BlockSpec((TB, HEAD_PAD), lambda bi, i: (i, 0)),
                  _const_spec((1, MLA_Q_RANK)), _const_spec((1, MLA_KV_RANK)),
                  _const_spec((2, HEAD_PAD)), _const_spec((2, HEAD_PAD)),
                  _const_spec((MLA_Q_RANK, hw)), _const_spec((MLA_Q_RANK, hw)),
                  _const_spec((MLA_KV_RANK, hw)), _const_spec((512, hw)), _const_spec((512, hw)),
                  _const_spec((MLA_KV_RANK, hw))],
        out_specs=[pl.BlockSpec((1, MLA_HEADS, HEAD_PAD, TB), lambda bi, i: (bi, 0, 0, i)),
                   pl.BlockSpec((1, MLA_HEADS, TB, HEAD_PAD), lambda bi, i: (bi, 0, i, 0)),
                   pl.BlockSpec((1, MLA_HEADS, V_ROWS, TB), lambda bi, i: (bi, 0, 0, i)),
                   pl.BlockSpec((1, 1, MLA_HEADS, HEAD_PAD), lambda bi, i: (bi, i, 0, 0))],
        out_shape=[jax.ShapeDtypeStruct((b, MLA_HEADS, HEAD_PAD, lt), BF16),
                   jax.ShapeDtypeStruct((b, MLA_HEADS, lt, HEAD_PAD), BF16),
                   jax.ShapeDtypeStruct((b, MLA_HEADS, V_ROWS, lt), BF16),
                   jax.ShapeDtypeStruct((b, lt // TB, MLA_HEADS, HEAD_PAD), F32)],
        compiler_params=_cparams(("parallel", "arbitrary")),
        name="mla_prep",
    )(p, cos_t, sin_t, qn, kvn, gq, gk, wq1, wq2, wk1, sk1, sk2, wv)


def _attend_bounded(qts, k_ref, vt_ref, s_scr, nkeys, tk):
    n = nkeys // tk

    def scores(kb, slot):
        r0 = pl.multiple_of(kb * tk, tk)
        for hh in range(2):
            s_scr[slot, hh, 0:tk, :] = jnp.dot(k_ref[0, hh, pl.ds(r0, tk), :], qts[hh], preferred_element_type=F32)

    def accumulate(kb, slot, accs):
        r0 = pl.multiple_of(kb * tk, tk)
        out = []
        for hh in range(2):
            p = jnp.exp2(s_scr[slot, hh, 0:tk, :]).astype(BF16)
            out.append(accs[hh] + jnp.dot(vt_ref[0, hh, :, pl.ds(r0, tk)], p, preferred_element_type=F32))
        return tuple(out)

    def pair_step(j, accs):
        scores(2 * j + 1, 1)
        accs = accumulate(2 * j, 0, accs)
        scores(2 * j + 2, 0)
        return accumulate(2 * j + 1, 1, accs)

    zero = jnp.zeros((V_ROWS, TB), F32)
    scores(0, 0)
    accs = lax.fori_loop(0, (n - 1) // 2, pair_step, (zero, zero))
    if n % 2 == 0:
        scores(n - 1, 1)
        accs = accumulate(n - 2, 0, accs)
        return accumulate(n - 1, 1, accs)
    return accumulate(n - 1, 0, accs)


def _attend_online(qts, k_ref, vt_ref, nkeys, tk):
    def kv_step(kb, carry):
        r0 = pl.multiple_of(kb * tk, tk)
        out = []
        for hh in range(2):
            m, acc = carry[hh]
            s = jnp.dot(k_ref[0, hh, pl.ds(r0, tk), :], qts[hh], preferred_element_type=F32)
            m_new = jnp.maximum(m, jnp.max(s, axis=0, keepdims=True))
            p = jnp.exp2(s - m_new).astype(BF16)
            acc = jnp.exp2(m - m_new) * acc + jnp.dot(vt_ref[0, hh, :, pl.ds(r0, tk)], p, preferred_element_type=F32)
            out.append((m_new, acc))
        return tuple(out)

    init = (jnp.full((1, TB), -jnp.inf, F32), jnp.zeros((V_ROWS, TB), F32))
    res = lax.fori_loop(0, nkeys // tk, kv_step, (init, init))
    return res[0][1], res[1][1]


def _flash_kernel(kmax_ref, qt_ref, k_ref, vt_ref, o_ref, s_scr, *, ctx_len, lt, tk):
    bi = pl.program_id(0)
    g = pl.program_id(1)
    i = pl.program_id(2)
    nctx = ctx_len // TB

    qts = [qt_ref[0, hh] for hh in range(2)]
    worst = 0.0
    for hh in range(2):
        qt = qts[hh].astype(F32)
        qnorm = jnp.sqrt(jnp.max(jnp.sum(qt * qt, axis=0, keepdims=True)))
        worst = jnp.maximum(worst, qnorm * kmax_ref[bi, 2 * g + hh])

    def finish(accs):
        outs = [(acc[0:MLA_V, :] / acc[MLA_V:MLA_V + 1, :]).T for acc in accs]
        o_ref[0] = jnp.concatenate(outs, axis=-1).astype(BF16)

    def run(nkeys, tkk):
        @pl.when(worst <= MAX_UNSHIFTED_SCORE)
        def _():
            finish(_attend_bounded(qts, k_ref, vt_ref, s_scr, nkeys, tkk))

        @pl.when(worst > MAX_UNSHIFTED_SCORE)
        def _():
            finish(_attend_online(qts, k_ref, vt_ref, nkeys, tkk))

    @pl.when(i < nctx)
    def _():
        run(ctx_len, TB)

    @pl.when(i >= nctx)
    def _():
        run(lt, tk)


def _flash(kmax, qt, k, vt, ctx_len):
    b, _, lt, _ = k.shape
    tk = next(t for t in (768, 512, 256) if lt % t == 0)
    return pl.pallas_call(
        functools.partial(_flash_kernel, ctx_len=ctx_len, lt=lt, tk=tk),
        grid=(b, MLA_HEADS // 2, lt // TB),
        in_specs=[pl.BlockSpec(memory_space=pltpu.SMEM),
                  pl.BlockSpec((1, 2, HEAD_PAD, TB), lambda bi, g, i: (bi, g, 0, i)),
                  pl.BlockSpec((1, 2, lt, HEAD_PAD), lambda bi, g, i: (bi, g, 0, 0)),
                  pl.BlockSpec((1, 2, V_ROWS, lt), lambda bi, g, i: (bi, g, 0, 0))],
        out_specs=pl.BlockSpec((1, TB, HEAD_PAD), lambda bi, g, i: (bi, i, g)),
        out_shape=jax.ShapeDtypeStruct((b, lt, MLA_W), BF16),
        scratch_shapes=[pltpu.VMEM((2, 2, tk, TB), F32)],
        compiler_params=_cparams(("parallel", "parallel", "arbitrary")),
        name="mla_attention",
    )(kmax, qt, k, vt)


def _lru_kernel(x_ref, xp_ref, xn_ref, cw_ref, cb_ref, wg_ref, bg_ref, lam_ref, o_ref, xe_scr, h_scr, *, nctx, nblk):
    d = pl.program_id(1)
    j = pl.program_id(2)
    blk = _scan_block(j, d, nctx, nblk)

    @pl.when(j == 0)
    def _():
        h_scr[...] = jnp.zeros_like(h_scr)

    keep_prev = jnp.logical_and(blk != 0, blk != nctx)
    keep_next = jnp.logical_and(blk != nctx - 1, blk != nblk - 1)
    xe_scr[0:HALO, :] = jnp.where(keep_prev, xp_ref[0].astype(F32), 0.0)
    xe_scr[HALO:HALO + TB, :] = x_ref[0].astype(F32)
    xe_scr[HALO + TB:2 * HALO + TB, :] = jnp.where(keep_next, xn_ref[0].astype(F32), 0.0)
    left = CONV_W // 2
    u = jnp.broadcast_to(cb_ref[...], (TB, LRU_W))
    for tap in range(CONV_W):
        u = u + xe_scr[pl.ds(HALO - left + tap, TB), :] * cw_ref[tap:tap + 1, :]

    g = jnp.dot(u.astype(BF16), wg_ref[0], preferred_element_type=F32) + bg_ref[0, 0:1, :]
    r = _sigmoid(g[:, 0:LRU_W])
    ig = _sigmoid(g[:, LRU_W:2 * LRU_W])
    neg_lam = -lam_ref[0, 0:1, :]
    softplus = jnp.maximum(neg_lam, 0.0) + jnp.log1p(jnp.exp(-jnp.abs(neg_lam)))
    a = jnp.exp(-LRU_C * r * softplus)
    bb = jnp.sqrt(1.0 - a * a) * (ig * u)
    rowi = lax.broadcasted_iota(jnp.int32, (TB, 1), 0)

    def scan(forward):
        aa, hh = a, bb
        k = 1
        while k < TB:
            shift = k if forward else TB - k
            valid = (rowi >= k) if forward else (rowi < TB - k)
            a_sh = jnp.where(valid, pltpu.roll(aa, shift, 0), 1.0)
            h_sh = jnp.where(valid, pltpu.roll(hh, shift, 0), 0.0)
            hh = aa * h_sh + hh
            aa = aa * a_sh
            k *= 2
        hh = hh + aa * h_scr[...]
        o_ref[0, 0] = hh
        last = TB - 1 if forward else 0
        h_scr[...] = hh[last:last + 1, :]

    @pl.when(d == 0)
    def _():
        scan(True)

    @pl.when(d == 1)
    def _():
        scan(False)


def _lru(p, conv_w, conv_b, wg, bg, lam, nctx):
    b, lt, _ = p.shape
    nblk = lt // TB
    hpb = TB // HALO
    nh = lt // HALO

    def blk(d, j):
        return _scan_block(j, d, nctx, nblk)

    return pl.pallas_call(
        functools.partial(_lru_kernel, nctx=nctx, nblk=nblk),
        grid=(b, 2, nblk),
        in_specs=[pl.BlockSpec((1, TB, LRU_W), lambda bi, d, j: (bi, blk(d, j), 4)),
                  pl.BlockSpec((1, HALO, LRU_W), lambda bi, d, j: (bi, jnp.maximum(blk(d, j) * hpb - 1, 0), 4)),
                  pl.BlockSpec((1, HALO, LRU_W), lambda bi, d, j: (bi, jnp.minimum((blk(d, j) + 1) * hpb, nh - 1), 4)),
                  _const_spec((8, LRU_W)), _const_spec((1, LRU_W)),
                  pl.BlockSpec((1, LRU_W, 2 * LRU_W), lambda bi, d, j: (d, 0, 0)),
                  pl.BlockSpec((1, 8, 2 * LRU_W), lambda bi, d, j: (d, 0, 0)),
                  pl.BlockSpec((1, 8, LRU_W), lambda bi, d, j: (d, 0, 0))],
        out_specs=pl.BlockSpec((1, 1, TB, LRU_W), lambda bi, d, j: (d, bi, blk(d, j), 0)),
        out_shape=jax.ShapeDtypeStruct((2, b, lt, LRU_W), F32),
        scratch_shapes=[pltpu.VMEM((TB + 2 * HALO, LRU_W), F32), pltpu.VMEM((1, LRU_W), F32)],
        compiler_params=_cparams(("parallel", "arbitrary", "arbitrary")),
        name="rglru",
    )(p, p, p, conv_w, conv_b, wg, bg, lam)


def _gelu_tanh(t):
    return 0.5 * t * (1.0 + jnp.tanh(0.7978845608028654 * (t + 0.044715 * t * t * t)))


def _merge_kernel(x_ref, mctx_ref, mb_ref, hgf_ref, hgb_ref, hgg_ref, hgn_ref, hm_ref, mla_ref,
                  lf_ref, lb_ref, ly_ref, g1_ref, g2_ref, g3_ref, wbr_ref, wout_ref, o_ref, *, ctx_len):
    i = pl.program_id(1)
    o = hgf_ref[0, 0] + hgb_ref[0, 0]
    ms = jnp.dot((o * o).astype(BF16), hm_ref[...], preferred_element_type=F32)
    o_hg = o * lax.rsqrt(ms + EPS) * hgn_ref[...] * _sigmoid(hgg_ref[0].astype(F32))
    o_lru = (lf_ref[0, 0] + lb_ref[0, 0]) * _gelu_tanh(ly_ref[0].astype(F32))
    y = (_sigmoid(g1_ref[0].astype(F32)) * jnp.dot(o_hg.astype(BF16), wbr_ref[0], preferred_element_type=F32)
         + _sigmoid(g2_ref[0].astype(F32)) * jnp.dot(mla_ref[0], wbr_ref[1], preferred_element_type=F32)
         + _sigmoid(g3_ref[0].astype(F32)) * jnp.dot(o_lru.astype(BF16), wbr_ref[2], preferred_element_type=F32))
    gate = _row_mods(i, TB, ctx_len, mctx_ref, mb_ref, 2)
    o_ref[0] = x_ref[0] + gate * jnp.dot(y.astype(BF16), wout_ref[...], preferred_element_type=F32)


def _merge(xs, mctx, mb, o_hg, p, hg_gain, head_mean, o_mla, h_lru, w_br, w_out, ctx_len):
    b, lt, _ = xs.shape

    def pcol(width, c):
        return pl.BlockSpec((1, TB, width), lambda bi, i: (bi, i, c))

    def dirspec(d):
        return pl.BlockSpec((1, 1, TB, 512), lambda bi, i: (d, bi, i, 0))

    return pl.pallas_call(
        functools.partial(_merge_kernel, ctx_len=ctx_len),
        grid=(b, lt // TB),
        in_specs=[pl.BlockSpec((1, TB, D_MODEL), lambda bi, i: (bi, i, 0)),
                  _const_spec((6, D_MODEL)),
                  pl.BlockSpec((1, 6, D_MODEL), lambda bi, i: (bi, 0, 0)),
                  dirspec(0), dirspec(1), pcol(512, 2),
                  _const_spec((1, HG_W)), _const_spec((HG_W, HG_W)),
                  pl.BlockSpec((1, TB, MLA_W), lambda bi, i: (bi, i, 0)),
                  dirspec(0), dirspec(1), pcol(512, 5),
                  pcol(D_MODEL, 3), pcol(D_MODEL, 4), pcol(D_MODEL, 5),
                  _const_spec((3, 512, D_MODEL)), _const_spec((D_MODEL, D_MODEL))],
        out_specs=pl.BlockSpec((1, TB, D_MODEL), lambda bi, i: (bi, i, 0)),
        out_shape=jax.ShapeDtypeStruct((b, lt, D_MODEL), F32),
        compiler_params=_cparams(("parallel", "arbitrary")),
        name="merge",
    )(xs, mctx, mb, o_hg, o_hg, p, hg_gain, head_mean, o_mla, h_lru, h_lru, p, p, p, p, w_br, w_out)


def _first_index_of_max(vals, lane, valid):
    masked = jnp.where(valid, vals, -jnp.inf)
    m = jnp.max(masked, axis=-1, keepdims=True)
    idx = jnp.min(jnp.where(masked == m, lane, ROUTER_PAD), axis=-1, keepdims=True)
    return m, idx


def _router(h, wr, br):
    logits = jnp.dot(h, wr, precision=HIGHEST, preferred_element_type=F32)
    lane = lax.broadcasted_iota(jnp.int32, logits.shape, 1)
    biased = logits + br
    is_group = jnp.logical_and(lane >= N_EXPERTS, lane < N_EXPERTS + N_GROUPS)
    _, g_lane = _first_index_of_max(biased, lane, is_group)
    g_max, _ = _first_index_of_max(logits, lane, is_group)
    g_exp = jnp.where(is_group, jnp.exp(logits - g_max), 0.0)
    g_sel_logit = jnp.sum(jnp.where(lane == g_lane, logits, 0.0), axis=-1, keepdims=True)
    p_g = jnp.exp(g_sel_logit - g_max) / jnp.sum(g_exp, axis=-1, keepdims=True)
    in_group = jnp.right_shift(lane, 2) == (g_lane - N_EXPERTS)
    _, i1 = _first_index_of_max(biased, lane, in_group)
    _, i2 = _first_index_of_max(biased, lane, jnp.logical_and(in_group, lane != i1))
    l1 = jnp.sum(jnp.where(lane == i1, logits, 0.0), axis=-1, keepdims=True)
    l2 = jnp.sum(jnp.where(lane == i2, logits, 0.0), axis=-1, keepdims=True)
    lm = jnp.maximum(l1, l2)
    e1 = jnp.exp(l1 - lm)
    e2 = jnp.exp(l2 - lm)
    inv = p_g / (e1 + e2)
    return jnp.where(lane == i1, e1 * inv, 0.0) + jnp.where(lane == i2, e2 * inv, 0.0)


def _moe_kernel(x_ref, mctx_ref, mb_ref, gain_ref, wr_ref, br_ref, w13_ref, w2_ref, o_ref,
                h_scr, comb_scr, acc_scr, *, ctx_len, tm):
    i = pl.program_id(1)
    e = pl.program_id(2)
    epb = N_EXPERTS // EXPERT_STEPS
    hw = epb * D_EXPERT

    @pl.when(e == 0)
    def _():
        shift = _row_mods(i, tm, ctx_len, mctx_ref, mb_ref, 3)
        scale = _row_mods(i, tm, ctx_len, mctx_ref, mb_ref, 4)
        h = _norm_modulate(x_ref[0], gain_ref[...], shift, scale)
        h_scr[...] = h.astype(BF16)
        comb_scr[...] = _router(h, wr_ref[...], br_ref[...])
        acc_scr[...] = jnp.zeros_like(acc_scr)

    h = h_scr[...]
    h13 = jnp.dot(h, w13_ref[0], preferred_element_type=F32)
    h1 = h13[:, 0:hw]
    hid = h1 * _sigmoid(h1) * h13[:, hw:2 * hw]
    comb = comb_scr[...]
    lane = lax.broadcasted_iota(jnp.int32, comb.shape, 1)
    parts = []
    for k in range(epb):
        wk = jnp.sum(jnp.where(lane == e * epb + k, comb, 0.0), axis=-1, keepdims=True)
        parts.append(hid[:, k * D_EXPERT:(k + 1) * D_EXPERT] * wk)
    hid = jnp.concatenate(parts, axis=-1).astype(BF16)
    acc_scr[...] += jnp.dot(hid, w2_ref[0], preferred_element_type=F32)

    @pl.when(e == EXPERT_STEPS - 1)
    def _():
        gate = _row_mods(i, tm, ctx_len, mctx_ref, mb_ref, 5)
        o_ref[0] = x_ref[0] + gate * acc_scr[...]


def _moe(xs, mctx, mb, gain, wr, br, w13, w2, ctx_len):
    b, lt, _ = xs.shape
    tm = MOE_TM if lt % MOE_TM == 0 else TB
    epb = N_EXPERTS // EXPERT_STEPS
    return pl.pallas_call(
        functools.partial(_moe_kernel, ctx_len=ctx_len, tm=tm),
        grid=(b, lt // tm, EXPERT_STEPS),
        in_specs=[pl.BlockSpec((1, tm, D_MODEL), lambda bi, i, e: (bi, i, 0)),
                  _const_spec((6, D_MODEL)),
                  pl.BlockSpec((1, 6, D_MODEL), lambda bi, i, e: (bi, 0, 0)),
                  _const_spec((1, D_MODEL)),
                  _const_spec((D_MODEL, ROUTER_PAD)), _const_spec((1, ROUTER_PAD)),
                  pl.BlockSpec((1, D_MODEL, 2 * epb * D_EXPERT), lambda bi, i, e: (e, 0, 0)),
                  pl.BlockSpec((1, epb * D_EXPERT, D_MODEL), lambda bi, i, e: (e, 0, 0))],
        out_specs=pl.BlockSpec((1, tm, D_MODEL), lambda bi, i, e: (bi, i, 0)),
        out_shape=jax.ShapeDtypeStruct((b, lt, D_MODEL), F32),
        scratch_shapes=[pltpu.VMEM((tm, D_MODEL), BF16), pltpu.VMEM((tm, ROUTER_PAD), F32),
                        pltpu.VMEM((tm, D_MODEL), F32)],
        compiler_params=_cparams(("parallel", "arbitrary", "arbitrary")),
        name="moe",
    )(xs, mctx, mb, gain, wr, br, w13, w2)


def _rope_tables(ctx_len, seq):
    half = MLA_ROPE // 2
    rows = seq // GRID_W
    pos_row = jnp.repeat(jnp.arange(rows, dtype=F32), GRID_W)
    pos_col = jnp.tile(jnp.arange(GRID_W, dtype=F32), rows)
    inv = ROPE_BASE ** (-jnp.arange(0, half, 2, dtype=F32) / half)
    ang = jnp.concatenate([pos_row[:, None] * inv, pos_col[:, None] * inv], axis=-1)
    cos, sin = jnp.cos(ang), jnp.sin(ang)
    ones = jnp.ones((seq, MLA_NOPE), F32)
    zpad = jnp.zeros((seq, HEAD_PAD - MLA_QK), F32)
    cos_x = jnp.concatenate([ones, cos, cos, zpad], axis=-1)
    sin_x = jnp.concatenate([jnp.zeros((seq, MLA_NOPE), F32), -sin, sin, zpad], axis=-1)
    cos_c = jnp.concatenate([jnp.ones((ctx_len, MLA_QK), F32), jnp.zeros((ctx_len, HEAD_PAD - MLA_QK), F32)], axis=-1)
    sin_c = jnp.zeros((ctx_len, HEAD_PAD), F32)
    return jnp.concatenate([cos_c, cos_x], axis=0), jnp.concatenate([sin_c, sin_x], axis=0)


def _swap_rope_halves(t):
    half = MLA_ROPE // 2
    return jnp.concatenate([jnp.zeros_like(t[..., :MLA_NOPE]), t[..., MLA_NOPE + half:], t[..., MLA_NOPE:MLA_NOPE + half]],
                           axis=-1)


def _pad_heads(t):
    pad = [(0, 0)] * (t.ndim - 1) + [(0, HEAD_PAD - t.shape[-1])]
    t = jnp.pad(t, pad)
    return t.reshape(t.shape[:-2] + (t.shape[-2] * HEAD_PAD,))


def _mla_weights(w_uq, w_ukv, gq, gk):
    wq = w_uq.reshape(MLA_Q_RANK, MLA_HEADS, MLA_QK)
    wq1 = _pad_heads(wq).astype(BF16)
    wq2 = _pad_heads(_swap_rope_halves(wq)).astype(BF16)
    wkv = w_ukv.reshape(MLA_KV_RANK, MLA_HEADS, MLA_NOPE + MLA_V)
    wk1 = _pad_heads(wkv[..., :MLA_NOPE]).astype(BF16)
    kr0 = MLA_Q_RANK + MLA_KV_RANK
    eye = jnp.eye(MLA_ROPE, dtype=F32)
    place = jnp.concatenate([jnp.zeros((MLA_ROPE, MLA_NOPE), F32), eye], axis=-1)
    place = jnp.broadcast_to(place[:, None, :], (MLA_ROPE, MLA_HEADS, MLA_QK))
    zrow_a = jnp.zeros((kr0, MLA_HEADS * HEAD_PAD), F32)
    zrow_b = jnp.zeros((512 - kr0 - MLA_ROPE, MLA_HEADS * HEAD_PAD), F32)
    sk1 = jnp.concatenate([zrow_a, _pad_heads(place), zrow_b], axis=0).astype(BF16)
    sk2 = jnp.concatenate([zrow_a, _pad_heads(_swap_rope_halves(place)), zrow_b], axis=0).astype(BF16)
    wv = _pad_heads(wkv[..., MLA_NOPE:]).astype(BF16)

    def gains(g):
        g1 = jnp.pad(g, (0, HEAD_PAD - MLA_QK))
        g2 = jnp.pad(_swap_rope_halves(g), (0, HEAD_PAD - MLA_QK))
        return jnp.stack([g1, g2], axis=0)

    return wq1, wq2, wk1, sk1, sk2, wv, gains(gq), gains(gk)


def _block_diag(w):
    n, a, bb = w.shape
    eye = jnp.eye(n, dtype=w.dtype)
    return (eye[:, None, :, None] * w[:, :, None, :]).reshape(n * a, n * bb)


def _pad_rows(t, rows=8):
    return jnp.pad(t, ((0, rows - t.shape[0]), (0, 0)))


def kernel(x, c, ctx, c_ctx, w_mod, b_mod, norm_mix, norm_ffn, w_in, hg_lb, hg_norm, mla_q_norm, mla_kv_norm, mla_w_uq, mla_w_ukv, mla_qk_gain_q, mla_qk_gain_k, lru_conv_w, lru_conv_b, lru_wa, lru_ba, lru_wx, lru_bx, lru_lambda, w_br_hg, w_br_mla, w_br_lru, w_out, moe_w_rg, moe_b_rg, moe_w_re, moe_b_re, moe_w1, moe_w3, moe_w2):
    bsz, seq, _ = x.shape
    ctx_len = ctx.shape[1]
    depth = w_in.shape[0]
    assert seq % TB == 0 and ctx_len % TB == 0 and seq % GRID_W == 0 and bsz < 8
    nctx = ctx_len // TB

    xs = jnp.concatenate([ctx, x], axis=1)
    cos_t, sin_t = _rope_tables(ctx_len, seq)
    cvec = jnp.zeros((8, D_MODEL), F32).at[:bsz].set(c).at[bsz].set(c_ctx)

    lb_cs = jnp.cumsum(jax.nn.softmax(hg_lb.astype(F32), axis=0), axis=0)
    lb_all = lb_cs - lb_cs[0:1]
    tri_lo = jnp.tril(jnp.ones((HG_CHUNK, HG_CHUNK), F32))
    tri = jnp.stack([tri_lo, tri_lo.T], axis=0)
    head_ones = _block_diag(jnp.ones((HG_HEADS, HG_DK, HG_DK), F32))
    head_mean = (head_ones / HG_DK).astype(BF16)

    offs = np.cumsum((HG_W,) * 5 + (MLA_Q_RANK, MLA_KV_RANK, MLA_ROPE, LRU_W, LRU_W) + (D_MODEL,) * 3)[:-1].tolist()

    for l in range(depth):
        mods = _modulation(cvec, w_mod[l], b_mod[l]).reshape(8, 6, D_MODEL)
        mctx, mb = mods[bsz], mods[:bsz]

        q_, ff, fb, i_, g_, dq, dkv, kr, lx, ly, g1, g2, g3 = jnp.split(w_in[l], offs, axis=-1)
        kr_pad = jnp.zeros((D_MODEL, 512 - MLA_Q_RANK - MLA_KV_RANK - MLA_ROPE), F32)
        w_all = jnp.concatenate([ff, fb, q_, i_, g_, dq, dkv, kr, kr_pad, lx, ly, g1, g2, g3], axis=-1).astype(BF16)
        f, p = _inproj(xs, mctx, mb, norm_mix[l][None, :], w_all, ctx_len)

        lb = lb_all[l]
        lbc = jnp.stack([jnp.log(lb), jnp.log1p(-lb), 1.0 - lb] + [jnp.zeros_like(lb)] * 5, axis=1)
        o_hg = _hgrn(p, f, tri, lbc, head_ones, nctx)

        wq1, wq2, wk1, sk1, sk2, wv, gq, gk = _mla_weights(mla_w_uq[l], mla_w_ukv[l], mla_qk_gain_q[l], mla_qk_gain_k[l])
        qt, kh, vt, ksq = _mla_prep(p, cos_t, sin_t, mla_q_norm[l][None, :], mla_kv_norm[l][None, :], gq, gk,
                                    wq1, wq2, wk1, sk1, sk2, wv)
        kmax = jnp.sqrt(jnp.max(ksq, axis=(1, 3)))
        o_mla = _flash(kmax, qt, kh, vt, ctx_len)

        wg = jnp.stack([jnp.concatenate([_block_diag(lru_wa[l, d]), _block_diag(lru_wx[l, d])], axis=-1)
                        for d in range(2)], axis=0).astype(BF16)
        bg = jnp.concatenate([lru_ba[l], lru_bx[l]], axis=-1)[:, None, :] * jnp.ones((1, 8, 1), F32)
        lam = lru_lambda[l][:, None, :] * jnp.ones((1, 8, 1), F32)
        h_lru = _lru(p, _pad_rows(lru_conv_w[l]), lru_conv_b[l][None, :], wg, bg, lam, nctx)

        w_br = jnp.stack([w_br_hg[l], w_br_mla[l], w_br_lru[l]], axis=0).astype(BF16)
        hg_gain = jnp.tile(hg_norm[l], HG_HEADS)[None, :]
        xs = _merge(xs, mctx, mb, o_hg, p, hg_gain, head_mean, o_mla, h_lru, w_br, w_out[l].astype(BF16), ctx_len)

        wr = jnp.pad(jnp.concatenate([moe_w_re[l], moe_w_rg[l]], axis=-1), ((0, 0), (0, ROUTER_PAD - N_EXPERTS - N_GROUPS)))
        br = jnp.pad(jnp.concatenate([moe_b_re[l], moe_b_rg[l]]), (0, ROUTER_PAD - N_EXPERTS - N_GROUPS))[None, :]
        epb = N_EXPERTS // EXPERT_STEPS
        w1 = moe_w1[l].reshape(EXPERT_STEPS, epb, D_MODEL, D_EXPERT).transpose(0, 2, 1, 3).reshape(EXPERT_STEPS, D_MODEL, epb * D_EXPERT)
        w3 = moe_w3[l].reshape(EXPERT_STEPS, epb, D_MODEL, D_EXPERT).transpose(0, 2, 1, 3).reshape(EXPERT_STEPS, D_MODEL, epb * D_EXPERT)
        w13 = jnp.concatenate([w1, w3], axis=-1).astype(BF16)
        w2 = moe_w2[l].reshape(EXPERT_STEPS, epb * D_EXPERT, D_MODEL).astype(BF16)
        xs = _moe(xs, mctx, mb, norm_ffn[l][None, :], wr, br, w13, w2, ctx_len)

    return xs[:, ctx_len:, :]
```

```python
import functools

import numpy as np
import jax
import jax.numpy as jnp
from jax import lax
from jax.experimental import pallas as pl
from jax.experimental.pallas import tpu as pltpu

F32 = jnp.float32
BF16 = jnp.bfloat16
HIGHEST = lax.Precision.HIGHEST

D_MODEL = 1024
GRID_W = 64
EPS = 1e-6

HG_HEADS = 8
HG_DK = 64
HG_W = 512
HG_CHUNK = 64
HG_MID = HG_CHUNK // 2
HG_MAX_LOG_RANGE = 60.0

MLA_HEADS = 8
MLA_Q_RANK = 256
MLA_KV_RANK = 128
MLA_NOPE = 64
MLA_ROPE = 32
MLA_V = 64
MLA_QK = MLA_NOPE + MLA_ROPE
MLA_W = MLA_HEADS * MLA_V
HEAD_PAD = 128
V_ROWS = 80
SCORE_SCALE = MLA_QK ** -0.5 * 1.4426950408889634
MAX_UNSHIFTED_SCORE = 57.0
ROPE_BASE = 10000.0

LRU_W = 512
LRU_BLOCKS = 8
LRU_BD = LRU_W // LRU_BLOCKS
CONV_W = 4
LRU_C = 8.0

N_GROUPS = 4
EXP_PER_GROUP = 4
N_EXPERTS = N_GROUPS * EXP_PER_GROUP
D_EXPERT = 256
EXPERT_STEPS = 4
ROUTER_PAD = 128
MOE_TM = 768

TB = 256
HALO = 16
P_WIDTH = 6144
F_WIDTH = 1024
VMEM_LIMIT = 56 * 1024 * 1024

NT_DIMS = (((1,), (1,)), ((), ()))


def _cparams(sem):
    return pltpu.CompilerParams(dimension_semantics=sem, vmem_limit_bytes=VMEM_LIMIT)


def _const_spec(shape):
    nd = len(shape)
    return pl.BlockSpec(shape, lambda *_: (0,) * nd, pipeline_mode=pl.Buffered(1))


def _sigmoid(t):
    return 1.0 / (1.0 + jnp.exp(-t))


def _scan_block(j, d, nctx, nblk):
    fwd = j
    bwd = jnp.where(j < nctx, nctx - 1 - j, nblk - 1 - (j - nctx))
    return jnp.where(d == 0, fwd, bwd)


def _mod_kernel(c_ref, w_ref, b_ref, o_ref):
    c = c_ref[...]
    s = c * _sigmoid(c)
    o_ref[...] = jnp.dot(s, w_ref[...], precision=HIGHEST, preferred_element_type=F32) + b_ref[...]


def _modulation(cvec, w_mod, b_mod):
    n = w_mod.shape[1]
    tn = 1024
    return pl.pallas_call(
        _mod_kernel,
        grid=(n // tn,),
        in_specs=[pl.BlockSpec((8, D_MODEL), lambda j: (0, 0)),
                  pl.BlockSpec((D_MODEL, tn), lambda j: (0, j)),
                  pl.BlockSpec((1, tn), lambda j: (0, j))],
        out_specs=pl.BlockSpec((8, tn), lambda j: (0, j)),
        out_shape=jax.ShapeDtypeStruct((8, n), F32),
        compiler_params=_cparams(("arbitrary",)),
        name="modulation",
    )(cvec, w_mod, b_mod.reshape(1, n))


def _row_mods(i, tm, ctx_len, mctx_ref, mb_ref, k):
    row = i * tm + lax.broadcasted_iota(jnp.int32, (tm, 1), 0)
    return jnp.where(row < ctx_len, mctx_ref[k:k + 1, :], mb_ref[0, k:k + 1, :])


def _norm_modulate(x, gain, shift, scale):
    ms = jnp.mean(x * x, axis=-1, keepdims=True)
    xn = x * lax.rsqrt(ms + EPS) * gain
    return xn * (1.0 + scale) + shift


def _inproj_kernel(x_ref, mctx_ref, mb_ref, gain_ref, w_ref, f_ref, p_ref, *, ctx_len):
    i = pl.program_id(1)
    shift = _row_mods(i, TB, ctx_len, mctx_ref, mb_ref, 0)
    scale = _row_mods(i, TB, ctx_len, mctx_ref, mb_ref, 1)
    h = _norm_modulate(x_ref[0], gain_ref[...], shift, scale).astype(BF16)
    f_ref[0] = jnp.dot(h, w_ref[:, 0:F_WIDTH], preferred_element_type=F32)
    cw = 512
    for j in range(P_WIDTH // cw):
        lo = F_WIDTH + j * cw
        p_ref[0, :, j * cw:(j + 1) * cw] = jnp.dot(
            h, w_ref[:, lo:lo + cw], preferred_element_type=F32).astype(BF16)


def _inproj(xs, mctx, mb, gain, w_all, ctx_len):
    b, lt, _ = xs.shape
    return pl.pallas_call(
        functools.partial(_inproj_kernel, ctx_len=ctx_len),
        grid=(b, lt // TB),
        in_specs=[pl.BlockSpec((1, TB, D_MODEL), lambda bi, i: (bi, i, 0)),
                  _const_spec((6, D_MODEL)),
                  pl.BlockSpec((1, 6, D_MODEL), lambda bi, i: (bi, 0, 0)),
                  _const_spec((1, D_MODEL)),
                  _const_spec((D_MODEL, F_WIDTH + P_WIDTH))],
        out_specs=[pl.BlockSpec((1, TB, F_WIDTH), lambda bi, i: (bi, i, 0)),
                   pl.BlockSpec((1, TB, P_WIDTH), lambda bi, i: (bi, i, 0))],
        out_shape=[jax.ShapeDtypeStruct((b, lt, F_WIDTH), F32),
                   jax.ShapeDtypeStruct((b, lt, P_WIDTH), BF16)],
        compiler_params=_cparams(("parallel", "arbitrary")),
        name="inproj",
    )(xs, mctx, mb, gain, w_all)


def _hgrn_kernel(q_ref, v_ref, f_ref, tri_ref, lbc_ref, bd_ref, o_ref, st_ref, c_scr, k_scr, v_scr, *, reverse):
    j = pl.program_id(1)
    nch = TB // HG_CHUNK
    npair = HG_W // 128
    first, last = (HG_CHUNK - 1, 0) if reverse else (0, HG_CHUNK - 1)

    @pl.when(j == 0)
    def _():
        st_ref[...] = jnp.zeros_like(st_ref)

    log_lb = lbc_ref[0, 0:1, :]
    log_1mlb = lbc_ref[0, 1:2, :]
    one_mlb = lbc_ref[0, 2:3, :]
    z = f_ref[0]
    q = q_ref[0].astype(F32)
    v = v_ref[0].astype(F32)
    e = jnp.exp(-jnp.abs(z))
    log_sig = jnp.minimum(z, 0.0) - jnp.log(1.0 + e)
    t = log_1mlb + log_sig
    logf = jnp.maximum(log_lb, t) + jnp.log(1.0 + jnp.exp(-jnp.abs(log_lb - t)))
    k = one_mlb * (jnp.where(z >= 0, e, 1.0) / (1.0 + e))
    hi = logf.astype(BF16)
    lo = (logf - hi.astype(F32)).astype(BF16)
    tri = tri_ref[0]
    c = jnp.dot(tri, hi, preferred_element_type=F32) + jnp.dot(tri, lo, preferred_element_type=F32)

    span = jnp.zeros((1, HG_W), F32)
    for ci in range(nch):
        r0 = ci * HG_CHUNK
        c_mid = c[r0 + HG_MID:r0 + HG_MID + 1, :]
        span = jnp.maximum(span, jnp.maximum(c[r0 + first:r0 + first + 1, :] - c_mid, c_mid - c[r0 + last:r0 + last + 1, :]))
    span = jnp.max(span)

    lane = lax.broadcasted_iota(jnp.int32, (HG_CHUNK, HG_W), 1)
    low_head = (lane & (HG_DK * 2 - 1)) < HG_DK
    ri = lax.broadcasted_iota(jnp.int32, (HG_CHUNK, 128), 0)
    si = lax.broadcasted_iota(jnp.int32, (HG_CHUNK, 128), 1) & (HG_DK - 1)
    visited = (si >= ri) if reverse else (si <= ri)
    r2 = lax.broadcasted_iota(jnp.int32, (128, 128), 0)
    l2 = lax.broadcasted_iota(jnp.int32, (128, 128), 1)
    same_head = (r2 < HG_DK) == (l2 < HG_DK)
    rowi = lax.broadcasted_iota(jnp.int32, (HG_CHUNK, 1), 0)

    def split_heads(t):
        top = jnp.where(low_head, t, 0.0).astype(BF16)
        bot = jnp.where(low_head, 0.0, t).astype(BF16)
        return [jnp.concatenate([top[:, 128 * g:128 * (g + 1)], bot[:, 128 * g:128 * (g + 1)]], axis=0) for g in range(npair)]

    def exact_intra(qq, kk, vv, cc):
        c_scr[...] = cc
        k_scr[...] = kk
        v_scr[...] = vv
        bd = bd_ref[...]

        def key_row(s, acc):
            c_s = c_scr[pl.ds(s, 1), :]
            w = qq * jnp.exp(jnp.minimum(cc - c_s, 0.0)) * k_scr[pl.ds(s, 1), :]
            ws = jnp.dot(w, bd, precision=HIGHEST, preferred_element_type=F32)
            ok = (rowi <= s) if reverse else (rowi >= s)
            return acc + jnp.where(ok, ws, 0.0) * v_scr[pl.ds(s, 1), :]

        return lax.fori_loop(0, HG_CHUNK, key_row, jnp.zeros((HG_CHUNK, HG_W), F32))

    def chunk(ci, exact):
        r0 = ci * HG_CHUNK
        rows = slice(r0, r0 + HG_CHUNK)
        cc, qq, kk, vv = c[rows], q[rows], k[rows], v[rows]
        c_mid = cc[HG_MID:HG_MID + 1, :]
        c_end = cc[last:last + 1, :]
        qs = (qq * jnp.exp(cc)).astype(BF16)
        ks = (kk * jnp.exp(c_end - cc)).astype(BF16)
        dec = jnp.exp(c_end)
        if exact:
            intra = exact_intra(qq, kk, vv, cc)
        else:
            qd = (qq * jnp.exp(cc - c_mid)).astype(BF16)
            kd_blocks = split_heads(kk * jnp.exp(c_mid - cc))
            v_blocks = split_heads(vv)
        pieces = []
        for g in range(npair):
            ls = slice(128 * g, 128 * (g + 1))
            st = st_ref[g]
            o_g = lax.dot_general(qs[:, ls], st.astype(BF16), NT_DIMS, preferred_element_type=F32)
            if not exact:
                a = lax.dot_general(qd[:, ls], kd_blocks[g], NT_DIMS, preferred_element_type=F32)
                a = jnp.where(visited, a, 0.0).astype(BF16)
                o_g = o_g + jnp.dot(a, v_blocks[g], preferred_element_type=F32)
            vt = vv[:, ls].T.astype(BF16)
            kv = jnp.dot(vt, ks[:, ls], preferred_element_type=F32)
            st_ref[g] = st * dec[:, ls] + jnp.where(same_head, kv, 0.0)
            pieces.append(o_g)
        o = jnp.concatenate(pieces, axis=-1)
        o_ref[0, 0, rows, :] = o + intra if exact else o

    order = range(nch - 1, -1, -1) if reverse else range(nch)

    @pl.when(span <= HG_MAX_LOG_RANGE)
    def _matmul_form():
        for ci in order:
            chunk(ci, False)

    @pl.when(span > HG_MAX_LOG_RANGE)
    def _exact_form():
        for ci in order:
            chunk(ci, True)


def _hgrn(p, f, tri, lbc, bd, nctx, reverse):
    b, lt, _ = p.shape
    nblk = lt // TB
    d = 1 if reverse else 0

    def blk(j):
        return _scan_block(j, d, nctx, nblk)

    return pl.pallas_call(
        functools.partial(_hgrn_kernel, reverse=reverse),
        grid=(b, nblk),
        in_specs=[pl.BlockSpec((1, TB, HG_W), lambda bi, j: (bi, blk(j), 0)),
                  pl.BlockSpec((1, TB, HG_W), lambda bi, j: (bi, blk(j), 1)),
                  pl.BlockSpec((1, TB, HG_W), lambda bi, j: (bi, blk(j), d)),
                  pl.BlockSpec((1, TB, TB), lambda bi, j: (d, 0, 0), pipeline_mode=pl.Buffered(1)),
                  pl.BlockSpec((1, 8, HG_W), lambda bi, j: (d, 0, 0), pipeline_mode=pl.Buffered(1)),
                  _const_spec((HG_W, HG_W))],
        out_specs=pl.BlockSpec((1, 1, TB, HG_W), lambda bi, j: (0, bi, blk(j), 0)),
        out_shape=jax.ShapeDtypeStruct((1, b, lt, HG_W), F32),
        scratch_shapes=[pltpu.VMEM((HG_W // 128, 128, 128), F32),
                        pltpu.VMEM((HG_CHUNK, HG_W), F32),
                        pltpu.VMEM((HG_CHUNK, HG_W), F32),
                        pltpu.VMEM((HG_CHUNK, HG_W), F32)],
        compiler_params=_cparams(("parallel", "arbitrary")),
        name="hgrn2_bwd" if reverse else "hgrn2_fwd",
    )(p, p, f, tri, lbc, bd)


def _rms_rows(t, width):
    return lax.rsqrt(jnp.sum(t * t, axis=-1, keepdims=True) * (1.0 / width) + EPS)


def _mla_prep_kernel(pm_ref, cos_ref, sin_ref, qn_ref, kvn_ref, gq_ref, gk_ref,
                     wq1_ref, wq2_ref, wk1_ref, sk1_ref, sk2_ref, wv_ref,
                     qt_ref, k_ref, vt_ref, kn_ref):
    pm = pm_ref[0]
    dq = pm[:, 0:MLA_Q_RANK].astype(F32)
    dkv = pm[:, MLA_Q_RANK:MLA_Q_RANK + MLA_KV_RANK].astype(F32)
    dqn = (dq * _rms_rows(dq, MLA_Q_RANK) * qn_ref[...]).astype(BF16)
    dkvn = (dkv * _rms_rows(dkv, MLA_KV_RANK) * kvn_ref[...]).astype(BF16)
    q1 = jnp.dot(dqn, wq1_ref[...], preferred_element_type=F32)
    q2 = jnp.dot(dqn, wq2_ref[...], preferred_element_type=F32)
    k1 = (jnp.dot(dkvn, wk1_ref[...], preferred_element_type=F32)
          + jnp.dot(pm, sk1_ref[...], preferred_element_type=F32))
    k2 = jnp.dot(pm, sk2_ref[...], preferred_element_type=F32)
    vv = jnp.dot(dkvn, wv_ref[...], preferred_element_type=F32)
    cos = cos_ref[...]
    sin = sin_ref[...]
    gq_c = gq_ref[0:1, :] * cos
    gq_s = gq_ref[1:2, :] * sin
    gk_c = gk_ref[0:1, :] * cos
    gk_s = gk_ref[1:2, :] * sin
    vrow = lax.broadcasted_iota(jnp.int32, (V_ROWS, TB), 0)
    kn = []
    for h in range(MLA_HEADS):
        sl = slice(HEAD_PAD * h, HEAD_PAD * (h + 1))
        qh = q1[:, sl]
        qo = (qh * gq_c + q2[:, sl] * gq_s) * (_rms_rows(qh, MLA_QK) * SCORE_SCALE)
        qt_ref[0, h] = qo.T.astype(BF16)
        kh = k1[:, sl]
        ko = (kh * gk_c + k2[:, sl] * gk_s) * _rms_rows(kh, MLA_QK)
        k_ref[0, h] = ko.astype(BF16)
        ksq = jnp.max(jnp.sum(ko * ko, axis=-1, keepdims=True), axis=0, keepdims=True)
        kn.append(jnp.broadcast_to(ksq, (1, HEAD_PAD)))
        vt = vv[:, sl].T[0:V_ROWS, :]
        vt_ref[0, h] = jnp.where(vrow == MLA_V, 1.0, vt).astype(BF16)
    kn_ref[0, 0] = jnp.concatenate(kn, axis=0)


def _mla_prep(p, cos_t, sin_t, qn, kvn, gq, gk, wq1, wq2, wk1, sk1, sk2, wv):
    b, lt, _ = p.shape
    hw = MLA_HEADS * HEAD_PAD
    return pl.pallas_call(
        _mla_prep_kernel,
        grid=(b, lt // TB),
        in_specs=[pl.BlockSpec((1, TB, 512), lambda bi, i: (bi, i, 3)),
                  pl.BlockSpec((TB, HEAD_PAD), lambda bi, i: (i, 0)),
                  pl.BlockSpec((TB, HEAD_PAD), lambda bi, i: (i, 0)),
                  _const_spec((1, MLA_Q_RANK)), _const_spec((1, MLA_KV_RANK)),
                  _const_spec((2, HEAD_PAD)), _const_spec((2, HEAD_PAD)),
                  _const_spec((MLA_Q_RANK, hw)), _const_spec((MLA_Q_RANK, hw)),
                  _const_spec((MLA_KV_RANK, hw)), _const_spec((512, hw)), _const_spec((512, hw)),
                  _const_spec((MLA_KV_RANK, hw))],
        out_specs=[pl.BlockSpec((1, MLA_HEADS, HEAD_PAD, TB), lambda bi, i: (bi, 0, 0, i)),
                   pl.BlockSpec((1, MLA_HEADS, TB, HEAD_PAD), lambda bi, i: (bi, 0, i, 0)),
                   pl.BlockSpec((1, MLA_HEADS, V_ROWS, TB), lambda bi, i: (bi, 0, 0, i)),
                   pl.BlockSpec((1, 1, MLA_HEADS, HEAD_PAD), lambda bi, i: (bi, i, 0, 0))],
        out_shape=[jax.ShapeDtypeStruct((b, MLA_HEADS, HEAD_PAD, lt), BF16),
                   jax.ShapeDtypeStruct((b, MLA_HEADS, lt, HEAD_PAD), BF16),
                   jax.ShapeDtypeStruct((b, MLA_HEADS, V_ROWS, lt), BF16),
                   jax.ShapeDtypeStruct((b, lt // TB, MLA_HEADS, HEAD_PAD), F32)],
        compiler_params=_cparams(("parallel", "arbitrary")),
        name="mla_prep",
    )(p, cos_t, sin_t, qn, kvn, gq, gk, wq1, wq2, wk1, sk1, sk2, wv)


def _attend_bounded(qts, k_ref, vt_ref, s_scr, nkeys, tk):
    n = nkeys // tk

    def scores(kb, slot):
        for hh in range(2):
            s_scr[slot, hh, 0:tk, :] = jnp.dot(k_ref[0, hh, kb * tk:(kb + 1) * tk, :], qts[hh], preferred_element_type=F32)

    def accumulate(kb, slot, accs):
        out = []
        for hh in range(2):
            p = jnp.exp2(s_scr[slot, hh, 0:tk, :]).astype(BF16)
            pv = jnp.dot(vt_ref[0, hh, :, kb * tk:(kb + 1) * tk], p, preferred_element_type=F32)
            out.append(pv if accs is None else accs[hh] + pv)
        return out

    scores(0, 0)
    accs = None
    for kb in range(n):
        if kb + 1 < n:
            scores(kb + 1, (kb + 1) % 2)
        accs = accumulate(kb, kb % 2, accs)
    return accs


def _attend_online(qts, k_ref, vt_ref, nkeys, tk):
    def kv_step(kb, carry):
        r0 = pl.multiple_of(kb * tk, tk)
        out = []
        for hh in range(2):
            m, acc = carry[hh]
            s = jnp.dot(k_ref[0, hh, pl.ds(r0, tk), :], qts[hh], preferred_element_type=F32)
            m_new = jnp.maximum(m, jnp.max(s, axis=0, keepdims=True))
            p = jnp.exp2(s - m_new).astype(BF16)
            acc = jnp.exp2(m - m_new) * acc + jnp.dot(vt_ref[0, hh, :, pl.ds(r0, tk)], p, preferred_element_type=F32)
            out.append((m_new, acc))
        return tuple(out)

    init = (jnp.full((1, TB), -jnp.inf, F32), jnp.zeros((V_ROWS, TB), F32))
    res = lax.fori_loop(0, nkeys // tk, kv_step, (init, init))
    return res[0][1], res[1][1]


def _flash_kernel(kmax_ref, qt_ref, k_ref, vt_ref, o_ref, s_scr, *, ctx_len, lt, tk):
    bi = pl.program_id(0)
    g = pl.program_id(1)
    i = pl.program_id(2)
    nctx = ctx_len // TB

    qts = [qt_ref[0, hh] for hh in range(2)]
    worst = 0.0
    for hh in range(2):
        qt = qts[hh].astype(F32)
        qnorm = jnp.sqrt(jnp.max(jnp.sum(qt * qt, axis=0, keepdims=True)))
        worst = jnp.maximum(worst, qnorm * kmax_ref[bi, 2 * g + hh])

    def finish(accs):
        outs = [(acc[0:MLA_V, :] / acc[MLA_V:MLA_V + 1, :]).T for acc in accs]
        o_ref[0] = jnp.concatenate(outs, axis=-1).astype(BF16)

    def run(nkeys, tkk):
        @pl.when(worst <= MAX_UNSHIFTED_SCORE)
        def _():
            finish(_attend_bounded(qts, k_ref, vt_ref, s_scr, nkeys, tkk))

        @pl.when(worst > MAX_UNSHIFTED_SCORE)
        def _():
            finish(_attend_online(qts, k_ref, vt_ref, nkeys, tkk))

    @pl.when(i < nctx)
    def _():
        run(ctx_len, TB)

    @pl.when(i >= nctx)
    def _():
        run(lt, tk)


def _flash(kmax, qt, k, vt, ctx_len):
    b, _, lt, _ = k.shape
    tk = next(t for t in (768, 512, 256) if lt % t == 0)
    return pl.pallas_call(
        functools.partial(_flash_kernel, ctx_len=ctx_len, lt=lt, tk=tk),
        grid=(b, MLA_HEADS // 2, lt // TB),
        in_specs=[pl.BlockSpec(memory_space=pltpu.SMEM),
                  pl.BlockSpec((1, 2, HEAD_PAD, TB), lambda bi, g, i: (bi, g, 0, i)),
                  pl.BlockSpec((1, 2, lt, HEAD_PAD), lambda bi, g, i: (bi, g, 0, 0)),
                  pl.BlockSpec((1, 2, V_ROWS, lt), lambda bi, g, i: (bi, g, 0, 0))],
        out_specs=pl.BlockSpec((1, TB, HEAD_PAD), lambda bi, g, i: (bi, i, g)),
        out_shape=jax.ShapeDtypeStruct((b, lt, MLA_W), BF16),
        scratch_shapes=[pltpu.VMEM((2, 2, tk, TB), F32)],
        compiler_params=_cparams(("parallel", "parallel", "arbitrary")),
        name="mla_attention",
    )(kmax, qt, k, vt)


def _lru_kernel(x_ref, xp_ref, xn_ref, cw_ref, cb_ref, wg_ref, bg_ref, lam_ref, o_ref, xe_scr, h_scr, *, nctx, nblk):
    d = pl.program_id(1)
    j = pl.program_id(2)
    blk = _scan_block(j, d, nctx, nblk)

    @pl.when(j == 0)
    def _():
        h_scr[...] = jnp.zeros_like(h_scr)

    keep_prev = jnp.logical_and(blk != 0, blk != nctx)
    keep_next = jnp.logical_and(blk != nctx - 1, blk != nblk - 1)
    xe_scr[0:HALO, :] = jnp.where(keep_prev, xp_ref[0].astype(F32), 0.0)
    xe_scr[HALO:HALO + TB, :] = x_ref[0].astype(F32)
    xe_scr[HALO + TB:2 * HALO + TB, :] = jnp.where(keep_next, xn_ref[0].astype(F32), 0.0)
    left = CONV_W // 2
    u = jnp.broadcast_to(cb_ref[...], (TB, LRU_W))
    for tap in range(CONV_W):
        u = u + xe_scr[pl.ds(HALO - left + tap, TB), :] * cw_ref[tap:tap + 1, :]

    g = jnp.dot(u.astype(BF16), wg_ref[0], preferred_element_type=F32) + bg_ref[0, 0:1, :]
    r = _sigmoid(g[:, 0:LRU_W])
    ig = _sigmoid(g[:, LRU_W:2 * LRU_W])
    neg_lam = -lam_ref[0, 0:1, :]
    softplus = jnp.maximum(neg_lam, 0.0) + jnp.log1p(jnp.exp(-jnp.abs(neg_lam)))
    a = jnp.exp(-LRU_C * r * softplus)
    bb = jnp.sqrt(1.0 - a * a) * (ig * u)
    rowi = lax.broadcasted_iota(jnp.int32, (TB, 1), 0)

    def scan(forward):
        aa, hh = a, bb
        k = 1
        while k < TB:
            shift = k if forward else TB - k
            valid = (rowi >= k) if forward else (rowi < TB - k)
            a_sh = jnp.where(valid, pltpu.roll(aa, shift, 0), 1.0)
            h_sh = jnp.where(valid, pltpu.roll(hh, shift, 0), 0.0)
            hh = aa * h_sh + hh
            aa = aa * a_sh
            k *= 2
        hh = hh + aa * h_scr[...]
        o_ref[0, 0] = hh
        last = TB - 1 if forward else 0
        h_scr[...] = hh[last:last + 1, :]

    @pl.when(d == 0)
    def _():
        scan(True)

    @pl.when(d == 1)
    def _():
        scan(False)


def _lru(p, conv_w, conv_b, wg, bg, lam, nctx):
    b, lt, _ = p.shape
    nblk = lt // TB
    hpb = TB // HALO
    nh = lt // HALO

    def blk(d, j):
        return _scan_block(j, d, nctx, nblk)

    return pl.pallas_call(
        functools.partial(_lru_kernel, nctx=nctx, nblk=nblk),
        grid=(b, 2, nblk),
        in_specs=[pl.BlockSpec((1, TB, LRU_W), lambda bi, d, j: (bi, blk(d, j), 4)),
                  pl.BlockSpec((1, HALO, LRU_W), lambda bi, d, j: (bi, jnp.maximum(blk(d, j) * hpb - 1, 0), 4)),
                  pl.BlockSpec((1, HALO, LRU_W), lambda bi, d, j: (bi, jnp.minimum((blk(d, j) + 1) * hpb, nh - 1), 4)),
                  _const_spec((8, LRU_W)), _const_spec((1, LRU_W)),
                  pl.BlockSpec((1, LRU_W, 2 * LRU_W), lambda bi, d, j: (d, 0, 0)),
                  pl.BlockSpec((1, 8, 2 * LRU_W), lambda bi, d, j: (d, 0, 0)),
                  pl.BlockSpec((1, 8, LRU_W), lambda bi, d, j: (d, 0, 0))],
        out_specs=pl.BlockSpec((1, 1, TB, LRU_W), lambda bi, d, j: (d, bi, blk(d, j), 0)),
        out_shape=jax.ShapeDtypeStruct((2, b, lt, LRU_W), F32),
        scratch_shapes=[pltpu.VMEM((TB + 2 * HALO, LRU_W), F32), pltpu.VMEM((1, LRU_W), F32)],
        compiler_params=_cparams(("parallel", "arbitrary", "arbitrary")),
        name="rglru",
    )(p, p, p, conv_w, conv_b, wg, bg, lam)


def _gelu_tanh(t):
    return 0.5 * t * (1.0 + jnp.tanh(0.7978845608028654 * (t + 0.044715 * t * t * t)))


def _merge_kernel(x_ref, mctx_ref, mb_ref, hgf_ref, hgb_ref, hgg_ref, hgn_ref, hm_ref, mla_ref,
                  lf_ref, lb_ref, ly_ref, g1_ref, g2_ref, g3_ref, wbr_ref, wout_ref, o_ref, *, ctx_len):
    i = pl.program_id(1)
    o = hgf_ref[0, 0] + hgb_ref[0, 0]
    ms = jnp.dot((o * o).astype(BF16), hm_ref[...], preferred_element_type=F32)
    o_hg = o * lax.rsqrt(ms + EPS) * hgn_ref[...] * _sigmoid(hgg_ref[0].astype(F32))
    o_lru = (lf_ref[0, 0] + lb_ref[0, 0]) * _gelu_tanh(ly_ref[0].astype(F32))
    y = (_sigmoid(g1_ref[0].astype(F32)) * jnp.dot(o_hg.astype(BF16), wbr_ref[0], preferred_element_type=F32)
         + _sigmoid(g2_ref[0].astype(F32)) * jnp.dot(mla_ref[0], wbr_ref[1], preferred_element_type=F32)
         + _sigmoid(g3_ref[0].astype(F32)) * jnp.dot(o_lru.astype(BF16), wbr_ref[2], preferred_element_type=F32))
    gate = _row_mods(i, TB, ctx_len, mctx_ref, mb_ref, 2)
    o_ref[0] = x_ref[0] + gate * jnp.dot(y.astype(BF16), wout_ref[...], preferred_element_type=F32)


def _merge(xs, mctx, mb, o_hg, p, hg_gain, head_mean, o_mla, h_lru, w_br, w_out, ctx_len):
    b, lt, _ = xs.shape

    def pcol(width, c):
        return pl.BlockSpec((1, TB, width), lambda bi, i: (bi, i, c))

    def dirspec(d):
        return pl.BlockSpec((1, 1, TB, 512), lambda bi, i: (d, bi, i, 0))

    return pl.pallas_call(
        functools.partial(_merge_kernel, ctx_len=ctx_len),
        grid=(b, lt // TB),
        in_specs=[pl.BlockSpec((1, TB, D_MODEL), lambda bi, i: (bi, i, 0)),
                  _const_spec((6, D_MODEL)),
                  pl.BlockSpec((1, 6, D_MODEL), lambda bi, i: (bi, 0, 0)),
                  dirspec(0), dirspec(0), pcol(512, 2),
                  _const_spec((1, HG_W)), _const_spec((HG_W, HG_W)),
                  pl.BlockSpec((1, TB, MLA_W), lambda bi, i: (bi, i, 0)),
                  dirspec(0), dirspec(1), pcol(512, 5),
                  pcol(D_MODEL, 3), pcol(D_MODEL, 4), pcol(D_MODEL, 5),
                  _const_spec((3, 512, D_MODEL)), _const_spec((D_MODEL, D_MODEL))],
        out_specs=pl.BlockSpec((1, TB, D_MODEL), lambda bi, i: (bi, i, 0)),
        out_shape=jax.ShapeDtypeStruct((b, lt, D_MODEL), F32),
        compiler_params=_cparams(("parallel", "arbitrary")),
        name="merge",
    )(xs, mctx, mb, o_hg[0], o_hg[1], p, hg_gain, head_mean, o_mla, h_lru, h_lru, p, p, p, p, w_br, w_out)


def _first_index_of_max(vals, lane, valid):
    masked = jnp.where(valid, vals, -jnp.inf)
    m = jnp.max(masked, axis=-1, keepdims=True)
    idx = jnp.min(jnp.where(masked == m, lane, ROUTER_PAD), axis=-1, keepdims=True)
    return m, idx


def _router(h, wr, br):
    logits = jnp.dot(h, wr, precision=HIGHEST, preferred_element_type=F32)
    lane = lax.broadcasted_iota(jnp.int32, logits.shape, 1)
    biased = logits + br
    is_group = jnp.logical_and(lane >= N_EXPERTS, lane < N_EXPERTS + N_GROUPS)
    _, g_lane = _first_index_of_max(biased, lane, is_group)
    g_max, _ = _first_index_of_max(logits, lane, is_group)
    g_exp = jnp.where(is_group, jnp.exp(logits - g_max), 0.0)
    g_sel_logit = jnp.sum(jnp.where(lane == g_lane, logits, 0.0), axis=-1, keepdims=True)
    p_g = jnp.exp(g_sel_logit - g_max) / jnp.sum(g_exp, axis=-1, keepdims=True)
    in_group = jnp.right_shift(lane, 2) == (g_lane - N_EXPERTS)
    _, i1 = _first_index_of_max(biased, lane, in_group)
    _, i2 = _first_index_of_max(biased, lane, jnp.logical_and(in_group, lane != i1))
    l1 = jnp.sum(jnp.where(lane == i1, logits, 0.0), axis=-1, keepdims=True)
    l2 = jnp.sum(jnp.where(lane == i2, logits, 0.0), axis=-1, keepdims=True)
    lm = jnp.maximum(l1, l2)
    e1 = jnp.exp(l1 - lm)
    e2 = jnp.exp(l2 - lm)
    inv = p_g / (e1 + e2)
    return jnp.where(lane == i1, e1 * inv, 0.0) + jnp.where(lane == i2, e2 * inv, 0.0)


def _moe_kernel(x_ref, mctx_ref, mb_ref, gain_ref, wr_ref, br_ref, w13_ref, w2_ref, o_ref,
                h_scr, comb_scr, acc_scr, *, ctx_len, tm):
    i = pl.program_id(1)
    e = pl.program_id(2)
    epb = N_EXPERTS // EXPERT_STEPS
    hw = epb * D_EXPERT

    @pl.when(e == 0)
    def _():
        shift = _row_mods(i, tm, ctx_len, mctx_ref, mb_ref, 3)
        scale = _row_mods(i, tm, ctx_len, mctx_ref, mb_ref, 4)
        h = _norm_modulate(x_ref[0], gain_ref[...], shift, scale)
        h_scr[...] = h.astype(BF16)
        comb_scr[...] = _router(h, wr_ref[...], br_ref[...])
        acc_scr[...] = jnp.zeros_like(acc_scr)

    h = h_scr[...]
    h13 = jnp.dot(h, w13_ref[0], preferred_element_type=F32)
    h1 = h13[:, 0:hw]
    hid = h1 * _sigmoid(h1) * h13[:, hw:2 * hw]
    comb = comb_scr[...]
    lane = lax.broadcasted_iota(jnp.int32, comb.shape, 1)
    parts = []
    for k in range(epb):
        wk = jnp.sum(jnp.where(lane == e * epb + k, comb, 0.0), axis=-1, keepdims=True)
        parts.append(hid[:, k * D_EXPERT:(k + 1) * D_EXPERT] * wk)
    hid = jnp.concatenate(parts, axis=-1).astype(BF16)
    acc_scr[...] += jnp.dot(hid, w2_ref[0], preferred_element_type=F32)

    @pl.when(e == EXPERT_STEPS - 1)
    def _():
        gate = _row_mods(i, tm, ctx_len, mctx_ref, mb_ref, 5)
        o_ref[0] = x_ref[0] + gate * acc_scr[...]


def _moe(xs, mctx, mb, gain, wr, br, w13, w2, ctx_len):
    b, lt, _ = xs.shape
    tm = MOE_TM if lt % MOE_TM == 0 else TB
    epb = N_EXPERTS // EXPERT_STEPS
    return pl.pallas_call(
        functools.partial(_moe_kernel, ctx_len=ctx_len, tm=tm),
        grid=(b, lt // tm, EXPERT_STEPS),
        in_specs=[pl.BlockSpec((1, tm, D_MODEL), lambda bi, i, e: (bi, i, 0)),
                  _const_spec((6, D_MODEL)),
                  pl.BlockSpec((1, 6, D_MODEL), lambda bi, i, e: (bi, 0, 0)),
                  _const_spec((1, D_MODEL)),
                  _const_spec((D_MODEL, ROUTER_PAD)), _const_spec((1, ROUTER_PAD)),
                  pl.BlockSpec((1, D_MODEL, 2 * epb * D_EXPERT), lambda bi, i, e: (e, 0, 0)),
                  pl.BlockSpec((1, epb * D_EXPERT, D_MODEL), lambda bi, i, e: (e, 0, 0))],
        out_specs=pl.BlockSpec((1, tm, D_MODEL), lambda bi, i, e: (bi, i, 0)),
        out_shape=jax.ShapeDtypeStruct((b, lt, D_MODEL), F32),
        scratch_shapes=[pltpu.VMEM((tm, D_MODEL), BF16), pltpu.VMEM((tm, ROUTER_PAD), F32),
                        pltpu.VMEM((tm, D_MODEL), F32)],
        compiler_params=_cparams(("parallel", "arbitrary", "arbitrary")),
        name="moe",
    )(xs, mctx, mb, gain, wr, br, w13, w2)


def _rope_tables(ctx_len, seq):
    half = MLA_ROPE // 2
    rows = seq // GRID_W
    pos_row = jnp.repeat(jnp.arange(rows, dtype=F32), GRID_W)
    pos_col = jnp.tile(jnp.arange(GRID_W, dtype=F32), rows)
    inv = ROPE_BASE ** (-jnp.arange(0, half, 2, dtype=F32) / half)
    ang = jnp.concatenate([pos_row[:, None] * inv, pos_col[:, None] * inv], axis=-1)
    cos, sin = jnp.cos(ang), jnp.sin(ang)
    ones = jnp.ones((seq, MLA_NOPE), F32)
    zpad = jnp.zeros((seq, HEAD_PAD - MLA_QK), F32)
    cos_x = jnp.concatenate([ones, cos, cos, zpad], axis=-1)
    sin_x = jnp.concatenate([jnp.zeros((seq, MLA_NOPE), F32), -sin, sin, zpad], axis=-1)
    cos_c = jnp.concatenate([jnp.ones((ctx_len, MLA_QK), F32), jnp.zeros((ctx_len, HEAD_PAD - MLA_QK), F32)], axis=-1)
    sin_c = jnp.zeros((ctx_len, HEAD_PAD), F32)
    return jnp.concatenate([cos_c, cos_x], axis=0), jnp.concatenate([sin_c, sin_x], axis=0)


def _swap_rope_halves(t):
    half = MLA_ROPE // 2
    return jnp.concatenate([jnp.zeros_like(t[..., :MLA_NOPE]), t[..., MLA_NOPE + half:], t[..., MLA_NOPE:MLA_NOPE + half]],
                           axis=-1)


def _pad_heads(t):
    pad = [(0, 0)] * (t.ndim - 1) + [(0, HEAD_PAD - t.shape[-1])]
    t = jnp.pad(t, pad)
    return t.reshape(t.shape[:-2] + (t.shape[-2] * HEAD_PAD,))


def _mla_weights(w_uq, w_ukv, gq, gk):
    wq = w_uq.reshape(MLA_Q_RANK, MLA_HEADS, MLA_QK)
    wq1 = _pad_heads(wq).astype(BF16)
    wq2 = _pad_heads(_swap_rope_halves(wq)).astype(BF16)
    wkv = w_ukv.reshape(MLA_KV_RANK, MLA_HEADS, MLA_NOPE + MLA_V)
    wk1 = _pad_heads(wkv[..., :MLA_NOPE]).astype(BF16)
    kr0 = MLA_Q_RANK + MLA_KV_RANK
    eye = jnp.eye(MLA_ROPE, dtype=F32)
    place = jnp.concatenate([jnp.zeros((MLA_ROPE, MLA_NOPE), F32), eye], axis=-1)
    place = jnp.broadcast_to(place[:, None, :], (MLA_ROPE, MLA_HEADS, MLA_QK))
    zrow_a = jnp.zeros((kr0, MLA_HEADS * HEAD_PAD), F32)
    zrow_b = jnp.zeros((512 - kr0 - MLA_ROPE, MLA_HEADS * HEAD_PAD), F32)
    sk1 = jnp.concatenate([zrow_a, _pad_heads(place), zrow_b], axis=0).astype(BF16)
    sk2 = jnp.concatenate([zrow_a, _pad_heads(_swap_rope_halves(place)), zrow_b], axis=0).astype(BF16)
    wv = _pad_heads(wkv[..., MLA_NOPE:]).astype(BF16)

    def gains(g):
        g1 = jnp.pad(g, (0, HEAD_PAD - MLA_QK))
        g2 = jnp.pad(_swap_rope_halves(g), (0, HEAD_PAD - MLA_QK))
        return jnp.stack([g1, g2], axis=0)

    return wq1, wq2, wk1, sk1, sk2, wv, gains(gq), gains(gk)


def _block_diag(w):
    n, a, bb = w.shape
    eye = jnp.eye(n, dtype=w.dtype)
    return (eye[:, None, :, None] * w[:, :, None, :]).reshape(n * a, n * bb)


def _pad_rows(t, rows=8):
    return jnp.pad(t, ((0, rows - t.shape[0]), (0, 0)))


def kernel(x, c, ctx, c_ctx, w_mod, b_mod, norm_mix, norm_ffn, w_in, hg_lb, hg_norm, mla_q_norm, mla_kv_norm, mla_w_uq, mla_w_ukv, mla_qk_gain_q, mla_qk_gain_k, lru_conv_w, lru_conv_b, lru_wa, lru_ba, lru_wx, lru_bx, lru_lambda, w_br_hg, w_br_mla, w_br_lru, w_out, moe_w_rg, moe_b_rg, moe_w_re, moe_b_re, moe_w1, moe_w3, moe_w2):
    bsz, seq, _ = x.shape
    ctx_len = ctx.shape[1]
    depth = w_in.shape[0]
    assert seq % TB == 0 and ctx_len % TB == 0 and seq % GRID_W == 0 and bsz < 8
    nctx = ctx_len // TB

    xs = jnp.concatenate([ctx, x], axis=1)
    cos_t, sin_t = _rope_tables(ctx_len, seq)
    cvec = jnp.zeros((8, D_MODEL), F32).at[:bsz].set(c).at[bsz].set(c_ctx)

    lb_cs = jnp.cumsum(jax.nn.softmax(hg_lb.astype(F32), axis=0), axis=0)
    lb_all = lb_cs - lb_cs[0:1]
    tri_lo = _block_diag(jnp.broadcast_to(jnp.tril(jnp.ones((HG_CHUNK, HG_CHUNK), F32)), (TB // HG_CHUNK, HG_CHUNK, HG_CHUNK)))
    tri = jnp.stack([tri_lo, tri_lo.T], axis=0).astype(BF16)
    head_ones = _block_diag(jnp.ones((HG_HEADS, HG_DK, HG_DK), F32))
    head_mean = (head_ones / HG_DK).astype(BF16)

    offs = np.cumsum((HG_W,) * 5 + (MLA_Q_RANK, MLA_KV_RANK, MLA_ROPE, LRU_W, LRU_W) + (D_MODEL,) * 3)[:-1].tolist()

    for l in range(depth):
        mods = _modulation(cvec, w_mod[l], b_mod[l]).reshape(8, 6, D_MODEL)
        mctx, mb = mods[bsz], mods[:bsz]

        q_, ff, fb, i_, g_, dq, dkv, kr, lx, ly, g1, g2, g3 = jnp.split(w_in[l], offs, axis=-1)
        kr_pad = jnp.zeros((D_MODEL, 512 - MLA_Q_RANK - MLA_KV_RANK - MLA_ROPE), F32)
        w_all = jnp.concatenate([ff, fb, q_, i_, g_, dq, dkv, kr, kr_pad, lx, ly, g1, g2, g3], axis=-1).astype(BF16)
        f, p = _inproj(xs, mctx, mb, norm_mix[l][None, :], w_all, ctx_len)

        lb = lb_all[l]
        lbc = jnp.stack([jnp.log(lb), jnp.log1p(-lb), 1.0 - lb] + [jnp.zeros_like(lb)] * 5, axis=1)
        o_hg = [_hgrn(p, f, tri, lbc, head_ones, nctx, reverse) for reverse in (False, True)]

        wq1, wq2, wk1, sk1, sk2, wv, gq, gk = _mla_weights(mla_w_uq[l], mla_w_ukv[l], mla_qk_gain_q[l], mla_qk_gain_k[l])
        qt, kh, vt, ksq = _mla_prep(p, cos_t, sin_t, mla_q_norm[l][None, :], mla_kv_norm[l][None, :], gq, gk,
                                    wq1, wq2, wk1, sk1, sk2, wv)
        kmax = jnp.sqrt(jnp.max(ksq, axis=(1, 3)))
        o_mla = _flash(kmax, qt, kh, vt, ctx_len)

        wg = jnp.stack([jnp.concatenate([_block_diag(lru_wa[l, d]), _block_diag(lru_wx[l, d])], axis=-1)
                        for d in range(2)], axis=0).astype(BF16)
        bg = jnp.concatenate([lru_ba[l], lru_bx[l]], axis=-1)[:, None, :] * jnp.ones((1, 8, 1), F32)
        lam = lru_lambda[l][:, None, :] * jnp.ones((1, 8, 1), F32)
        h_lru = _lru(p, _pad_rows(lru_conv_w[l]), lru_conv_b[l][None, :], wg, bg, lam, nctx)

        w_br = jnp.stack([w_br_hg[l], w_br_mla[l], w_br_lru[l]], axis=0).astype(BF16)
        hg_gain = jnp.tile(hg_norm[l], HG_HEADS)[None, :]
        xs = _merge(xs, mctx, mb, o_hg, p, hg_gain, head_mean, o_mla, h_lru, w_br, w_out[l].astype(BF16), ctx_len)

        wr = jnp.pad(jnp.concatenate([moe_w_re[l], moe_w_rg[l]], axis=-1), ((0, 0), (0, ROUTER_PAD - N_EXPERTS - N_GROUPS)))
        br = jnp.pad(jnp.concatenate([moe_b_re[l], moe_b_rg[l]]), (0, ROUTER_PAD - N_EXPERTS - N_GROUPS))[None, :]
        epb = N_EXPERTS // EXPERT_STEPS
        w1 = moe_w1[l].reshape(EXPERT_STEPS, epb, D_MODEL, D_EXPERT).transpose(0, 2, 1, 3).reshape(EXPERT_STEPS, D_MODEL, epb * D_EXPERT)
        w3 = moe_w3[l].reshape(EXPERT_STEPS, epb, D_MODEL, D_EXPERT).transpose(0, 2, 1, 3).reshape(EXPERT_STEPS, D_MODEL, epb * D_EXPERT)
        w13 = jnp.concatenate([w1, w3], axis=-1).astype(BF16)
        w2 = moe_w2[l].reshape(EXPERT_STEPS, epb * D_EXPERT, D_MODEL).astype(BF16)
        xs = _moe(xs, mctx, mb, norm_ffn[l][None, :], wr, br, w13, w2, ctx_len)

    return xs[:, ctx_len:, :]
```

```python
import functools

import numpy as np
import jax
import jax.numpy as jnp
from jax import lax
from jax.experimental import pallas as pl
from jax.experimental.pallas import tpu as pltpu

F32 = jnp.float32
BF16 = jnp.bfloat16
HIGHEST = lax.Precision.HIGHEST

D_MODEL = 1024
GRID_W = 64
EPS = 1e-6

HG_HEADS = 8
HG_DK = 64
HG_W = 512
HG_CHUNK = 64
HG_MID = HG_CHUNK // 2
HG_MAX_LOG_RANGE = 80.0

MLA_HEADS = 8
MLA_Q_RANK = 256
MLA_KV_RANK = 128
MLA_NOPE = 64
MLA_ROPE = 32
MLA_V = 64
MLA_QK = MLA_NOPE + MLA_ROPE
MLA_W = MLA_HEADS * MLA_V
HEAD_PAD = 128
V_ROWS = 80
SCORE_SCALE = MLA_QK ** -0.5 * 1.4426950408889634
MAX_UNSHIFTED_SCORE = 57.0
ROPE_BASE = 10000.0

LRU_W = 512
LRU_BLOCKS = 8
LRU_BD = LRU_W // LRU_BLOCKS
CONV_W = 4
LRU_C = 8.0

N_GROUPS = 4
EXP_PER_GROUP = 4
N_EXPERTS = N_GROUPS * EXP_PER_GROUP
D_EXPERT = 256
EXPERT_STEPS = 4
ROUTER_PAD = 128
MOE_TM = 768
MOE_RB = 256

TB = 256
HALO = 16
P_WIDTH = 6144
F_WIDTH = 1024
VMEM_LIMIT = 56 * 1024 * 1024

NT_DIMS = (((1,), (1,)), ((), ()))


def _cparams(sem):
    return pltpu.CompilerParams(dimension_semantics=sem, vmem_limit_bytes=VMEM_LIMIT)


def _const_spec(shape):
    nd = len(shape)
    return pl.BlockSpec(shape, lambda *_: (0,) * nd, pipeline_mode=pl.Buffered(1))


def _sigmoid(t):
    return 1.0 / (1.0 + jnp.exp(-t))


def _scan_block(j, d, nctx, nblk):
    fwd = j
    bwd = jnp.where(j < nctx, nctx - 1 - j, nblk - 1 - (j - nctx))
    return jnp.where(d == 0, fwd, bwd)


def _mod_kernel(c_ref, w_ref, b_ref, o_ref):
    c = c_ref[...]
    s = c * _sigmoid(c)
    o_ref[...] = jnp.dot(s, w_ref[...], precision=HIGHEST, preferred_element_type=F32) + b_ref[...]


def _modulation(cvec, w_mod, b_mod):
    n = w_mod.shape[1]
    tn = 1024
    return pl.pallas_call(
        _mod_kernel,
        grid=(n // tn,),
        in_specs=[pl.BlockSpec((8, D_MODEL), lambda j: (0, 0)),
                  pl.BlockSpec((D_MODEL, tn), lambda j: (0, j)),
                  pl.BlockSpec((1, tn), lambda j: (0, j))],
        out_specs=pl.BlockSpec((8, tn), lambda j: (0, j)),
        out_shape=jax.ShapeDtypeStruct((8, n), F32),
        compiler_params=_cparams(("arbitrary",)),
        name="modulation",
    )(cvec, w_mod, b_mod.reshape(1, n))


def _row_mods(i, tm, ctx_len, mctx_ref, mb_ref, k):
    row = i * tm + lax.broadcasted_iota(jnp.int32, (tm, 1), 0)
    return jnp.where(row < ctx_len, mctx_ref[k:k + 1, :], mb_ref[0, k:k + 1, :])


def _norm_modulate(x, gain, shift, scale):
    ms = jnp.mean(x * x, axis=-1, keepdims=True)
    xn = x * lax.rsqrt(ms + EPS) * gain
    return xn * (1.0 + scale) + shift


def _inproj_kernel(x_ref, mctx_ref, mb_ref, gain_ref, w_ref, f_ref, p_ref, *, ctx_len):
    i = pl.program_id(1)
    shift = _row_mods(i, TB, ctx_len, mctx_ref, mb_ref, 0)
    scale = _row_mods(i, TB, ctx_len, mctx_ref, mb_ref, 1)
    h = _norm_modulate(x_ref[0], gain_ref[...], shift, scale).astype(BF16)
    f_ref[0] = jnp.dot(h, w_ref[:, 0:F_WIDTH], preferred_element_type=F32)
    cw = 512
    for j in range(P_WIDTH // cw):
        lo = F_WIDTH + j * cw
        p_ref[0, :, j * cw:(j + 1) * cw] = jnp.dot(
            h, w_ref[:, lo:lo + cw], preferred_element_type=F32).astype(BF16)


def _inproj(xs, mctx, mb, gain, w_all, ctx_len):
    b, lt, _ = xs.shape
    return pl.pallas_call(
        functools.partial(_inproj_kernel, ctx_len=ctx_len),
        grid=(b, lt // TB),
        in_specs=[pl.BlockSpec((1, TB, D_MODEL), lambda bi, i: (bi, i, 0)),
                  _const_spec((6, D_MODEL)),
                  pl.BlockSpec((1, 6, D_MODEL), lambda bi, i: (bi, 0, 0)),
                  _const_spec((1, D_MODEL)),
                  _const_spec((D_MODEL, F_WIDTH + P_WIDTH))],
        out_specs=[pl.BlockSpec((1, TB, F_WIDTH), lambda bi, i: (bi, i, 0)),
                   pl.BlockSpec((1, TB, P_WIDTH), lambda bi, i: (bi, i, 0))],
        out_shape=[jax.ShapeDtypeStruct((b, lt, F_WIDTH), F32),
                   jax.ShapeDtypeStruct((b, lt, P_WIDTH), BF16)],
        compiler_params=_cparams(("parallel", "arbitrary")),
        name="inproj",
    )(xs, mctx, mb, gain, w_all)


def _hgrn_kernel(q_ref, v_ref, f_ref, tri_ref, lbc_ref, bd_ref, o_ref, st_ref, c_scr, k_scr, v_scr, *, reverse):
    j = pl.program_id(1)
    nch = TB // HG_CHUNK
    npair = HG_W // 128
    first, last = (HG_CHUNK - 1, 0) if reverse else (0, HG_CHUNK - 1)

    @pl.when(j == 0)
    def _():
        st_ref[...] = jnp.zeros_like(st_ref)

    log_lb = lbc_ref[0, 0:1, :]
    log_1mlb = lbc_ref[0, 1:2, :]
    one_mlb = lbc_ref[0, 2:3, :]
    z = f_ref[0]
    q = q_ref[0].astype(F32)
    v = v_ref[0].astype(F32)
    e = jnp.exp(-jnp.abs(z))
    log_sig = jnp.minimum(z, 0.0) - jnp.log(1.0 + e)
    t = log_1mlb + log_sig
    logf = jnp.maximum(log_lb, t) + jnp.log(1.0 + jnp.exp(-jnp.abs(log_lb - t)))
    k = one_mlb * (jnp.where(z >= 0, e, 1.0) / (1.0 + e))
    hi = logf.astype(BF16)
    lo = (logf - hi.astype(F32)).astype(BF16)
    tri = tri_ref[0]
    c = jnp.dot(tri, hi, preferred_element_type=F32) + jnp.dot(tri, lo, preferred_element_type=F32)

    span = jnp.zeros((1, HG_W), F32)
    for ci in range(nch):
        r0 = ci * HG_CHUNK
        c_mid = c[r0 + HG_MID:r0 + HG_MID + 1, :]
        span = jnp.maximum(span, jnp.maximum(c[r0 + first:r0 + first + 1, :] - c_mid, c_mid - c[r0 + last:r0 + last + 1, :]))
    span = jnp.max(span)

    lane = lax.broadcasted_iota(jnp.int32, (HG_CHUNK, HG_W), 1)
    low_head = (lane & (HG_DK * 2 - 1)) < HG_DK
    ri = lax.broadcasted_iota(jnp.int32, (HG_CHUNK, 128), 0)
    si = lax.broadcasted_iota(jnp.int32, (HG_CHUNK, 128), 1) & (HG_DK - 1)
    visited = (si >= ri) if reverse else (si <= ri)
    r2 = lax.broadcasted_iota(jnp.int32, (128, 128), 0)
    l2 = lax.broadcasted_iota(jnp.int32, (128, 128), 1)
    same_head = (r2 < HG_DK) == (l2 < HG_DK)
    rowi = lax.broadcasted_iota(jnp.int32, (HG_CHUNK, 1), 0)

    def split_heads(t):
        top = jnp.where(low_head, t, 0.0).astype(BF16)
        bot = jnp.where(low_head, 0.0, t).astype(BF16)
        return [jnp.concatenate([top[:, 128 * g:128 * (g + 1)], bot[:, 128 * g:128 * (g + 1)]], axis=0) for g in range(npair)]

    def exact_intra(qq, kk, vv, cc):
        c_scr[...] = cc
        k_scr[...] = kk
        v_scr[...] = vv
        bd = bd_ref[...]

        def key_row(s, acc):
            c_s = c_scr[pl.ds(s, 1), :]
            w = qq * jnp.exp(jnp.minimum(cc - c_s, 0.0)) * k_scr[pl.ds(s, 1), :]
            ws = jnp.dot(w, bd, precision=HIGHEST, preferred_element_type=F32)
            ok = (rowi <= s) if reverse else (rowi >= s)
            return acc + jnp.where(ok, ws, 0.0) * v_scr[pl.ds(s, 1), :]

        return lax.fori_loop(0, HG_CHUNK, key_row, jnp.zeros((HG_CHUNK, HG_W), F32))

    def chunk(ci, exact):
        r0 = ci * HG_CHUNK
        rows = slice(r0, r0 + HG_CHUNK)
        cc, qq, kk, vv = c[rows], q[rows], k[rows], v[rows]
        c_mid = cc[HG_MID:HG_MID + 1, :]
        c_end = cc[last:last + 1, :]
        qs = (qq * jnp.exp(cc)).astype(BF16)
        ks = (kk * jnp.exp(c_end - cc)).astype(BF16)
        dec = jnp.exp(c_end)
        if exact:
            intra = exact_intra(qq, kk, vv, cc)
        else:
            qd = (qq * jnp.exp(cc - c_mid)).astype(BF16)
            kd_blocks = split_heads(kk * jnp.exp(c_mid - cc))
            v_blocks = split_heads(vv)
        pieces = []
        for g in range(npair):
            ls = slice(128 * g, 128 * (g + 1))
            st = st_ref[g]
            o_g = lax.dot_general(qs[:, ls], st.astype(BF16), NT_DIMS, preferred_element_type=F32)
            if not exact:
                a = lax.dot_general(qd[:, ls], kd_blocks[g], NT_DIMS, preferred_element_type=F32)
                a = jnp.where(visited, a, 0.0).astype(BF16)
                o_g = o_g + jnp.dot(a, v_blocks[g], preferred_element_type=F32)
            vt = vv[:, ls].T.astype(BF16)
            kv = jnp.dot(vt, ks[:, ls], preferred_element_type=F32)
            st_ref[g] = st * dec[:, ls] + jnp.where(same_head, kv, 0.0)
            pieces.append(o_g)
        o = jnp.concatenate(pieces, axis=-1)
        o_ref[0, 0, rows, :] = o + intra if exact else o

    order = range(nch - 1, -1, -1) if reverse else range(nch)

    @pl.when(span <= HG_MAX_LOG_RANGE)
    def _matmul_form():
        for ci in order:
            chunk(ci, False)

    @pl.when(span > HG_MAX_LOG_RANGE)
    def _exact_form():
        for ci in order:
            chunk(ci, True)


def _hgrn(p, f, tri, lbc, bd, nctx, reverse):
    b, lt, _ = p.shape
    nblk = lt // TB
    d = 1 if reverse else 0

    def blk(j):
        return _scan_block(j, d, nctx, nblk)

    return pl.pallas_call(
        functools.partial(_hgrn_kernel, reverse=reverse),
        grid=(b, nblk),
        in_specs=[pl.BlockSpec((1, TB, HG_W), lambda bi, j: (bi, blk(j), 0)),
                  pl.BlockSpec((1, TB, HG_W), lambda bi, j: (bi, blk(j), 1)),
                  pl.BlockSpec((1, TB, HG_W), lambda bi, j: (bi, blk(j), d)),
                  pl.BlockSpec((1, TB, TB), lambda bi, j: (d, 0, 0), pipeline_mode=pl.Buffered(1)),
                  pl.BlockSpec((1, 8, HG_W), lambda bi, j: (d, 0, 0), pipeline_mode=pl.Buffered(1)),
                  _const_spec((HG_W, HG_W))],
        out_specs=pl.BlockSpec((1, 1, TB, HG_W), lambda bi, j: (0, bi, blk(j), 0)),
        out_shape=jax.ShapeDtypeStruct((1, b, lt, HG_W), F32),
        scratch_shapes=[pltpu.VMEM((HG_W // 128, 128, 128), F32),
                        pltpu.VMEM((HG_CHUNK, HG_W), F32),
                        pltpu.VMEM((HG_CHUNK, HG_W), F32),
                        pltpu.VMEM((HG_CHUNK, HG_W), F32)],
        compiler_params=_cparams(("parallel", "arbitrary")),
        name="hgrn2_bwd" if reverse else "hgrn2_fwd",
    )(p, p, f, tri, lbc, bd)


def _rms_rows(t, width):
    return lax.rsqrt(jnp.sum(t * t, axis=-1, keepdims=True) * (1.0 / width) + EPS)


def _mla_prep_kernel(pm_ref, cos_ref, sin_ref, qn_ref, kvn_ref, gq_ref, gk_ref,
                     wq1_ref, wq2_ref, wk1_ref, sk1_ref, sk2_ref, wv_ref,
                     qt_ref, k_ref, vt_ref, kn_ref):
    pm = pm_ref[0]
    dq = pm[:, 0:MLA_Q_RANK].astype(F32)
    dkv = pm[:, MLA_Q_RANK:MLA_Q_RANK + MLA_KV_RANK].astype(F32)
    dqn = (dq * _rms_rows(dq, MLA_Q_RANK) * qn_ref[...]).astype(BF16)
    dkvn = (dkv * _rms_rows(dkv, MLA_KV_RANK) * kvn_ref[...]).astype(BF16)
    q1 = jnp.dot(dqn, wq1_ref[...], preferred_element_type=F32)
    q2 = jnp.dot(dqn, wq2_ref[...], preferred_element_type=F32)
    k1 = (jnp.dot(dkvn, wk1_ref[...], preferred_element_type=F32)
          + jnp.dot(pm, sk1_ref[...], preferred_element_type=F32))
    k2 = jnp.dot(pm, sk2_ref[...], preferred_element_type=F32)
    vv = jnp.dot(dkvn, wv_ref[...], preferred_element_type=F32)
    cos = cos_ref[...]
    sin = sin_ref[...]
    gq_c = gq_ref[0:1, :] * cos
    gq_s = gq_ref[1:2, :] * sin
    gk_c = gk_ref[0:1, :] * cos
    gk_s = gk_ref[1:2, :] * sin
    vrow = lax.broadcasted_iota(jnp.int32, (V_ROWS, TB), 0)
    kn = []
    for h in range(MLA_HEADS):
        sl = slice(HEAD_PAD * h, HEAD_PAD * (h + 1))
        qh = q1[:, sl]
        qo = (qh * gq_c + q2[:, sl] * gq_s) * (_rms_rows(qh, MLA_QK) * SCORE_SCALE)
        qt_ref[0, h] = qo.T.astype(BF16)
        kh = k1[:, sl]
        ko = (kh * gk_c + k2[:, sl] * gk_s) * _rms_rows(kh, MLA_QK)
        k_ref[0, h] = ko.astype(BF16)
        ksq = jnp.max(jnp.sum(ko * ko, axis=-1, keepdims=True), axis=0, keepdims=True)
        kn.append(jnp.broadcast_to(ksq, (1, HEAD_PAD)))
        vt = vv[:, sl].T[0:V_ROWS, :]
        vt_ref[0, h] = jnp.where(vrow == MLA_V, 1.0, vt).astype(BF16)
    kn_ref[0, 0] = jnp.concatenate(kn, axis=0)


def _mla_prep(p, cos_t, sin_t, qn, kvn, gq, gk, wq1, wq2, wk1, sk1, sk2, wv, nctx):
    b, lt, _ = p.shape
    hw = MLA_HEADS * HEAD_PAD
    nblk = lt // TB
    return pl.pallas_call(
        _mla_prep_kernel,
        grid=(b, lt // TB),
        in_specs=[pl.BlockSpec((1, TB, 512), lambda bi, i: (bi, i, 3)),
                  pl.BlockSpec((TB, HEAD_PAD), lambda bi, i: (i, 0)),
                  pl.BlockSpec((TB, HEAD_PAD), lambda bi, i: (i, 0)),
                  _const_spec((1, MLA_Q_RANK)), _const_spec((1, MLA_KV_RANK)),
                  _const_spec((2, HEAD_PAD)), _const_spec((2, HEAD_PAD)),
                  _const_spec((MLA_Q_RANK, hw)), _const_spec((MLA_Q_RANK, hw)),
                  _const_spec((MLA_KV_RANK, hw)), _const_spec((512, hw)), _const_spec((512, hw)),
                  _const_spec((MLA_KV_RANK, hw))],
        out_specs=[pl.BlockSpec((1, MLA_HEADS, HEAD_PAD, TB), lambda bi, i: (bi, 0, 0, (i + nblk - nctx) % nblk)),
                   pl.BlockSpec((1, MLA_HEADS, TB, HEAD_PAD), lambda bi, i: (bi, 0, i, 0)),
                   pl.BlockSpec((1, MLA_HEADS, V_ROWS, TB), lambda bi, i: (bi, 0, 0, i)),
                   pl.BlockSpec((1, 1, MLA_HEADS, HEAD_PAD), lambda bi, i: (bi, i, 0, 0))],
        out_shape=[jax.ShapeDtypeStruct((b, MLA_HEADS, HEAD_PAD, lt), BF16),
                   jax.ShapeDtypeStruct((b, MLA_HEADS, lt, HEAD_PAD), BF16),
                   jax.ShapeDtypeStruct((b, MLA_HEADS, V_ROWS, lt), BF16),
                   jax.ShapeDtypeStruct((b, lt // TB, MLA_HEADS, HEAD_PAD), F32)],
        compiler_params=_cparams(("parallel", "arbitrary")),
        name="mla_prep",
    )(p, cos_t, sin_t, qn, kvn, gq, gk, wq1, wq2, wk1, sk1, sk2, wv)


def _attend_bounded(qts, k_ref, vt_ref, s_scr, nkeys, tk):
    n = nkeys // tk

    def scores(kb, slot):
        for hh in range(2):
            s_scr[slot, hh, 0:tk, :] = jnp.dot(k_ref[0, hh, kb * tk:(kb + 1) * tk, :], qts[hh], preferred_element_type=F32)

    def accumulate(kb, slot, accs):
        out = []
        for hh in range(2):
            p = jnp.exp2(s_scr[slot, hh, 0:tk, :]).astype(BF16)
            pv = jnp.dot(vt_ref[0, hh, :, kb * tk:(kb + 1) * tk], p, preferred_element_type=F32)
            out.append(pv if accs is None else accs[hh] + pv)
        return out

    scores(0, 0)
    accs = None
    for kb in range(n):
        if kb + 1 < n:
            scores(kb + 1, (kb + 1) % 2)
        accs = accumulate(kb, kb % 2, accs)
    return accs


def _attend_online(qts, k_ref, vt_ref, nkeys, tk):
    def kv_step(kb, carry):
        r0 = pl.multiple_of(kb * tk, tk)
        out = []
        for hh in range(2):
            m, acc = carry[hh]
            s = jnp.dot(k_ref[0, hh, pl.ds(r0, tk), :], qts[hh], preferred_element_type=F32)
            m_new = jnp.maximum(m, jnp.max(s, axis=0, keepdims=True))
            p = jnp.exp2(s - m_new).astype(BF16)
            acc = jnp.exp2(m - m_new) * acc + jnp.dot(vt_ref[0, hh, :, pl.ds(r0, tk)], p, preferred_element_type=F32)
            out.append((m_new, acc))
        return tuple(out)

    tq = qts[0].shape[1]
    init = (jnp.full((1, tq), -jnp.inf, F32), jnp.zeros((V_ROWS, tq), F32))
    res = lax.fori_loop(0, nkeys // tk, kv_step, (init, init))
    return res[0][1], res[1][1]


def _flash_kernel(kmax_ref, qt_ref, k_ref, vt_ref, o_ref, s_scr, *, nkeys, tk):
    bi = pl.program_id(0)
    g = pl.program_id(1)

    qts = [qt_ref[0, hh] for hh in range(2)]
    worst = 0.0
    for hh in range(2):
        qt = qts[hh].astype(F32)
        qnorm = jnp.sqrt(jnp.max(jnp.sum(qt * qt, axis=0, keepdims=True)))
        worst = jnp.maximum(worst, qnorm * kmax_ref[bi, 2 * g + hh])

    def finish(accs):
        outs = [(acc[0:MLA_V, :] / acc[MLA_V:MLA_V + 1, :]).T for acc in accs]
        o_ref[0] = jnp.concatenate(outs, axis=-1).astype(BF16)

    @pl.when(worst <= MAX_UNSHIFTED_SCORE)
    def _():
        finish(_attend_bounded(qts, k_ref, vt_ref, s_scr, nkeys, tk))

    @pl.when(worst > MAX_UNSHIFTED_SCORE)
    def _():
        finish(_attend_online(qts, k_ref, vt_ref, nkeys, tk))


def _flash(kmax, qt, k, vt, nq, nkeys, tq, q_col0, name):
    b = k.shape[0]
    tk = next(t for t in (768, 512, 256) if nkeys % t == 0)
    cb0 = q_col0 // tq
    return pl.pallas_call(
        functools.partial(_flash_kernel, nkeys=nkeys, tk=tk),
        grid=(b, MLA_HEADS // 2, nq // tq),
        in_specs=[pl.BlockSpec(memory_space=pltpu.SMEM),
                  pl.BlockSpec((1, 2, HEAD_PAD, tq), lambda bi, g, i: (bi, g, 0, cb0 + i)),
                  pl.BlockSpec((1, 2, nkeys, HEAD_PAD), lambda bi, g, i: (bi, g, 0, 0)),
                  pl.BlockSpec((1, 2, V_ROWS, nkeys), lambda bi, g, i: (bi, g, 0, 0))],
        out_specs=pl.BlockSpec((1, tq, HEAD_PAD), lambda bi, g, i: (bi, i, g)),
        out_shape=jax.ShapeDtypeStruct((b, nq, MLA_W), BF16),
        scratch_shapes=[pltpu.VMEM((2, 2, tk, tq), F32)],
        compiler_params=_cparams(("parallel", "parallel", "arbitrary")),
        name=name,
    )(kmax, qt, k, vt)


def _lru_kernel(x_ref, xp_ref, xn_ref, cw_ref, cb_ref, wg_ref, bg_ref, lam_ref, o_ref, xe_scr, h_scr, *, nctx, nblk):
    d = pl.program_id(1)
    j = pl.program_id(2)
    blk = _scan_block(j, d, nctx, nblk)

    @pl.when(j == 0)
    def _():
        h_scr[...] = jnp.zeros_like(h_scr)

    keep_prev = jnp.logical_and(blk != 0, blk != nctx)
    keep_next = jnp.logical_and(blk != nctx - 1, blk != nblk - 1)
    xe_scr[0:HALO, :] = jnp.where(keep_prev, xp_ref[0].astype(F32), 0.0)
    xe_scr[HALO:HALO + TB, :] = x_ref[0].astype(F32)
    xe_scr[HALO + TB:2 * HALO + TB, :] = jnp.where(keep_next, xn_ref[0].astype(F32), 0.0)
    left = CONV_W // 2
    u = jnp.broadcast_to(cb_ref[...], (TB, LRU_W))
    for tap in range(CONV_W):
        u = u + xe_scr[pl.ds(HALO - left + tap, TB), :] * cw_ref[tap:tap + 1, :]

    g = jnp.dot(u.astype(BF16), wg_ref[0], preferred_element_type=F32) + bg_ref[0, 0:1, :]
    r = _sigmoid(g[:, 0:LRU_W])
    ig = _sigmoid(g[:, LRU_W:2 * LRU_W])
    neg_lam = -lam_ref[0, 0:1, :]
    softplus = jnp.maximum(neg_lam, 0.0) + jnp.log1p(jnp.exp(-jnp.abs(neg_lam)))
    a = jnp.exp(-LRU_C * r * softplus)
    bb = jnp.sqrt(1.0 - a * a) * (ig * u)
    rowi = lax.broadcasted_iota(jnp.int32, (TB, 1), 0)

    def scan(forward):
        aa, hh = a, bb
        k = 1
        while k < TB:
            shift = k if forward else TB - k
            valid = (rowi >= k) if forward else (rowi < TB - k)
            a_sh = jnp.where(valid, pltpu.roll(aa, shift, 0), 1.0)
            h_sh = jnp.where(valid, pltpu.roll(hh, shift, 0), 0.0)
            hh = aa * h_sh + hh
            aa = aa * a_sh
            k *= 2
        hh = hh + aa * h_scr[...]
        o_ref[0, 0] = hh
        last = TB - 1 if forward else 0
        h_scr[...] = hh[last:last + 1, :]

    @pl.when(d == 0)
    def _():
        scan(True)

    @pl.when(d == 1)
    def _():
        scan(False)


def _lru(p, conv_w, conv_b, wg, bg, lam, nctx):
    b, lt, _ = p.shape
    nblk = lt // TB
    hpb = TB // HALO
    nh = lt // HALO

    def blk(d, j):
        return _scan_block(j, d, nctx, nblk)

    return pl.pallas_call(
        functools.partial(_lru_kernel, nctx=nctx, nblk=nblk),
        grid=(b, 2, nblk),
        in_specs=[pl.BlockSpec((1, TB, LRU_W), lambda bi, d, j: (bi, blk(d, j), 4)),
                  pl.BlockSpec((1, HALO, LRU_W), lambda bi, d, j: (bi, jnp.maximum(blk(d, j) * hpb - 1, 0), 4)),
                  pl.BlockSpec((1, HALO, LRU_W), lambda bi, d, j: (bi, jnp.minimum((blk(d, j) + 1) * hpb, nh - 1), 4)),
                  _const_spec((8, LRU_W)), _const_spec((1, LRU_W)),
                  pl.BlockSpec((1, LRU_W, 2 * LRU_W), lambda bi, d, j: (d, 0, 0)),
                  pl.BlockSpec((1, 8, 2 * LRU_W), lambda bi, d, j: (d, 0, 0)),
                  pl.BlockSpec((1, 8, LRU_W), lambda bi, d, j: (d, 0, 0))],
        out_specs=pl.BlockSpec((1, 1, TB, LRU_W), lambda bi, d, j: (d, bi, blk(d, j), 0)),
        out_shape=jax.ShapeDtypeStruct((2, b, lt, LRU_W), F32),
        scratch_shapes=[pltpu.VMEM((TB + 2 * HALO, LRU_W), F32), pltpu.VMEM((1, LRU_W), F32)],
        compiler_params=_cparams(("parallel", "arbitrary", "arbitrary")),
        name="rglru",
    )(p, p, p, conv_w, conv_b, wg, bg, lam)


def _gelu_tanh(t):
    return 0.5 * t * (1.0 + jnp.tanh(0.7978845608028654 * (t + 0.044715 * t * t * t)))


def _merge_kernel(x_ref, mctx_ref, mb_ref, hgf_ref, hgb_ref, hgg_ref, hgn_ref, hm_ref, mlac_ref, mlax_ref,
                  lf_ref, lb_ref, ly_ref, g1_ref, g2_ref, g3_ref, wbr_ref, wout_ref, o_ref, *, ctx_len):
    i = pl.program_id(1)
    nctx = ctx_len // TB
    o_mla = jnp.where(i < nctx, mlac_ref[0], mlax_ref[0])
    o = hgf_ref[0, 0] + hgb_ref[0, 0]
    ms = jnp.dot((o * o).astype(BF16), hm_ref[...], preferred_element_type=F32)
    o_hg = o * lax.rsqrt(ms + EPS) * hgn_ref[...] * _sigmoid(hgg_ref[0].astype(F32))
    o_lru = (lf_ref[0, 0] + lb_ref[0, 0]) * _gelu_tanh(ly_ref[0].astype(F32))
    y = (_sigmoid(g1_ref[0].astype(F32)) * jnp.dot(o_hg.astype(BF16), wbr_ref[0], preferred_element_type=F32)
         + _sigmoid(g2_ref[0].astype(F32)) * jnp.dot(o_mla, wbr_ref[1], preferred_element_type=F32)
         + _sigmoid(g3_ref[0].astype(F32)) * jnp.dot(o_lru.astype(BF16), wbr_ref[2], preferred_element_type=F32))
    gate = _row_mods(i, TB, ctx_len, mctx_ref, mb_ref, 2)
    o_ref[0] = x_ref[0] + gate * jnp.dot(y.astype(BF16), wout_ref[...], preferred_element_type=F32)


def _merge(xs, mctx, mb, o_hg, p, hg_gain, head_mean, o_mla_c, o_mla_x, h_lru, w_br, w_out, ctx_len):
    b, lt, _ = xs.shape
    nctx = ctx_len // TB

    def pcol(width, c):
        return pl.BlockSpec((1, TB, width), lambda bi, i: (bi, i, c))

    def dirspec(d):
        return pl.BlockSpec((1, 1, TB, 512), lambda bi, i: (d, bi, i, 0))

    return pl.pallas_call(
        functools.partial(_merge_kernel, ctx_len=ctx_len),
        grid=(b, lt // TB),
        in_specs=[pl.BlockSpec((1, TB, D_MODEL), lambda bi, i: (bi, i, 0)),
                  _const_spec((6, D_MODEL)),
                  pl.BlockSpec((1, 6, D_MODEL), lambda bi, i: (bi, 0, 0)),
                  dirspec(0), dirspec(0), pcol(512, 2),
                  _const_spec((1, HG_W)), _const_spec((HG_W, HG_W)),
                  pl.BlockSpec((1, TB, MLA_W), lambda bi, i: (bi, jnp.minimum(i, nctx - 1), 0)),
                  pl.BlockSpec((1, TB, MLA_W), lambda bi, i: (bi, jnp.maximum(i - nctx, 0), 0)),
                  dirspec(0), dirspec(1), pcol(512, 5),
                  pcol(D_MODEL, 3), pcol(D_MODEL, 4), pcol(D_MODEL, 5),
                  _const_spec((3, 512, D_MODEL)), _const_spec((D_MODEL, D_MODEL))],
        out_specs=pl.BlockSpec((1, TB, D_MODEL), lambda bi, i: (bi, i, 0)),
        out_shape=jax.ShapeDtypeStruct((b, lt, D_MODEL), F32),
        compiler_params=_cparams(("parallel", "arbitrary")),
        name="merge",
    )(xs, mctx, mb, o_hg[0], o_hg[1], p, hg_gain, head_mean, o_mla_c, o_mla_x, h_lru, h_lru, p, p, p, p, w_br, w_out)


def _first_row_of_max(vals, row, valid):
    masked = jnp.where(valid, vals, -jnp.inf)
    m = jnp.max(masked, axis=0, keepdims=True)
    idx = jnp.min(jnp.where(masked == m, row, ROUTER_PAD), axis=0, keepdims=True)
    return m, idx


def _router(h, wr_hi, wr_lo, br):
    hi = h.astype(BF16)
    lo = (h - hi.astype(F32)).astype(BF16)
    logits = (jnp.dot(hi, wr_hi, preferred_element_type=F32) + jnp.dot(lo, wr_hi, preferred_element_type=F32)
              + jnp.dot(hi, wr_lo, preferred_element_type=F32))
    biased = (logits + br).T
    logits = logits.T
    row = lax.broadcasted_iota(jnp.int32, logits.shape, 0)
    is_group = jnp.logical_and(row >= N_EXPERTS, row < N_EXPERTS + N_GROUPS)
    _, g_row = _first_row_of_max(biased, row, is_group)
    g_max, _ = _first_row_of_max(logits, row, is_group)
    g_exp = jnp.where(is_group, jnp.exp(logits - g_max), 0.0)
    g_sel_logit = jnp.sum(jnp.where(row == g_row, logits, 0.0), axis=0, keepdims=True)
    p_g = jnp.exp(g_sel_logit - g_max) / jnp.sum(g_exp, axis=0, keepdims=True)
    in_group = jnp.right_shift(row, 2) == (g_row - N_EXPERTS)
    _, i1 = _first_row_of_max(biased, row, in_group)
    _, i2 = _first_row_of_max(biased, row, jnp.logical_and(in_group, row != i1))
    l1 = jnp.sum(jnp.where(row == i1, logits, 0.0), axis=0, keepdims=True)
    l2 = jnp.sum(jnp.where(row == i2, logits, 0.0), axis=0, keepdims=True)
    lm = jnp.maximum(l1, l2)
    e1 = jnp.exp(l1 - lm)
    e2 = jnp.exp(l2 - lm)
    inv = p_g / (e1 + e2)
    comb_t = jnp.where(row == i1, e1 * inv, 0.0) + jnp.where(row == i2, e2 * inv, 0.0)
    return comb_t.T


def _moe_kernel(x_ref, mctx_ref, mb_ref, gain_ref, wrh_ref, wrl_ref, br_ref, w13_ref, w2_ref, o_ref, *, ctx_len, tm):
    i = pl.program_id(1)
    epb = N_EXPERTS // EXPERT_STEPS
    hw = epb * D_EXPERT
    nrows = min(MOE_RB, tm)
    for rb in range(tm // nrows):
        rows = slice(rb * nrows, (rb + 1) * nrows)
        tile = i * (tm // nrows) + rb
        shift = _row_mods(tile, nrows, ctx_len, mctx_ref, mb_ref, 3)
        scale = _row_mods(tile, nrows, ctx_len, mctx_ref, mb_ref, 4)
        gate = _row_mods(tile, nrows, ctx_len, mctx_ref, mb_ref, 5)
        x = x_ref[0, rows, :]
        h = _norm_modulate(x, gain_ref[...], shift, scale)
        comb = _router(h, wrh_ref[...], wrl_ref[...], br_ref[...])
        hb = h.astype(BF16)
        acc = None
        for es in range(EXPERT_STEPS):
            h13 = jnp.dot(hb, w13_ref[es], preferred_element_type=F32)
            h1 = h13[:, 0:hw]
            hid = h1 * _sigmoid(h1) * h13[:, hw:2 * hw]
            parts = [hid[:, k * D_EXPERT:(k + 1) * D_EXPERT] * comb[:, es * epb + k:es * epb + k + 1] for k in range(epb)]
            y = jnp.dot(jnp.concatenate(parts, axis=-1).astype(BF16), w2_ref[es], preferred_element_type=F32)
            acc = y if acc is None else acc + y
        o_ref[0, rows, :] = x + gate * acc


def _moe(xs, mctx, mb, gain, wr_hi, wr_lo, br, w13, w2, ctx_len):
    b, lt, _ = xs.shape
    tm = MOE_TM if lt % MOE_TM == 0 else TB
    epb = N_EXPERTS // EXPERT_STEPS
    return pl.pallas_call(
        functools.partial(_moe_kernel, ctx_len=ctx_len, tm=tm),
        grid=(b, lt // tm),
        in_specs=[pl.BlockSpec((1, tm, D_MODEL), lambda bi, i: (bi, i, 0)),
                  _const_spec((6, D_MODEL)),
                  pl.BlockSpec((1, 6, D_MODEL), lambda bi, i: (bi, 0, 0)),
                  _const_spec((1, D_MODEL)),
                  _const_spec((D_MODEL, ROUTER_PAD)), _const_spec((D_MODEL, ROUTER_PAD)), _const_spec((1, ROUTER_PAD)),
                  _const_spec((EXPERT_STEPS, D_MODEL, 2 * epb * D_EXPERT)),
                  _const_spec((EXPERT_STEPS, epb * D_EXPERT, D_MODEL))],
        out_specs=pl.BlockSpec((1, tm, D_MODEL), lambda bi, i: (bi, i, 0)),
        out_shape=jax.ShapeDtypeStruct((b, lt, D_MODEL), F32),
        compiler_params=_cparams(("parallel", "arbitrary")),
        name="moe",
    )(xs, mctx, mb, gain, wr_hi, wr_lo, br, w13, w2)


def _rope_tables(ctx_len, seq):
    half = MLA_ROPE // 2
    rows = seq // GRID_W
    pos_row = jnp.repeat(jnp.arange(rows, dtype=F32), GRID_W)
    pos_col = jnp.tile(jnp.arange(GRID_W, dtype=F32), rows)
    inv = ROPE_BASE ** (-jnp.arange(0, half, 2, dtype=F32) / half)
    ang = jnp.concatenate([pos_row[:, None] * inv, pos_col[:, None] * inv], axis=-1)
    cos, sin = jnp.cos(ang), jnp.sin(ang)
    ones = jnp.ones((seq, MLA_NOPE), F32)
    zpad = jnp.zeros((seq, HEAD_PAD - MLA_QK), F32)
    cos_x = jnp.concatenate([ones, cos, cos, zpad], axis=-1)
    sin_x = jnp.concatenate([jnp.zeros((seq, MLA_NOPE), F32), -sin, sin, zpad], axis=-1)
    cos_c = jnp.concatenate([jnp.ones((ctx_len, MLA_QK), F32), jnp.zeros((ctx_len, HEAD_PAD - MLA_QK), F32)], axis=-1)
    sin_c = jnp.zeros((ctx_len, HEAD_PAD), F32)
    return jnp.concatenate([cos_c, cos_x], axis=0), jnp.concatenate([sin_c, sin_x], axis=0)


def _swap_rope_halves(t):
    half = MLA_ROPE // 2
    return jnp.concatenate([jnp.zeros_like(t[..., :MLA_NOPE]), t[..., MLA_NOPE + half:], t[..., MLA_NOPE:MLA_NOPE + half]],
                           axis=-1)


def _pad_heads(t):
    pad = [(0, 0)] * (t.ndim - 1) + [(0, HEAD_PAD - t.shape[-1])]
    t = jnp.pad(t, pad)
    return t.reshape(t.shape[:-2] + (t.shape[-2] * HEAD_PAD,))


def _mla_weights(w_uq, w_ukv, gq, gk):
    wq = w_uq.reshape(MLA_Q_RANK, MLA_HEADS, MLA_QK)
    wq1 = _pad_heads(wq).astype(BF16)
    wq2 = _pad_heads(_swap_rope_halves(wq)).astype(BF16)
    wkv = w_ukv.reshape(MLA_KV_RANK, MLA_HEADS, MLA_NOPE + MLA_V)
    wk1 = _pad_heads(wkv[..., :MLA_NOPE]).astype(BF16)
    kr0 = MLA_Q_RANK + MLA_KV_RANK
    eye = jnp.eye(MLA_ROPE, dtype=F32)
    place = jnp.concatenate([jnp.zeros((MLA_ROPE, MLA_NOPE), F32), eye], axis=-1)
    place = jnp.broadcast_to(place[:, None, :], (MLA_ROPE, MLA_HEADS, MLA_QK))
    zrow_a = jnp.zeros((kr0, MLA_HEADS * HEAD_PAD), F32)
    zrow_b = jnp.zeros((512 - kr0 - MLA_ROPE, MLA_HEADS * HEAD_PAD), F32)
    sk1 = jnp.concatenate([zrow_a, _pad_heads(place), zrow_b], axis=0).astype(BF16)
    sk2 = jnp.concatenate([zrow_a, _pad_heads(_swap_rope_halves(place)), zrow_b], axis=0).astype(BF16)
    wv = _pad_heads(wkv[..., MLA_NOPE:]).astype(BF16)

    def gains(g):
        g1 = jnp.pad(g, (0, HEAD_PAD - MLA_QK))
        g2 = jnp.pad(_swap_rope_halves(g), (0, HEAD_PAD - MLA_QK))
        return jnp.stack([g1, g2], axis=0)

    return wq1, wq2, wk1, sk1, sk2, wv, gains(gq), gains(gk)


def _block_diag(w):
    n, a, bb = w.shape
    eye = jnp.eye(n, dtype=w.dtype)
    return (eye[:, None, :, None] * w[:, :, None, :]).reshape(n * a, n * bb)


def _pad_rows(t, rows=8):
    return jnp.pad(t, ((0, rows - t.shape[0]), (0, 0)))


def kernel(x, c, ctx, c_ctx, w_mod, b_mod, norm_mix, norm_ffn, w_in, hg_lb, hg_norm, mla_q_norm, mla_kv_norm, mla_w_uq, mla_w_ukv, mla_qk_gain_q, mla_qk_gain_k, lru_conv_w, lru_conv_b, lru_wa, lru_ba, lru_wx, lru_bx, lru_lambda, w_br_hg, w_br_mla, w_br_lru, w_out, moe_w_rg, moe_b_rg, moe_w_re, moe_b_re, moe_w1, moe_w3, moe_w2):
    bsz, seq, _ = x.shape
    ctx_len = ctx.shape[1]
    depth = w_in.shape[0]
    assert seq % TB == 0 and ctx_len % TB == 0 and seq % GRID_W == 0 and bsz < 8
    nctx = ctx_len // TB

    xs = jnp.concatenate([ctx, x], axis=1)
    cos_t, sin_t = _rope_tables(ctx_len, seq)
    cvec = jnp.zeros((8, D_MODEL), F32).at[:bsz].set(c).at[bsz].set(c_ctx)

    lb_cs = jnp.cumsum(jax.nn.softmax(hg_lb.astype(F32), axis=0), axis=0)
    lb_all = lb_cs - lb_cs[0:1]
    tri_lo = _block_diag(jnp.broadcast_to(jnp.tril(jnp.ones((HG_CHUNK, HG_CHUNK), F32)), (TB // HG_CHUNK, HG_CHUNK, HG_CHUNK)))
    tri = jnp.stack([tri_lo, tri_lo.T], axis=0).astype(BF16)
    head_ones = _block_diag(jnp.ones((HG_HEADS, HG_DK, HG_DK), F32))
    head_mean = (head_ones / HG_DK).astype(BF16)

    offs = np.cumsum((HG_W,) * 5 + (MLA_Q_RANK, MLA_KV_RANK, MLA_ROPE, LRU_W, LRU_W) + (D_MODEL,) * 3)[:-1].tolist()

    for l in range(depth):
        mods = _modulation(cvec, w_mod[l], b_mod[l]).reshape(8, 6, D_MODEL)
        mctx, mb = mods[bsz], mods[:bsz]

        q_, ff, fb, i_, g_, dq, dkv, kr, lx, ly, g1, g2, g3 = jnp.split(w_in[l], offs, axis=-1)
        kr_pad = jnp.zeros((D_MODEL, 512 - MLA_Q_RANK - MLA_KV_RANK - MLA_ROPE), F32)
        w_all = jnp.concatenate([ff, fb, q_, i_, g_, dq, dkv, kr, kr_pad, lx, ly, g1, g2, g3], axis=-1).astype(BF16)
        f, p = _inproj(xs, mctx, mb, norm_mix[l][None, :], w_all, ctx_len)

        lb = lb_all[l]
        lbc = jnp.stack([jnp.log(lb), jnp.log1p(-lb), 1.0 - lb] + [jnp.zeros_like(lb)] * 5, axis=1)
        o_hg = [_hgrn(p, f, tri, lbc, head_ones, nctx, reverse) for reverse in (False, True)]

        wq1, wq2, wk1, sk1, sk2, wv, gq, gk = _mla_weights(mla_w_uq[l], mla_w_ukv[l], mla_qk_gain_q[l], mla_qk_gain_k[l])
        qt, kh, vt, ksq = _mla_prep(p, cos_t, sin_t, mla_q_norm[l][None, :], mla_kv_norm[l][None, :], gq, gk,
                                    wq1, wq2, wk1, sk1, sk2, wv, nctx)
        kmax = jnp.sqrt(jnp.max(ksq, axis=(1, 3)))
        tq = next(t for t in (512, 256) if seq % t == 0)
        o_mla_x = _flash(kmax, qt, kh, vt, seq, ctx_len + seq, tq, 0, "mla_attention")
        if l < depth - 1:
            o_mla_c = _flash(kmax, qt, kh, vt, ctx_len, ctx_len, TB, seq, "mla_attention_ctx")
        else:
            o_mla_c = o_mla_x[:, :ctx_len]

        wg = jnp.stack([jnp.concatenate([_block_diag(lru_wa[l, d]), _block_diag(lru_wx[l, d])], axis=-1)
                        for d in range(2)], axis=0).astype(BF16)
        bg = jnp.concatenate([lru_ba[l], lru_bx[l]], axis=-1)[:, None, :] * jnp.ones((1, 8, 1), F32)
        lam = lru_lambda[l][:, None, :] * jnp.ones((1, 8, 1), F32)
        h_lru = _lru(p, _pad_rows(lru_conv_w[l]), lru_conv_b[l][None, :], wg, bg, lam, nctx)

        w_br = jnp.stack([w_br_hg[l], w_br_mla[l], w_br_lru[l]], axis=0).astype(BF16)
        hg_gain = jnp.tile(hg_norm[l], HG_HEADS)[None, :]
        xs = _merge(xs, mctx, mb, o_hg, p, hg_gain, head_mean, o_mla_c, o_mla_x, h_lru, w_br, w_out[l].astype(BF16), ctx_len)

        wr = jnp.pad(jnp.concatenate([moe_w_re[l], moe_w_rg[l]], axis=-1), ((0, 0), (0, ROUTER_PAD - N_EXPERTS - N_GROUPS)))
        br = jnp.pad(jnp.concatenate([moe_b_re[l], moe_b_rg[l]]), (0, ROUTER_PAD - N_EXPERTS - N_GROUPS))[None, :]
        epb = N_EXPERTS // EXPERT_STEPS
        w1 = moe_w1[l].reshape(EXPERT_STEPS, epb, D_MODEL, D_EXPERT).transpose(0, 2, 1, 3).reshape(EXPERT_STEPS, D_MODEL, epb * D_EXPERT)
        w3 = moe_w3[l].reshape(EXPERT_STEPS, epb, D_MODEL, D_EXPERT).transpose(0, 2, 1, 3).reshape(EXPERT_STEPS, D_MODEL, epb * D_EXPERT)
        w13 = jnp.concatenate([w1, w3], axis=-1).astype(BF16)
        w2 = moe_w2[l].reshape(EXPERT_STEPS, epb * D_EXPERT, D_MODEL).astype(BF16)
        wr_hi = wr.astype(BF16)
        wr_lo = (wr - wr_hi.astype(F32)).astype(BF16)
        xs = _moe(xs, mctx, mb, norm_ffn[l][None, :], wr_hi, wr_lo, br, w13, w2, ctx_len)

    return xs[:, ctx_len:, :]
```

```python
import functools

import numpy as np
import jax
import jax.numpy as jnp
from jax import lax
from jax.experimental import pallas as pl
from jax.experimental.pallas import tpu as pltpu

F32 = jnp.float32
BF16 = jnp.bfloat16
HIGHEST = lax.Precision.HIGHEST

D_MODEL = 1024
GRID_W = 64
EPS = 1e-6

HG_HEADS = 8
HG_DK = 64
HG_W = 512
HG_CHUNK = 64
HG_MID = HG_CHUNK // 2
HG_MAX_LOG_RANGE = 80.0

MLA_HEADS = 8
MLA_Q_RANK = 256
MLA_KV_RANK = 128
MLA_NOPE = 64
MLA_ROPE = 32
MLA_V = 64
MLA_QK = MLA_NOPE + MLA_ROPE
MLA_W = MLA_HEADS * MLA_V
HEAD_PAD = 128
V_ROWS = 80
SCORE_SCALE = MLA_QK ** -0.5 * 1.4426950408889634
MAX_UNSHIFTED_SCORE = 57.0
ROPE_BASE = 10000.0

LRU_W = 512
LRU_BLOCKS = 8
LRU_BD = LRU_W // LRU_BLOCKS
CONV_W = 4
LRU_C = 8.0

N_GROUPS = 4
EXP_PER_GROUP = 4
N_EXPERTS = N_GROUPS * EXP_PER_GROUP
D_EXPERT = 256
EXPERT_STEPS = 4
ROUTER_PAD = 128
MOE_TM = 768
MOE_RB = 256

TB = 256
HALO = 16
P_WIDTH = 6144
F_WIDTH = 1024
VMEM_LIMIT = 56 * 1024 * 1024

NT_DIMS = (((1,), (1,)), ((), ()))


def _cparams(sem):
    return pltpu.CompilerParams(dimension_semantics=sem, vmem_limit_bytes=VMEM_LIMIT)


def _const_spec(shape):
    nd = len(shape)
    return pl.BlockSpec(shape, lambda *_: (0,) * nd, pipeline_mode=pl.Buffered(1))


def _sigmoid(t):
    return 0.5 * jnp.tanh(0.5 * t) + 0.5


def _scan_block(j, d, nctx, nblk):
    fwd = j
    bwd = jnp.where(j < nctx, nctx - 1 - j, nblk - 1 - (j - nctx))
    return jnp.where(d == 0, fwd, bwd)


def _mod_kernel(c_ref, w_ref, b_ref, o_ref):
    c = c_ref[...]
    s = c * _sigmoid(c)
    o_ref[...] = jnp.dot(s, w_ref[...], precision=HIGHEST, preferred_element_type=F32) + b_ref[...]


def _modulation(cvec, w_mod, b_mod):
    n = w_mod.shape[1]
    tn = 1024
    return pl.pallas_call(
        _mod_kernel,
        grid=(n // tn,),
        in_specs=[pl.BlockSpec((8, D_MODEL), lambda j: (0, 0)),
                  pl.BlockSpec((D_MODEL, tn), lambda j: (0, j)),
                  pl.BlockSpec((1, tn), lambda j: (0, j))],
        out_specs=pl.BlockSpec((8, tn), lambda j: (0, j)),
        out_shape=jax.ShapeDtypeStruct((8, n), F32),
        compiler_params=_cparams(("arbitrary",)),
        name="modulation",
    )(cvec, w_mod, b_mod.reshape(1, n))


def _row_mods(i, tm, ctx_len, mctx_ref, mb_ref, k):
    row = i * tm + lax.broadcasted_iota(jnp.int32, (tm, 1), 0)
    return jnp.where(row < ctx_len, mctx_ref[k:k + 1, :], mb_ref[0, k:k + 1, :])


def _norm_modulate(x, gain, shift, scale):
    ms = jnp.mean(x * x, axis=-1, keepdims=True)
    xn = x * lax.rsqrt(ms + EPS) * gain
    return xn * (1.0 + scale) + shift


def _inproj_kernel(x_ref, mctx_ref, mb_ref, gain_ref, w_ref, f_ref, p_ref, *, ctx_len):
    i = pl.program_id(1)
    shift = _row_mods(i, TB, ctx_len, mctx_ref, mb_ref, 0)
    scale = _row_mods(i, TB, ctx_len, mctx_ref, mb_ref, 1)
    h = _norm_modulate(x_ref[0], gain_ref[...], shift, scale).astype(BF16)
    f_ref[0] = jnp.dot(h, w_ref[:, 0:F_WIDTH], preferred_element_type=F32)
    cw = 512
    for j in range(P_WIDTH // cw):
        lo = F_WIDTH + j * cw
        p_ref[0, :, j * cw:(j + 1) * cw] = jnp.dot(
            h, w_ref[:, lo:lo + cw], preferred_element_type=F32).astype(BF16)


def _inproj(xs, mctx, mb, gain, w_all, ctx_len):
    b, lt, _ = xs.shape
    return pl.pallas_call(
        functools.partial(_inproj_kernel, ctx_len=ctx_len),
        grid=(b, lt // TB),
        in_specs=[pl.BlockSpec((1, TB, D_MODEL), lambda bi, i: (bi, i, 0)),
                  _const_spec((6, D_MODEL)),
                  pl.BlockSpec((1, 6, D_MODEL), lambda bi, i: (bi, 0, 0)),
                  _const_spec((1, D_MODEL)),
                  _const_spec((D_MODEL, F_WIDTH + P_WIDTH))],
        out_specs=[pl.BlockSpec((1, TB, F_WIDTH), lambda bi, i: (bi, i, 0)),
                   pl.BlockSpec((1, TB, P_WIDTH), lambda bi, i: (bi, i, 0))],
        out_shape=[jax.ShapeDtypeStruct((b, lt, F_WIDTH), F32),
                   jax.ShapeDtypeStruct((b, lt, P_WIDTH), BF16)],
        compiler_params=_cparams(("parallel", "arbitrary")),
        name="inproj",
    )(xs, mctx, mb, gain, w_all)


def _hgrn_kernel(q_ref, v_ref, f_ref, tri_ref, lbc_ref, bd_ref, o_ref, st_ref, c_scr, k_scr, v_scr, *, reverse):
    j = pl.program_id(1)
    nch = TB // HG_CHUNK
    npair = HG_W // 128
    first, last = (HG_CHUNK - 1, 0) if reverse else (0, HG_CHUNK - 1)

    @pl.when(j == 0)
    def _():
        st_ref[...] = jnp.zeros_like(st_ref)

    log_lb = lbc_ref[0, 0:1, :]
    log_1mlb = lbc_ref[0, 1:2, :]
    one_mlb = lbc_ref[0, 2:3, :]
    z = f_ref[0]
    q = q_ref[0].astype(F32)
    v = v_ref[0].astype(F32)
    e = jnp.exp(-jnp.abs(z))
    log_sig = jnp.minimum(z, 0.0) - jnp.log(1.0 + e)
    t = log_1mlb + log_sig
    logf = jnp.maximum(log_lb, t) + jnp.log(1.0 + jnp.exp(-jnp.abs(log_lb - t)))
    k = one_mlb * (jnp.where(z >= 0, e, 1.0) / (1.0 + e))
    hi = logf.astype(BF16)
    lo = (logf - hi.astype(F32)).astype(BF16)
    tri = tri_ref[0]
    c = jnp.dot(tri, hi, preferred_element_type=F32) + jnp.dot(tri, lo, preferred_element_type=F32)

    span = jnp.zeros((1, HG_W), F32)
    for ci in range(nch):
        r0 = ci * HG_CHUNK
        c_mid = c[r0 + HG_MID:r0 + HG_MID + 1, :]
        span = jnp.maximum(span, jnp.maximum(c[r0 + first:r0 + first + 1, :] - c_mid, c_mid - c[r0 + last:r0 + last + 1, :]))
    span = jnp.max(span)

    lane = lax.broadcasted_iota(jnp.int32, (HG_CHUNK, HG_W), 1)
    low_head = (lane & (HG_DK * 2 - 1)) < HG_DK
    ri = lax.broadcasted_iota(jnp.int32, (HG_CHUNK, 128), 0)
    si = lax.broadcasted_iota(jnp.int32, (HG_CHUNK, 128), 1) & (HG_DK - 1)
    visited = (si >= ri) if reverse else (si <= ri)
    r2 = lax.broadcasted_iota(jnp.int32, (128, 128), 0)
    l2 = lax.broadcasted_iota(jnp.int32, (128, 128), 1)
    same_head = (r2 < HG_DK) == (l2 < HG_DK)
    rowi = lax.broadcasted_iota(jnp.int32, (HG_CHUNK, 1), 0)

    def split_heads(t):
        top = jnp.where(low_head, t, 0.0).astype(BF16)
        bot = jnp.where(low_head, 0.0, t).astype(BF16)
        return [jnp.concatenate([top[:, 128 * g:128 * (g + 1)], bot[:, 128 * g:128 * (g + 1)]], axis=0) for g in range(npair)]

    def exact_intra(qq, kk, vv, cc):
        c_scr[...] = cc
        k_scr[...] = kk
        v_scr[...] = vv
        bd = bd_ref[...]

        def key_row(s, acc):
            c_s = c_scr[pl.ds(s, 1), :]
            w = qq * jnp.exp(jnp.minimum(cc - c_s, 0.0)) * k_scr[pl.ds(s, 1), :]
            ws = jnp.dot(w, bd, precision=HIGHEST, preferred_element_type=F32)
            ok = (rowi <= s) if reverse else (rowi >= s)
            return acc + jnp.where(ok, ws, 0.0) * v_scr[pl.ds(s, 1), :]

        return lax.fori_loop(0, HG_CHUNK, key_row, jnp.zeros((HG_CHUNK, HG_W), F32))

    def chunk(ci, exact):
        r0 = ci * HG_CHUNK
        rows = slice(r0, r0 + HG_CHUNK)
        cc, qq, kk, vv = c[rows], q[rows], k[rows], v[rows]
        c_mid = cc[HG_MID:HG_MID + 1, :]
        c_end = cc[last:last + 1, :]
        qs = (qq * jnp.exp(cc)).astype(BF16)
        ks = (kk * jnp.exp(c_end - cc)).astype(BF16)
        dec = jnp.exp(c_end)
        if exact:
            intra = exact_intra(qq, kk, vv, cc)
        else:
            qd = (qq * jnp.exp(cc - c_mid)).astype(BF16)
            kd_blocks = split_heads(kk * jnp.exp(c_mid - cc))
            v_blocks = split_heads(vv)
        pieces = []
        for g in range(npair):
            ls = slice(128 * g, 128 * (g + 1))
            st = st_ref[g]
            o_g = lax.dot_general(qs[:, ls], st.astype(BF16), NT_DIMS, preferred_element_type=F32)
            if not exact:
                a = lax.dot_general(qd[:, ls], kd_blocks[g], NT_DIMS, preferred_element_type=F32)
                a = jnp.where(visited, a, 0.0).astype(BF16)
                o_g = o_g + jnp.dot(a, v_blocks[g], preferred_element_type=F32)
            vt = vv[:, ls].T.astype(BF16)
            kv = jnp.dot(vt, ks[:, ls], preferred_element_type=F32)
            st_ref[g] = st * dec[:, ls] + jnp.where(same_head, kv, 0.0)
            pieces.append(o_g)
        o = jnp.concatenate(pieces, axis=-1)
        o_ref[0, 0, rows, :] = o + intra if exact else o

    order = range(nch - 1, -1, -1) if reverse else range(nch)

    @pl.when(span <= HG_MAX_LOG_RANGE)
    def _matmul_form():
        for ci in order:
            chunk(ci, False)

    @pl.when(span > HG_MAX_LOG_RANGE)
    def _exact_form():
        for ci in order:
            chunk(ci, True)


def _hgrn(p, f, tri, lbc, bd, nctx, reverse):
    b, lt, _ = p.shape
    nblk = lt // TB
    d = 1 if reverse else 0

    def blk(j):
        return _scan_block(j, d, nctx, nblk)

    return pl.pallas_call(
        functools.partial(_hgrn_kernel, reverse=reverse),
        grid=(b, nblk),
        in_specs=[pl.BlockSpec((1, TB, HG_W), lambda bi, j: (bi, blk(j), 0)),
                  pl.BlockSpec((1, TB, HG_W), lambda bi, j: (bi, blk(j), 1)),
                  pl.BlockSpec((1, TB, HG_W), lambda bi, j: (bi, blk(j), d)),
                  pl.BlockSpec((1, TB, TB), lambda bi, j: (d, 0, 0), pipeline_mode=pl.Buffered(1)),
                  pl.BlockSpec((1, 8, HG_W), lambda bi, j: (d, 0, 0), pipeline_mode=pl.Buffered(1)),
                  _const_spec((HG_W, HG_W))],
        out_specs=pl.BlockSpec((1, 1, TB, HG_W), lambda bi, j: (0, bi, blk(j), 0)),
        out_shape=jax.ShapeDtypeStruct((1, b, lt, HG_W), F32),
        scratch_shapes=[pltpu.VMEM((HG_W // 128, 128, 128), F32),
                        pltpu.VMEM((HG_CHUNK, HG_W), F32),
                        pltpu.VMEM((HG_CHUNK, HG_W), F32),
                        pltpu.VMEM((HG_CHUNK, HG_W), F32)],
        compiler_params=_cparams(("parallel", "arbitrary")),
        name="hgrn2_bwd" if reverse else "hgrn2_fwd",
    )(p, p, f, tri, lbc, bd)


def _rms_rows(t, width):
    return lax.rsqrt(jnp.sum(t * t, axis=-1, keepdims=True) * (1.0 / width) + EPS)


def _mla_prep_kernel(pm_ref, cos_ref, sin_ref, qn_ref, kvn_ref, gq_ref, gk_ref,
                     wq1_ref, wq2_ref, wk1_ref, sk1_ref, sk2_ref, wv_ref,
                     qt_ref, k_ref, vt_ref, kn_ref):
    pm = pm_ref[0]
    dq = pm[:, 0:MLA_Q_RANK].astype(F32)
    dkv = pm[:, MLA_Q_RANK:MLA_Q_RANK + MLA_KV_RANK].astype(F32)
    dqn = (dq * _rms_rows(dq, MLA_Q_RANK) * qn_ref[...]).astype(BF16)
    dkvn = (dkv * _rms_rows(dkv, MLA_KV_RANK) * kvn_ref[...]).astype(BF16)
    q1 = jnp.dot(dqn, wq1_ref[...], preferred_element_type=F32)
    q2 = jnp.dot(dqn, wq2_ref[...], preferred_element_type=F32)
    k1 = (jnp.dot(dkvn, wk1_ref[...], preferred_element_type=F32)
          + jnp.dot(pm, sk1_ref[...], preferred_element_type=F32))
    k2 = jnp.dot(pm, sk2_ref[...], preferred_element_type=F32)
    vv = jnp.dot(dkvn, wv_ref[...], preferred_element_type=F32)
    cos = cos_ref[...]
    sin = sin_ref[...]
    gq_c = gq_ref[0:1, :] * cos
    gq_s = gq_ref[1:2, :] * sin
    gk_c = gk_ref[0:1, :] * cos
    gk_s = gk_ref[1:2, :] * sin
    vrow = lax.broadcasted_iota(jnp.int32, (V_ROWS, TB), 0)
    kn = []
    for h in range(MLA_HEADS):
        sl = slice(HEAD_PAD * h, HEAD_PAD * (h + 1))
        qh = q1[:, sl]
        qo = (qh * gq_c + q2[:, sl] * gq_s) * (_rms_rows(qh, MLA_QK) * SCORE_SCALE)
        qt_ref[0, h] = qo.T.astype(BF16)
        kh = k1[:, sl]
        ko = (kh * gk_c + k2[:, sl] * gk_s) * _rms_rows(kh, MLA_QK)
        k_ref[0, h] = ko.astype(BF16)
        ksq = jnp.max(jnp.sum(ko * ko, axis=-1, keepdims=True), axis=0, keepdims=True)
        kn.append(jnp.broadcast_to(ksq, (1, HEAD_PAD)))
        vt = vv[:, sl].T[0:V_ROWS, :]
        vt_ref[0, h] = jnp.where(vrow == MLA_V, 1.0, vt).astype(BF16)
    kn_ref[0, 0] = jnp.concatenate(kn, axis=0)


def _mla_prep(p, cos_t, sin_t, qn, kvn, gq, gk, wq1, wq2, wk1, sk1, sk2, wv, nctx):
    b, lt, _ = p.shape
    hw = MLA_HEADS * HEAD_PAD
    nblk = lt // TB
    return pl.pallas_call(
        _mla_prep_kernel,
        grid=(b, lt // TB),
        in_specs=[pl.BlockSpec((1, TB, 512), lambda bi, i: (bi, i, 3)),
                  pl.BlockSpec((TB, HEAD_PAD), lambda bi, i: (i, 0)),
                  pl.BlockSpec((TB, HEAD_PAD), lambda bi, i: (i, 0)),
                  _const_spec((1, MLA_Q_RANK)), _const_spec((1, MLA_KV_RANK)),
                  _const_spec((2, HEAD_PAD)), _const_spec((2, HEAD_PAD)),
                  _const_spec((MLA_Q_RANK, hw)), _const_spec((MLA_Q_RANK, hw)),
                  _const_spec((MLA_KV_RANK, hw)), _const_spec((512, hw)), _const_spec((512, hw)),
                  _const_spec((MLA_KV_RANK, hw))],
        out_specs=[pl.BlockSpec((1, MLA_HEADS, HEAD_PAD, TB), lambda bi, i: (bi, 0, 0, (i + nblk - nctx) % nblk)),
                   pl.BlockSpec((1, MLA_HEADS, TB, HEAD_PAD), lambda bi, i: (bi, 0, i, 0)),
                   pl.BlockSpec((1, MLA_HEADS, V_ROWS, TB), lambda bi, i: (bi, 0, 0, i)),
                   pl.BlockSpec((1, 1, MLA_HEADS, HEAD_PAD), lambda bi, i: (bi, i, 0, 0))],
        out_shape=[jax.ShapeDtypeStruct((b, MLA_HEADS, HEAD_PAD, lt), BF16),
                   jax.ShapeDtypeStruct((b, MLA_HEADS, lt, HEAD_PAD), BF16),
                   jax.ShapeDtypeStruct((b, MLA_HEADS, V_ROWS, lt), BF16),
                   jax.ShapeDtypeStruct((b, lt // TB, MLA_HEADS, HEAD_PAD), F32)],
        compiler_params=_cparams(("parallel", "arbitrary")),
        name="mla_prep",
    )(p, cos_t, sin_t, qn, kvn, gq, gk, wq1, wq2, wk1, sk1, sk2, wv)


def _attend_bounded(qts, k_ref, vt_ref, s_scr, nkeys, tk):
    n = nkeys // tk

    def scores(kb, slot):
        for hh in range(2):
            s_scr[slot, hh, 0:tk, :] = jnp.dot(k_ref[0, hh, kb * tk:(kb + 1) * tk, :], qts[hh], preferred_element_type=F32)

    def accumulate(kb, slot, accs):
        out = []
        for hh in range(2):
            p = jnp.exp2(s_scr[slot, hh, 0:tk, :]).astype(BF16)
            pv = jnp.dot(vt_ref[0, hh, :, kb * tk:(kb + 1) * tk], p, preferred_element_type=F32)
            out.append(pv if accs is None else accs[hh] + pv)
        return out

    scores(0, 0)
    accs = None
    for kb in range(n):
        if kb + 1 < n:
            scores(kb + 1, (kb + 1) % 2)
        accs = accumulate(kb, kb % 2, accs)
    return accs


def _attend_online(qts, k_ref, vt_ref, nkeys, tk):
    def kv_step(kb, carry):
        r0 = pl.multiple_of(kb * tk, tk)
        out = []
        for hh in range(2):
            m, acc = carry[hh]
            s = jnp.dot(k_ref[0, hh, pl.ds(r0, tk), :], qts[hh], preferred_element_type=F32)
            m_new = jnp.maximum(m, jnp.max(s, axis=0, keepdims=True))
            p = jnp.exp2(s - m_new).astype(BF16)
            acc = jnp.exp2(m - m_new) * acc + jnp.dot(vt_ref[0, hh, :, pl.ds(r0, tk)], p, preferred_element_type=F32)
            out.append((m_new, acc))
        return tuple(out)

    tq = qts[0].shape[1]
    init = (jnp.full((1, tq), -jnp.inf, F32), jnp.zeros((V_ROWS, tq), F32))
    res = lax.fori_loop(0, nkeys // tk, kv_step, (init, init))
    return res[0][1], res[1][1]


def _flash_kernel(kmax_ref, qt_ref, k_ref, vt_ref, o_ref, s_scr, *, nkeys, tk):
    bi = pl.program_id(0)
    g = pl.program_id(1)

    qts = [qt_ref[0, hh] for hh in range(2)]
    worst = 0.0
    for hh in range(2):
        qt = qts[hh].astype(F32)
        qnorm = jnp.sqrt(jnp.max(jnp.sum(qt * qt, axis=0, keepdims=True)))
        worst = jnp.maximum(worst, qnorm * kmax_ref[bi, 2 * g + hh])

    def finish(accs):
        outs = [(acc[0:MLA_V, :] / acc[MLA_V:MLA_V + 1, :]).T for acc in accs]
        o_ref[0] = jnp.concatenate(outs, axis=-1).astype(BF16)

    @pl.when(worst <= MAX_UNSHIFTED_SCORE)
    def _():
        finish(_attend_bounded(qts, k_ref, vt_ref, s_scr, nkeys, tk))

    @pl.when(worst > MAX_UNSHIFTED_SCORE)
    def _():
        finish(_attend_online(qts, k_ref, vt_ref, nkeys, tk))


def _flash(kmax, qt, k, vt, nq, nkeys, tq, q_col0, name):
    b = k.shape[0]
    tk = next(t for t in (768, 512, 256) if nkeys % t == 0)
    cb0 = q_col0 // tq
    return pl.pallas_call(
        functools.partial(_flash_kernel, nkeys=nkeys, tk=tk),
        grid=(b, MLA_HEADS // 2, nq // tq),
        in_specs=[pl.BlockSpec(memory_space=pltpu.SMEM),
                  pl.BlockSpec((1, 2, HEAD_PAD, tq), lambda bi, g, i: (bi, g, 0, cb0 + i)),
                  pl.BlockSpec((1, 2, nkeys, HEAD_PAD), lambda bi, g, i: (bi, g, 0, 0)),
                  pl.BlockSpec((1, 2, V_ROWS, nkeys), lambda bi, g, i: (bi, g, 0, 0))],
        out_specs=pl.BlockSpec((1, tq, HEAD_PAD), lambda bi, g, i: (bi, i, g)),
        out_shape=jax.ShapeDtypeStruct((b, nq, MLA_W), BF16),
        scratch_shapes=[pltpu.VMEM((2, 2, tk, tq), F32)],
        compiler_params=_cparams(("parallel", "parallel", "arbitrary")),
        name=name,
    )(kmax, qt, k, vt)


def _lru_kernel(x_ref, xp_ref, xn_ref, cw_ref, cb_ref, wg_ref, bg_ref, lam_ref, o_ref, xe_scr, h_scr, *, nctx, nblk):
    d = pl.program_id(1)
    j = pl.program_id(2)
    blk = _scan_block(j, d, nctx, nblk)

    @pl.when(j == 0)
    def _():
        h_scr[...] = jnp.zeros_like(h_scr)

    keep_prev = jnp.logical_and(blk != 0, blk != nctx)
    keep_next = jnp.logical_and(blk != nctx - 1, blk != nblk - 1)
    xe_scr[0:HALO, :] = jnp.where(keep_prev, xp_ref[0].astype(F32), 0.0)
    xe_scr[HALO:HALO + TB, :] = x_ref[0].astype(F32)
    xe_scr[HALO + TB:2 * HALO + TB, :] = jnp.where(keep_next, xn_ref[0].astype(F32), 0.0)
    left = CONV_W // 2
    u = jnp.broadcast_to(cb_ref[...], (TB, LRU_W))
    for tap in range(CONV_W):
        u = u + xe_scr[pl.ds(HALO - left + tap, TB), :] * cw_ref[tap:tap + 1, :]

    g = jnp.dot(u.astype(BF16), wg_ref[0], preferred_element_type=F32) + bg_ref[0, 0:1, :]
    r = _sigmoid(g[:, 0:LRU_W])
    ig = _sigmoid(g[:, LRU_W:2 * LRU_W])
    neg_lam = -lam_ref[0, 0:1, :]
    softplus = jnp.maximum(neg_lam, 0.0) + jnp.log1p(jnp.exp(-jnp.abs(neg_lam)))
    a = jnp.exp(-LRU_C * r * softplus)
    bb = jnp.sqrt(1.0 - a * a) * (ig * u)
    rowi = lax.broadcasted_iota(jnp.int32, (TB, 1), 0)

    def scan(forward):
        aa, hh = a, bb
        k = 1
        while k < TB:
            shift = k if forward else TB - k
            valid = (rowi >= k) if forward else (rowi < TB - k)
            a_sh = jnp.where(valid, pltpu.roll(aa, shift, 0), 1.0)
            h_sh = jnp.where(valid, pltpu.roll(hh, shift, 0), 0.0)
            hh = aa * h_sh + hh
            aa = aa * a_sh
            k *= 2
        hh = hh + aa * h_scr[...]
        o_ref[0, 0] = hh
        last = TB - 1 if forward else 0
        h_scr[...] = hh[last:last + 1, :]

    @pl.when(d == 0)
    def _():
        scan(True)

    @pl.when(d == 1)
    def _():
        scan(False)


def _lru(p, conv_w, conv_b, wg, bg, lam, nctx):
    b, lt, _ = p.shape
    nblk = lt // TB
    hpb = TB // HALO
    nh = lt // HALO

    def blk(d, j):
        return _scan_block(j, d, nctx, nblk)

    return pl.pallas_call(
        functools.partial(_lru_kernel, nctx=nctx, nblk=nblk),
        grid=(b, 2, nblk),
        in_specs=[pl.BlockSpec((1, TB, LRU_W), lambda bi, d, j: (bi, blk(d, j), 4)),
                  pl.BlockSpec((1, HALO, LRU_W), lambda bi, d, j: (bi, jnp.maximum(blk(d, j) * hpb - 1, 0), 4)),
                  pl.BlockSpec((1, HALO, LRU_W), lambda bi, d, j: (bi, jnp.minimum((blk(d, j) + 1) * hpb, nh - 1), 4)),
                  _const_spec((8, LRU_W)), _const_spec((1, LRU_W)),
                  pl.BlockSpec((1, LRU_W, 2 * LRU_W), lambda bi, d, j: (d, 0, 0)),
                  pl.BlockSpec((1, 8, 2 * LRU_W), lambda bi, d, j: (d, 0, 0)),
                  pl.BlockSpec((1, 8, LRU_W), lambda bi, d, j: (d, 0, 0))],
        out_specs=pl.BlockSpec((1, 1, TB, LRU_W), lambda bi, d, j: (d, bi, blk(d, j), 0)),
        out_shape=jax.ShapeDtypeStruct((2, b, lt, LRU_W), F32),
        scratch_shapes=[pltpu.VMEM((TB + 2 * HALO, LRU_W), F32), pltpu.VMEM((1, LRU_W), F32)],
        compiler_params=_cparams(("parallel", "arbitrary", "arbitrary")),
        name="rglru",
    )(p, p, p, conv_w, conv_b, wg, bg, lam)


def _gelu_tanh(t):
    return 0.5 * t * (1.0 + jnp.tanh(0.7978845608028654 * (t + 0.044715 * t * t * t)))


def _merge_kernel(x_ref, mctx_ref, mb_ref, hgf_ref, hgb_ref, hgg_ref, hgn_ref, hm_ref, mlac_ref, mlax_ref,
                  lf_ref, lb_ref, ly_ref, g1_ref, g2_ref, g3_ref, wbr_ref, wout_ref, o_ref, *, ctx_len):
    i = pl.program_id(1)
    nctx = ctx_len // TB
    o_mla = jnp.where(i < nctx, mlac_ref[0], mlax_ref[0])
    o = hgf_ref[0, 0] + hgb_ref[0, 0]
    ms = jnp.dot((o * o).astype(BF16), hm_ref[...], preferred_element_type=F32)
    o_hg = o * lax.rsqrt(ms + EPS) * hgn_ref[...] * _sigmoid(hgg_ref[0].astype(F32))
    o_lru = (lf_ref[0, 0] + lb_ref[0, 0]) * _gelu_tanh(ly_ref[0].astype(F32))
    y = (_sigmoid(g1_ref[0].astype(F32)) * jnp.dot(o_hg.astype(BF16), wbr_ref[0], preferred_element_type=F32)
         + _sigmoid(g2_ref[0].astype(F32)) * jnp.dot(o_mla, wbr_ref[1], preferred_element_type=F32)
         + _sigmoid(g3_ref[0].astype(F32)) * jnp.dot(o_lru.astype(BF16), wbr_ref[2], preferred_element_type=F32))
    gate = _row_mods(i, TB, ctx_len, mctx_ref, mb_ref, 2)
    o_ref[0] = x_ref[0] + gate * jnp.dot(y.astype(BF16), wout_ref[...], preferred_element_type=F32)


def _merge(xs, mctx, mb, o_hg, p, hg_gain, head_mean, o_mla_c, o_mla_x, h_lru, w_br, w_out, ctx_len):
    b, lt, _ = xs.shape
    nctx = ctx_len // TB

    def pcol(width, c):
        return pl.BlockSpec((1, TB, width), lambda bi, i: (bi, i, c))

    def dirspec(d):
        return pl.BlockSpec((1, 1, TB, 512), lambda bi, i: (d, bi, i, 0))

    return pl.pallas_call(
        functools.partial(_merge_kernel, ctx_len=ctx_len),
        grid=(b, lt // TB),
        in_specs=[pl.BlockSpec((1, TB, D_MODEL), lambda bi, i: (bi, i, 0)),
                  _const_spec((6, D_MODEL)),
                  pl.BlockSpec((1, 6, D_MODEL), lambda bi, i: (bi, 0, 0)),
                  dirspec(0), dirspec(0), pcol(512, 2),
                  _const_spec((1, HG_W)), _const_spec((HG_W, HG_W)),
                  pl.BlockSpec((1, TB, MLA_W), lambda bi, i: (bi, jnp.minimum(i, nctx - 1), 0)),
                  pl.BlockSpec((1, TB, MLA_W), lambda bi, i: (bi, jnp.maximum(i - nctx, 0), 0)),
                  dirspec(0), dirspec(1), pcol(512, 5),
                  pcol(D_MODEL, 3), pcol(D_MODEL, 4), pcol(D_MODEL, 5),
                  _const_spec((3, 512, D_MODEL)), _const_spec((D_MODEL, D_MODEL))],
        out_specs=pl.BlockSpec((1, TB, D_MODEL), lambda bi, i: (bi, i, 0)),
        out_shape=jax.ShapeDtypeStruct((b, lt, D_MODEL), F32),
        compiler_params=_cparams(("parallel", "arbitrary")),
        name="merge",
    )(xs, mctx, mb, o_hg[0], o_hg[1], p, hg_gain, head_mean, o_mla_c, o_mla_x, h_lru, h_lru, p, p, p, p, w_br, w_out)


def _first_row_of_max(vals, row, valid):
    masked = jnp.where(valid, vals, -jnp.inf)
    m = jnp.max(masked, axis=0, keepdims=True)
    idx = jnp.min(jnp.where(masked == m, row, ROUTER_PAD), axis=0, keepdims=True)
    return m, idx


def _router(h, wr_hi, wr_lo, br):
    hi = h.astype(BF16)
    lo = (h - hi.astype(F32)).astype(BF16)
    logits = (jnp.dot(hi, wr_hi, preferred_element_type=F32) + jnp.dot(lo, wr_hi, preferred_element_type=F32)
              + jnp.dot(hi, wr_lo, preferred_element_type=F32))
    biased = (logits + br).T
    logits = logits.T
    row = lax.broadcasted_iota(jnp.int32, logits.shape, 0)
    is_group = jnp.logical_and(row >= N_EXPERTS, row < N_EXPERTS + N_GROUPS)
    _, g_row = _first_row_of_max(biased, row, is_group)
    g_max, _ = _first_row_of_max(logits, row, is_group)
    g_exp = jnp.where(is_group, jnp.exp(logits - g_max), 0.0)
    g_sel_logit = jnp.sum(jnp.where(row == g_row, logits, 0.0), axis=0, keepdims=True)
    p_g = jnp.exp(g_sel_logit - g_max) / jnp.sum(g_exp, axis=0, keepdims=True)
    in_group = jnp.right_shift(row, 2) == (g_row - N_EXPERTS)
    _, i1 = _first_row_of_max(biased, row, in_group)
    _, i2 = _first_row_of_max(biased, row, jnp.logical_and(in_group, row != i1))
    l1 = jnp.sum(jnp.where(row == i1, logits, 0.0), axis=0, keepdims=True)
    l2 = jnp.sum(jnp.where(row == i2, logits, 0.0), axis=0, keepdims=True)
    lm = jnp.maximum(l1, l2)
    e1 = jnp.exp(l1 - lm)
    e2 = jnp.exp(l2 - lm)
    inv = p_g / (e1 + e2)
    comb_t = jnp.where(row == i1, e1 * inv, 0.0) + jnp.where(row == i2, e2 * inv, 0.0)
    return comb_t.T


def _moe_kernel(x_ref, mctx_ref, mb_ref, gain_ref, wrh_ref, wrl_ref, br_ref, w1_ref, w3_ref, w2_ref, o_ref, *,
                ctx_len, tm, tile0):
    i = pl.program_id(1) + tile0
    epb = N_EXPERTS // EXPERT_STEPS
    nrows = min(MOE_RB, tm)
    for rb in range(tm // nrows):
        rows = slice(rb * nrows, (rb + 1) * nrows)
        tile = i * (tm // nrows) + rb
        shift = _row_mods(tile, nrows, ctx_len, mctx_ref, mb_ref, 3)
        scale = _row_mods(tile, nrows, ctx_len, mctx_ref, mb_ref, 4)
        gate = _row_mods(tile, nrows, ctx_len, mctx_ref, mb_ref, 5)
        x = x_ref[0, rows, :]
        h = _norm_modulate(x, gain_ref[...], shift, scale)
        comb = _router(h, wrh_ref[...], wrl_ref[...], br_ref[...])
        hb = h.astype(BF16)
        acc = None
        for es in range(EXPERT_STEPS):
            parts = []
            for e in range(es * epb, (es + 1) * epb):
                h1 = jnp.dot(hb, w1_ref[e], preferred_element_type=F32)
                h3 = jnp.dot(hb, w3_ref[e], preferred_element_type=F32)
                parts.append(h1 * _sigmoid(h1) * h3 * comb[:, e:e + 1])
            y = jnp.dot(jnp.concatenate(parts, axis=-1).astype(BF16), w2_ref[es], preferred_element_type=F32)
            acc = y if acc is None else acc + y
        o_ref[0, rows, :] = x + gate * acc


def _moe(xs, mctx, mb, gain, wr_hi, wr_lo, br, w1, w3, w2, ctx_len, latent_only):
    b, lt, _ = xs.shape
    epb = N_EXPERTS // EXPERT_STEPS
    if latent_only:
        tm, tile0, rows = TB, ctx_len // TB, lt - ctx_len
    else:
        tm, tile0, rows = (MOE_TM if lt % MOE_TM == 0 else TB), 0, lt
    return pl.pallas_call(
        functools.partial(_moe_kernel, ctx_len=ctx_len, tm=tm, tile0=tile0),
        grid=(b, rows // tm),
        in_specs=[pl.BlockSpec((1, tm, D_MODEL), lambda bi, i: (bi, i + tile0, 0)),
                  _const_spec((6, D_MODEL)),
                  pl.BlockSpec((1, 6, D_MODEL), lambda bi, i: (bi, 0, 0)),
                  _const_spec((1, D_MODEL)),
                  _const_spec((D_MODEL, ROUTER_PAD)), _const_spec((D_MODEL, ROUTER_PAD)), _const_spec((1, ROUTER_PAD)),
                  _const_spec((N_EXPERTS, D_MODEL, D_EXPERT)), _const_spec((N_EXPERTS, D_MODEL, D_EXPERT)),
                  _const_spec((EXPERT_STEPS, epb * D_EXPERT, D_MODEL))],
        out_specs=pl.BlockSpec((1, tm, D_MODEL), lambda bi, i: (bi, i, 0)),
        out_shape=jax.ShapeDtypeStruct((b, rows, D_MODEL), F32),
        compiler_params=_cparams(("parallel", "arbitrary")),
        name="moe",
    )(xs, mctx, mb, gain, wr_hi, wr_lo, br, w1, w3, w2)


def _rope_tables(ctx_len, seq):
    half = MLA_ROPE // 2
    rows = seq // GRID_W
    pos_row = np.repeat(np.arange(rows, dtype=np.float32), GRID_W)
    pos_col = np.tile(np.arange(GRID_W, dtype=np.float32), rows)
    inv = (ROPE_BASE ** (-np.arange(0, half, 2, dtype=np.float32) / half)).astype(np.float32)
    ang = np.concatenate([pos_row[:, None] * inv, pos_col[:, None] * inv], axis=-1)
    cos, sin = np.cos(ang), np.sin(ang)
    cos_t = np.zeros((ctx_len + seq, HEAD_PAD), np.float32)
    sin_t = np.zeros((ctx_len + seq, HEAD_PAD), np.float32)
    cos_t[:, :MLA_NOPE] = 1.0
    cos_t[:ctx_len, MLA_NOPE:MLA_QK] = 1.0
    cos_t[ctx_len:, MLA_NOPE:MLA_NOPE + half] = cos
    cos_t[ctx_len:, MLA_NOPE + half:MLA_QK] = cos
    sin_t[ctx_len:, MLA_NOPE:MLA_NOPE + half] = -sin
    sin_t[ctx_len:, MLA_NOPE + half:MLA_QK] = sin
    return cos_t, sin_t


def _rope_key_selectors():
    half = MLA_ROPE // 2
    kr0 = MLA_Q_RANK + MLA_KV_RANK
    sk1 = np.zeros((512, MLA_HEADS * HEAD_PAD), np.float32)
    sk2 = np.zeros((512, MLA_HEADS * HEAD_PAD), np.float32)
    for h in range(MLA_HEADS):
        for i in range(MLA_ROPE):
            sk1[kr0 + i, h * HEAD_PAD + MLA_NOPE + i] = 1.0
            sk2[kr0 + (i + half) % MLA_ROPE, h * HEAD_PAD + MLA_NOPE + i] = 1.0
    return sk1.astype(BF16), sk2.astype(BF16)


def _np_block_diag(block, n):
    a, bb = block.shape
    out = np.zeros((n * a, n * bb), np.float32)
    for i in range(n):
        out[i * a:(i + 1) * a, i * bb:(i + 1) * bb] = block
    return out


def _swap_rope_halves(t):
    half = MLA_ROPE // 2
    return jnp.concatenate([jnp.zeros_like(t[..., :MLA_NOPE]), t[..., MLA_NOPE + half:], t[..., MLA_NOPE:MLA_NOPE + half]],
                           axis=-1)


def _pad_heads(t):
    pad = [(0, 0)] * (t.ndim - 1) + [(0, HEAD_PAD - t.shape[-1])]
    t = jnp.pad(t, pad)
    return t.reshape(t.shape[:-2] + (t.shape[-2] * HEAD_PAD,))


def _mla_weights(w_uq, w_ukv, gq, gk):
    wq = w_uq.reshape(MLA_Q_RANK, MLA_HEADS, MLA_QK)
    wq1 = _pad_heads(wq).astype(BF16)
    wq2 = _pad_heads(_swap_rope_halves(wq)).astype(BF16)
    wkv = w_ukv.reshape(MLA_KV_RANK, MLA_HEADS, MLA_NOPE + MLA_V)
    wk1 = _pad_heads(wkv[..., :MLA_NOPE]).astype(BF16)
    wv = _pad_heads(wkv[..., MLA_NOPE:]).astype(BF16)

    def gains(g):
        g1 = jnp.pad(g, (0, HEAD_PAD - MLA_QK))
        g2 = jnp.pad(_swap_rope_halves(g), (0, HEAD_PAD - MLA_QK))
        return jnp.stack([g1, g2], axis=0)

    return wq1, wq2, wk1, wv, gains(gq), gains(gk)


def _block_diag(w):
    n, a, bb = w.shape
    eye = jnp.eye(n, dtype=w.dtype)
    return (eye[:, None, :, None] * w[:, :, None, :]).reshape(n * a, n * bb)


def _pad_rows(t, rows=8):
    return jnp.pad(t, ((0, rows - t.shape[0]), (0, 0)))


def kernel(x, c, ctx, c_ctx, w_mod, b_mod, norm_mix, norm_ffn, w_in, hg_lb, hg_norm, mla_q_norm, mla_kv_norm, mla_w_uq, mla_w_ukv, mla_qk_gain_q, mla_qk_gain_k, lru_conv_w, lru_conv_b, lru_wa, lru_ba, lru_wx, lru_bx, lru_lambda, w_br_hg, w_br_mla, w_br_lru, w_out, moe_w_rg, moe_b_rg, moe_w_re, moe_b_re, moe_w1, moe_w3, moe_w2):
    bsz, seq, _ = x.shape
    ctx_len = ctx.shape[1]
    depth = w_in.shape[0]
    assert seq % TB == 0 and ctx_len % TB == 0 and seq % GRID_W == 0 and bsz < 8
    nctx = ctx_len // TB

    xs = jnp.concatenate([ctx, x], axis=1)
    cos_t, sin_t = _rope_tables(ctx_len, seq)
    cvec = jnp.zeros((8, D_MODEL), F32).at[:bsz].set(c).at[bsz].set(c_ctx)

    lb_cs = jnp.cumsum(jax.nn.softmax(hg_lb.astype(F32), axis=0), axis=0)
    lb_all = lb_cs - lb_cs[0:1]
    tri_lo = _np_block_diag(np.tril(np.ones((HG_CHUNK, HG_CHUNK), np.float32)), TB // HG_CHUNK)
    tri = np.stack([tri_lo, tri_lo.T], axis=0).astype(BF16)
    head_ones = _np_block_diag(np.ones((HG_DK, HG_DK), np.float32), HG_HEADS)
    head_mean = (head_ones / HG_DK).astype(BF16)
    sk1, sk2 = _rope_key_selectors()

    offs = np.cumsum((HG_W,) * 5 + (MLA_Q_RANK, MLA_KV_RANK, MLA_ROPE, LRU_W, LRU_W) + (D_MODEL,) * 3)[:-1].tolist()

    for l in range(depth):
        mods = _modulation(cvec, w_mod[l], b_mod[l]).reshape(8, 6, D_MODEL)
        mctx, mb = mods[bsz], mods[:bsz]

        q_, ff, fb, i_, g_, dq, dkv, kr, lx, ly, g1, g2, g3 = jnp.split(w_in[l].astype(BF16), offs, axis=-1)
        kr_pad = jnp.zeros((D_MODEL, 512 - MLA_Q_RANK - MLA_KV_RANK - MLA_ROPE), BF16)
        w_all = jnp.concatenate([ff, fb, q_, i_, g_, dq, dkv, kr, kr_pad, lx, ly, g1, g2, g3], axis=-1)
        f, p = _inproj(xs, mctx, mb, norm_mix[l][None, :], w_all, ctx_len)

        lb = lb_all[l]
        lbc = jnp.stack([jnp.log(lb), jnp.log1p(-lb), 1.0 - lb] + [jnp.zeros_like(lb)] * 5, axis=1)
        o_hg = [_hgrn(p, f, tri, lbc, head_ones, nctx, reverse) for reverse in (False, True)]

        wq1, wq2, wk1, wv, gq, gk = _mla_weights(mla_w_uq[l], mla_w_ukv[l], mla_qk_gain_q[l], mla_qk_gain_k[l])
        qt, kh, vt, ksq = _mla_prep(p, cos_t, sin_t, mla_q_norm[l][None, :], mla_kv_norm[l][None, :], gq, gk,
                                    wq1, wq2, wk1, sk1, sk2, wv, nctx)
        kmax = jnp.sqrt(jnp.max(ksq, axis=(1, 3)))
        tq = next(t for t in (512, 256) if seq % t == 0)
        o_mla_x = _flash(kmax, qt, kh, vt, seq, ctx_len + seq, tq, 0, "mla_attention")
        if l < depth - 1:
            o_mla_c = _flash(kmax, qt, kh, vt, ctx_len, ctx_len, TB, seq, "mla_attention_ctx")
        else:
            o_mla_c = o_mla_x[:, :ctx_len]

        wg = jnp.stack([jnp.concatenate([_block_diag(lru_wa[l, d]), _block_diag(lru_wx[l, d])], axis=-1)
                        for d in range(2)], axis=0).astype(BF16)
        bg = jnp.concatenate([lru_ba[l], lru_bx[l]], axis=-1)[:, None, :] * jnp.ones((1, 8, 1), F32)
        lam = lru_lambda[l][:, None, :] * jnp.ones((1, 8, 1), F32)
        h_lru = _lru(p, _pad_rows(lru_conv_w[l]), lru_conv_b[l][None, :], wg, bg, lam, nctx)

        w_br = jnp.stack([w_br_hg[l], w_br_mla[l], w_br_lru[l]], axis=0).astype(BF16)
        hg_gain = jnp.tile(hg_norm[l], HG_HEADS)[None, :]
        xs = _merge(xs, mctx, mb, o_hg, p, hg_gain, head_mean, o_mla_c, o_mla_x, h_lru, w_br, w_out[l].astype(BF16), ctx_len)

        wr = jnp.pad(jnp.concatenate([moe_w_re[l], moe_w_rg[l]], axis=-1), ((0, 0), (0, ROUTER_PAD - N_EXPERTS - N_GROUPS)))
        br = jnp.pad(jnp.concatenate([moe_b_re[l], moe_b_rg[l]]), (0, ROUTER_PAD - N_EXPERTS - N_GROUPS))[None, :]
        w2 = moe_w2[l].astype(BF16).reshape(EXPERT_STEPS, N_EXPERTS // EXPERT_STEPS * D_EXPERT, D_MODEL)
        wr_hi = wr.astype(BF16)
        wr_lo = (wr - wr_hi.astype(F32)).astype(BF16)
        xs = _moe(xs, mctx, mb, norm_ffn[l][None, :], wr_hi, wr_lo, br, moe_w1[l].astype(BF16), moe_w3[l].astype(BF16), w2,
                  ctx_len, latent_only=(l == depth - 1))

    return xs
```

```python
import functools

import numpy as np
import jax
import jax.numpy as jnp
from jax import lax
from jax.experimental import pallas as pl
from jax.experimental.pallas import tpu as pltpu

F32 = jnp.float32
BF16 = jnp.bfloat16
HIGHEST = lax.Precision.HIGHEST

D_MODEL = 1024
GRID_W = 64
EPS = 1e-6

HG_HEADS = 8
HG_DK = 64
HG_W = 512
HG_CHUNK = 64
HG_MID = HG_CHUNK // 2
HG_MAX_LOG_RANGE = 80.0

MLA_HEADS = 8
MLA_Q_RANK = 256
MLA_KV_RANK = 128
MLA_NOPE = 64
MLA_ROPE = 32
MLA_V = 64
MLA_QK = MLA_NOPE + MLA_ROPE
MLA_W = MLA_HEADS * MLA_V
HEAD_PAD = 128
V_ROWS = 80
SCORE_SCALE = MLA_QK ** -0.5 * 1.4426950408889634
MAX_UNSHIFTED_SCORE = 57.0
ROPE_BASE = 10000.0

LRU_W = 512
LRU_BLOCKS = 8
LRU_BD = LRU_W // LRU_BLOCKS
CONV_W = 4
LRU_C = 8.0

N_GROUPS = 4
EXP_PER_GROUP = 4
N_EXPERTS = N_GROUPS * EXP_PER_GROUP
D_EXPERT = 256
EXPERT_STEPS = 4
ROUTER_PAD = 128
MOE_TM = 768
MOE_RB = 256
INPROJ_TM = 768

TB = 256
HALO = 16
SCAN_GROUP = 8
P_WIDTH = 6144
F_WIDTH = 1024
VMEM_LIMIT = 56 * 1024 * 1024

NT_DIMS = (((1,), (1,)), ((), ()))


def _cparams(sem):
    return pltpu.CompilerParams(dimension_semantics=sem, vmem_limit_bytes=VMEM_LIMIT)


def _const_spec(shape):
    nd = len(shape)
    return pl.BlockSpec(shape, lambda *_: (0,) * nd, pipeline_mode=pl.Buffered(1))


def _sigmoid(t):
    return 0.5 * jnp.tanh(0.5 * t) + 0.5


def _scan_block(j, d, nctx, nblk):
    fwd = j
    bwd = jnp.where(j < nctx, nctx - 1 - j, nblk - 1 - (j - nctx))
    return jnp.where(d == 0, fwd, bwd)


def _mod_kernel(c_ref, w_ref, b_ref, o_ref):
    c = c_ref[...]
    s = c * _sigmoid(c)
    o_ref[...] = jnp.dot(s, w_ref[...], precision=HIGHEST, preferred_element_type=F32) + b_ref[...]


def _modulation(cvec, w_mod, b_mod):
    n = w_mod.shape[1]
    tn = 1024
    return pl.pallas_call(
        _mod_kernel,
        grid=(n // tn,),
        in_specs=[pl.BlockSpec((8, D_MODEL), lambda j: (0, 0)),
                  pl.BlockSpec((D_MODEL, tn), lambda j: (0, j)),
                  pl.BlockSpec((1, tn), lambda j: (0, j))],
        out_specs=pl.BlockSpec((8, tn), lambda j: (0, j)),
        out_shape=jax.ShapeDtypeStruct((8, n), F32),
        compiler_params=_cparams(("arbitrary",)),
        name="modulation",
    )(cvec, w_mod, b_mod.reshape(1, n))


def _row_mods(i, tm, ctx_len, mctx_ref, mb_ref, k):
    row = i * tm + lax.broadcasted_iota(jnp.int32, (tm, 1), 0)
    return jnp.where(row < ctx_len, mctx_ref[k:k + 1, :], mb_ref[0, k:k + 1, :])


def _norm_modulate(x, gain, shift, scale):
    ms = jnp.mean(x * x, axis=-1, keepdims=True)
    xn = x * lax.rsqrt(ms + EPS) * gain
    return xn * (1.0 + scale) + shift


def _inproj_kernel(x_ref, mctx_ref, mb_ref, gain_ref, w_ref, f_ref, p_ref, *, ctx_len, tm):
    i = pl.program_id(1)
    shift = _row_mods(i, tm, ctx_len, mctx_ref, mb_ref, 0)
    scale = _row_mods(i, tm, ctx_len, mctx_ref, mb_ref, 1)
    h = _norm_modulate(x_ref[0], gain_ref[...], shift, scale).astype(BF16)
    f_ref[0] = jnp.dot(h, w_ref[:, 0:F_WIDTH], preferred_element_type=F32)
    cw = 512
    for j in range(P_WIDTH // cw):
        lo = F_WIDTH + j * cw
        p_ref[0, :, j * cw:(j + 1) * cw] = jnp.dot(
            h, w_ref[:, lo:lo + cw], preferred_element_type=F32).astype(BF16)


def _inproj(xs, mctx, mb, gain, w_all, ctx_len):
    b, lt, _ = xs.shape
    tm = INPROJ_TM if lt % INPROJ_TM == 0 else TB
    return pl.pallas_call(
        functools.partial(_inproj_kernel, ctx_len=ctx_len, tm=tm),
        grid=(b, lt // tm),
        in_specs=[pl.BlockSpec((1, tm, D_MODEL), lambda bi, i: (bi, i, 0)),
                  _const_spec((6, D_MODEL)),
                  pl.BlockSpec((1, 6, D_MODEL), lambda bi, i: (bi, 0, 0)),
                  _const_spec((1, D_MODEL)),
                  _const_spec((D_MODEL, F_WIDTH + P_WIDTH))],
        out_specs=[pl.BlockSpec((1, tm, F_WIDTH), lambda bi, i: (bi, i, 0)),
                   pl.BlockSpec((1, tm, P_WIDTH), lambda bi, i: (bi, i, 0))],
        out_shape=[jax.ShapeDtypeStruct((b, lt, F_WIDTH), F32),
                   jax.ShapeDtypeStruct((b, lt, P_WIDTH), BF16)],
        compiler_params=_cparams(("parallel", "arbitrary")),
        name="inproj",
    )(xs, mctx, mb, gain, w_all)


def _hgrn_kernel(q_ref, v_ref, f_ref, tri_ref, lbc_ref, bd_ref, o_ref, st_ref, c_scr, k_scr, v_scr, *, reverse):
    j = pl.program_id(1)
    nch = TB // HG_CHUNK
    npair = HG_W // 128
    first, last = (HG_CHUNK - 1, 0) if reverse else (0, HG_CHUNK - 1)

    @pl.when(j == 0)
    def _():
        st_ref[...] = jnp.zeros_like(st_ref)

    log_lb = lbc_ref[0, 0:1, :]
    log_1mlb = lbc_ref[0, 1:2, :]
    one_mlb = lbc_ref[0, 2:3, :]
    z = f_ref[0]
    q = q_ref[0].astype(F32)
    v = v_ref[0].astype(F32)
    e = jnp.exp(-jnp.abs(z))
    log_sig = jnp.minimum(z, 0.0) - jnp.log(1.0 + e)
    t = log_1mlb + log_sig
    logf = jnp.maximum(log_lb, t) + jnp.log(1.0 + jnp.exp(-jnp.abs(log_lb - t)))
    k = one_mlb * (jnp.where(z >= 0, e, 1.0) / (1.0 + e))
    hi = logf.astype(BF16)
    lo = (logf - hi.astype(F32)).astype(BF16)
    tri = tri_ref[0]
    c = jnp.dot(tri, hi, preferred_element_type=F32) + jnp.dot(tri, lo, preferred_element_type=F32)

    span = jnp.zeros((1, HG_W), F32)
    for ci in range(nch):
        r0 = ci * HG_CHUNK
        c_mid = c[r0 + HG_MID:r0 + HG_MID + 1, :]
        span = jnp.maximum(span, jnp.maximum(c[r0 + first:r0 + first + 1, :] - c_mid, c_mid - c[r0 + last:r0 + last + 1, :]))
    span = jnp.max(span)

    lane = lax.broadcasted_iota(jnp.int32, (HG_CHUNK, HG_W), 1)
    low_head = (lane & (HG_DK * 2 - 1)) < HG_DK
    ri = lax.broadcasted_iota(jnp.int32, (HG_CHUNK, 128), 0)
    si = lax.broadcasted_iota(jnp.int32, (HG_CHUNK, 128), 1) & (HG_DK - 1)
    visited = (si >= ri) if reverse else (si <= ri)
    r2 = lax.broadcasted_iota(jnp.int32, (128, 128), 0)
    l2 = lax.broadcasted_iota(jnp.int32, (128, 128), 1)
    same_head = (r2 < HG_DK) == (l2 < HG_DK)
    rowi = lax.broadcasted_iota(jnp.int32, (HG_CHUNK, 1), 0)

    def split_heads(t):
        top = jnp.where(low_head, t, 0.0).astype(BF16)
        bot = jnp.where(low_head, 0.0, t).astype(BF16)
        return [jnp.concatenate([top[:, 128 * g:128 * (g + 1)], bot[:, 128 * g:128 * (g + 1)]], axis=0) for g in range(npair)]

    def exact_intra(qq, kk, vv, cc):
        c_scr[...] = cc
        k_scr[...] = kk
        v_scr[...] = vv
        bd = bd_ref[...]

        def key_row(s, acc):
            c_s = c_scr[pl.ds(s, 1), :]
            w = qq * jnp.exp(jnp.minimum(cc - c_s, 0.0)) * k_scr[pl.ds(s, 1), :]
            ws = jnp.dot(w, bd, precision=HIGHEST, preferred_element_type=F32)
            ok = (rowi <= s) if reverse else (rowi >= s)
            return acc + jnp.where(ok, ws, 0.0) * v_scr[pl.ds(s, 1), :]

        return lax.fori_loop(0, HG_CHUNK, key_row, jnp.zeros((HG_CHUNK, HG_W), F32))

    def chunk(ci, exact):
        r0 = ci * HG_CHUNK
        rows = slice(r0, r0 + HG_CHUNK)
        cc, qq, kk, vv = c[rows], q[rows], k[rows], v[rows]
        c_mid = cc[HG_MID:HG_MID + 1, :]
        c_end = cc[last:last + 1, :]
        qs = (qq * jnp.exp(cc)).astype(BF16)
        ks = (kk * jnp.exp(c_end - cc)).astype(BF16)
        dec = jnp.exp(c_end)
        if exact:
            intra = exact_intra(qq, kk, vv, cc)
        else:
            qd = (qq * jnp.exp(cc - c_mid)).astype(BF16)
            kd_blocks = split_heads(kk * jnp.exp(c_mid - cc))
            v_blocks = split_heads(vv)
        pieces = []
        for g in range(npair):
            ls = slice(128 * g, 128 * (g + 1))
            st = st_ref[g]
            o_g = lax.dot_general(qs[:, ls], st.astype(BF16), NT_DIMS, preferred_element_type=F32)
            if not exact:
                a = lax.dot_general(qd[:, ls], kd_blocks[g], NT_DIMS, preferred_element_type=F32)
                a = jnp.where(visited, a, 0.0).astype(BF16)
                o_g = o_g + jnp.dot(a, v_blocks[g], preferred_element_type=F32)
            vt = vv[:, ls].T.astype(BF16)
            kv = jnp.dot(vt, ks[:, ls], preferred_element_type=F32)
            st_ref[g] = st * dec[:, ls] + jnp.where(same_head, kv, 0.0)
            pieces.append(o_g)
        o = jnp.concatenate(pieces, axis=-1)
        o_ref[0, 0, rows, :] = o + intra if exact else o

    order = range(nch - 1, -1, -1) if reverse else range(nch)

    @pl.when(span <= HG_MAX_LOG_RANGE)
    def _matmul_form():
        for ci in order:
            chunk(ci, False)

    @pl.when(span > HG_MAX_LOG_RANGE)
    def _exact_form():
        for ci in order:
            chunk(ci, True)


def _hgrn(p, f, tri, lbc, bd, nctx, reverse):
    b, lt, _ = p.shape
    nblk = lt // TB
    d = 1 if reverse else 0

    def blk(j):
        return _scan_block(j, d, nctx, nblk)

    return pl.pallas_call(
        functools.partial(_hgrn_kernel, reverse=reverse),
        grid=(b, nblk),
        in_specs=[pl.BlockSpec((1, TB, HG_W), lambda bi, j: (bi, blk(j), 0)),
                  pl.BlockSpec((1, TB, HG_W), lambda bi, j: (bi, blk(j), 1)),
                  pl.BlockSpec((1, TB, HG_W), lambda bi, j: (bi, blk(j), d)),
                  pl.BlockSpec((1, TB, TB), lambda bi, j: (d, 0, 0), pipeline_mode=pl.Buffered(1)),
                  pl.BlockSpec((1, 8, HG_W), lambda bi, j: (d, 0, 0), pipeline_mode=pl.Buffered(1)),
                  _const_spec((HG_W, HG_W))],
        out_specs=pl.BlockSpec((1, 1, TB, HG_W), lambda bi, j: (0, bi, blk(j), 0)),
        out_shape=jax.ShapeDtypeStruct((1, b, lt, HG_W), F32),
        scratch_shapes=[pltpu.VMEM((HG_W // 128, 128, 128), F32),
                        pltpu.VMEM((HG_CHUNK, HG_W), F32),
                        pltpu.VMEM((HG_CHUNK, HG_W), F32),
                        pltpu.VMEM((HG_CHUNK, HG_W), F32)],
        compiler_params=_cparams(("parallel", "arbitrary")),
        name="hgrn2_bwd" if reverse else "hgrn2_fwd",
    )(p, p, f, tri, lbc, bd)


def _rms_rows(t, width):
    return lax.rsqrt(jnp.sum(t * t, axis=-1, keepdims=True) * (1.0 / width) + EPS)


def _mla_prep_kernel(pm_ref, cos_ref, sin_ref, qn_ref, kvn_ref, gq_ref, gk_ref,
                     wq1_ref, wq2_ref, wk1_ref, sk1_ref, sk2_ref, wv_ref,
                     qt_ref, k_ref, vt_ref, kn_ref):
    pm = pm_ref[0]
    dq = pm[:, 0:MLA_Q_RANK].astype(F32)
    dkv = pm[:, MLA_Q_RANK:MLA_Q_RANK + MLA_KV_RANK].astype(F32)
    dqn = (dq * _rms_rows(dq, MLA_Q_RANK) * qn_ref[...]).astype(BF16)
    dkvn = (dkv * _rms_rows(dkv, MLA_KV_RANK) * kvn_ref[...]).astype(BF16)
    q1 = jnp.dot(dqn, wq1_ref[...], preferred_element_type=F32)
    q2 = jnp.dot(dqn, wq2_ref[...], preferred_element_type=F32)
    k1 = (jnp.dot(dkvn, wk1_ref[...], preferred_element_type=F32)
          + jnp.dot(pm, sk1_ref[...], preferred_element_type=F32))
    k2 = jnp.dot(pm, sk2_ref[...], preferred_element_type=F32)
    vv = jnp.dot(dkvn, wv_ref[...], preferred_element_type=F32)
    cos = cos_ref[...]
    sin = sin_ref[...]
    gq_c = gq_ref[0:1, :] * cos
    gq_s = gq_ref[1:2, :] * sin
    gk_c = gk_ref[0:1, :] * cos
    gk_s = gk_ref[1:2, :] * sin
    vrow = lax.broadcasted_iota(jnp.int32, (V_ROWS, TB), 0)
    kn = []
    for h in range(MLA_HEADS):
        sl = slice(HEAD_PAD * h, HEAD_PAD * (h + 1))
        qh = q1[:, sl]
        qo = (qh * gq_c + q2[:, sl] * gq_s) * (_rms_rows(qh, MLA_QK) * SCORE_SCALE)
        qt_ref[0, h] = qo.T.astype(BF16)
        kh = k1[:, sl]
        ko = (kh * gk_c + k2[:, sl] * gk_s) * _rms_rows(kh, MLA_QK)
        k_ref[0, h] = ko.astype(BF16)
        ksq = jnp.max(jnp.sum(ko * ko, axis=-1, keepdims=True), axis=0, keepdims=True)
        kn.append(jnp.broadcast_to(ksq, (1, HEAD_PAD)))
        vt = vv[:, sl].T[0:V_ROWS, :]
        vt_ref[0, h] = jnp.where(vrow == MLA_V, 1.0, vt).astype(BF16)
    kn_ref[0, 0] = jnp.concatenate(kn, axis=0)


def _mla_prep(p, cos_t, sin_t, qn, kvn, gq, gk, wq1, wq2, wk1, sk1, sk2, wv, nctx):
    b, lt, _ = p.shape
    hw = MLA_HEADS * HEAD_PAD
    nblk = lt // TB
    return pl.pallas_call(
        _mla_prep_kernel,
        grid=(b, lt // TB),
        in_specs=[pl.BlockSpec((1, TB, 512), lambda bi, i: (bi, i, 3)),
                  pl.BlockSpec((TB, HEAD_PAD), lambda bi, i: (i, 0)),
                  pl.BlockSpec((TB, HEAD_PAD), lambda bi, i: (i, 0)),
                  _const_spec((1, MLA_Q_RANK)), _const_spec((1, MLA_KV_RANK)),
                  _const_spec((2, HEAD_PAD)), _const_spec((2, HEAD_PAD)),
                  _const_spec((MLA_Q_RANK, hw)), _const_spec((MLA_Q_RANK, hw)),
                  _const_spec((MLA_KV_RANK, hw)), _const_spec((512, hw)), _const_spec((512, hw)),
                  _const_spec((MLA_KV_RANK, hw))],
        out_specs=[pl.BlockSpec((1, MLA_HEADS, HEAD_PAD, TB), lambda bi, i: (bi, 0, 0, (i + nblk - nctx) % nblk)),
                   pl.BlockSpec((1, MLA_HEADS, TB, HEAD_PAD), lambda bi, i: (bi, 0, i, 0)),
                   pl.BlockSpec((1, MLA_HEADS, V_ROWS, TB), lambda bi, i: (bi, 0, 0, i)),
                   pl.BlockSpec((1, 1, MLA_HEADS, HEAD_PAD), lambda bi, i: (bi, i, 0, 0))],
        out_shape=[jax.ShapeDtypeStruct((b, MLA_HEADS, HEAD_PAD, lt), BF16),
                   jax.ShapeDtypeStruct((b, MLA_HEADS, lt, HEAD_PAD), BF16),
                   jax.ShapeDtypeStruct((b, MLA_HEADS, V_ROWS, lt), BF16),
                   jax.ShapeDtypeStruct((b, lt // TB, MLA_HEADS, HEAD_PAD), F32)],
        compiler_params=_cparams(("parallel", "arbitrary")),
        name="mla_prep",
    )(p, cos_t, sin_t, qn, kvn, gq, gk, wq1, wq2, wk1, sk1, sk2, wv)


def _attend_bounded(qts, k_ref, vt_ref, s_scr, nkeys, tk):
    n = nkeys // tk

    def scores(kb, slot):
        for hh in range(2):
            s_scr[slot, hh, 0:tk, :] = jnp.dot(k_ref[0, hh, kb * tk:(kb + 1) * tk, :], qts[hh], preferred_element_type=F32)

    def accumulate(kb, slot, accs):
        out = []
        for hh in range(2):
            p = jnp.exp2(s_scr[slot, hh, 0:tk, :]).astype(BF16)
            pv = jnp.dot(vt_ref[0, hh, :, kb * tk:(kb + 1) * tk], p, preferred_element_type=F32)
            out.append(pv if accs is None else accs[hh] + pv)
        return out

    scores(0, 0)
    accs = None
    for kb in range(n):
        if kb + 1 < n:
            scores(kb + 1, (kb + 1) % 2)
        accs = accumulate(kb, kb % 2, accs)
    return accs


def _attend_online(qts, k_ref, vt_ref, nkeys, tk):
    def kv_step(kb, carry):
        r0 = pl.multiple_of(kb * tk, tk)
        out = []
        for hh in range(2):
            m, acc = carry[hh]
            s = jnp.dot(k_ref[0, hh, pl.ds(r0, tk), :], qts[hh], preferred_element_type=F32)
            m_new = jnp.maximum(m, jnp.max(s, axis=0, keepdims=True))
            p = jnp.exp2(s - m_new).astype(BF16)
            acc = jnp.exp2(m - m_new) * acc + jnp.dot(vt_ref[0, hh, :, pl.ds(r0, tk)], p, preferred_element_type=F32)
            out.append((m_new, acc))
        return tuple(out)

    tq = qts[0].shape[1]
    init = (jnp.full((1, tq), -jnp.inf, F32), jnp.zeros((V_ROWS, tq), F32))
    res = lax.fori_loop(0, nkeys // tk, kv_step, (init, init))
    return res[0][1], res[1][1]


def _flash_kernel(kmax_ref, qt_ref, k_ref, vt_ref, o_ref, s_scr, *, nkeys, tk):
    bi = pl.program_id(0)
    g = pl.program_id(1)

    qts = [qt_ref[0, hh] for hh in range(2)]
    worst = 0.0
    for hh in range(2):
        qt = qts[hh].astype(F32)
        qnorm = jnp.sqrt(jnp.max(jnp.sum(qt * qt, axis=0, keepdims=True)))
        worst = jnp.maximum(worst, qnorm * kmax_ref[bi, 2 * g + hh])

    def finish(accs):
        outs = [(acc[0:MLA_V, :] / acc[MLA_V:MLA_V + 1, :]).T for acc in accs]
        o_ref[0] = jnp.concatenate(outs, axis=-1).astype(BF16)

    @pl.when(worst <= MAX_UNSHIFTED_SCORE)
    def _():
        finish(_attend_bounded(qts, k_ref, vt_ref, s_scr, nkeys, tk))

    @pl.when(worst > MAX_UNSHIFTED_SCORE)
    def _():
        finish(_attend_online(qts, k_ref, vt_ref, nkeys, tk))


def _flash(kmax, qt, k, vt, nq, nkeys, tq, q_col0, name):
    b = k.shape[0]
    tk = next(t for t in (768, 512, 256) if nkeys % t == 0)
    cb0 = q_col0 // tq
    return pl.pallas_call(
        functools.partial(_flash_kernel, nkeys=nkeys, tk=tk),
        grid=(b, MLA_HEADS // 2, nq // tq),
        in_specs=[pl.BlockSpec(memory_space=pltpu.SMEM),
                  pl.BlockSpec((1, 2, HEAD_PAD, tq), lambda bi, g, i: (bi, g, 0, cb0 + i)),
                  pl.BlockSpec((1, 2, nkeys, HEAD_PAD), lambda bi, g, i: (bi, g, 0, 0)),
                  pl.BlockSpec((1, 2, V_ROWS, nkeys), lambda bi, g, i: (bi, g, 0, 0))],
        out_specs=pl.BlockSpec((1, tq, HEAD_PAD), lambda bi, g, i: (bi, i, g)),
        out_shape=jax.ShapeDtypeStruct((b, nq, MLA_W), BF16),
        scratch_shapes=[pltpu.VMEM((2, 2, tk, tq), F32)],
        compiler_params=_cparams(("parallel", "parallel", "arbitrary")),
        name=name,
    )(kmax, qt, k, vt)


def _lru_kernel(x_ref, xp_ref, xn_ref, cw_ref, cb_ref, wg_ref, bg_ref, lam_ref, o_ref, xe_scr, h_scr, *, nctx, nblk):
    d = pl.program_id(1)
    j = pl.program_id(2)
    blk = _scan_block(j, d, nctx, nblk)

    @pl.when(j == 0)
    def _():
        h_scr[...] = jnp.zeros_like(h_scr)

    keep_prev = jnp.logical_and(blk != 0, blk != nctx)
    keep_next = jnp.logical_and(blk != nctx - 1, blk != nblk - 1)
    xe_scr[0:HALO, :] = jnp.where(keep_prev, xp_ref[0].astype(F32), 0.0)
    xe_scr[HALO:HALO + TB, :] = x_ref[0].astype(F32)
    xe_scr[HALO + TB:2 * HALO + TB, :] = jnp.where(keep_next, xn_ref[0].astype(F32), 0.0)
    left = CONV_W // 2
    u = jnp.broadcast_to(cb_ref[...], (TB, LRU_W))
    for tap in range(CONV_W):
        u = u + xe_scr[pl.ds(HALO - left + tap, TB), :] * cw_ref[tap:tap + 1, :]

    g = jnp.dot(u.astype(BF16), wg_ref[0], preferred_element_type=F32) + bg_ref[0, 0:1, :]
    r = _sigmoid(g[:, 0:LRU_W])
    ig = _sigmoid(g[:, LRU_W:2 * LRU_W])
    neg_lam = -lam_ref[0, 0:1, :]
    softplus = jnp.maximum(neg_lam, 0.0) + jnp.log1p(jnp.exp(-jnp.abs(neg_lam)))
    a = jnp.exp(-LRU_C * r * softplus)
    bb = jnp.sqrt(1.0 - a * a) * (ig * u)
    def scan(forward):
        ngroups = TB // SCAN_GROUP
        aa = a.reshape(ngroups, SCAN_GROUP, LRU_W)
        hh = bb.reshape(ngroups, SCAN_GROUP, LRU_W)
        rg = lax.broadcasted_iota(jnp.int32, (1, SCAN_GROUP, 1), 1)
        k = 1
        while k < SCAN_GROUP:
            shift = k if forward else SCAN_GROUP - k
            valid = (rg >= k) if forward else (rg < SCAN_GROUP - k)
            a_sh = jnp.where(valid, pltpu.roll(aa, shift, 1), 1.0)
            h_sh = jnp.where(valid, pltpu.roll(hh, shift, 1), 0.0)
            hh = aa * h_sh + hh
            aa = aa * a_sh
            k *= 2
        carry = h_scr[...]
        last = SCAN_GROUP - 1 if forward else 0
        for g in (range(ngroups) if forward else range(ngroups - 1, -1, -1)):
            hg = hh[g] + aa[g] * carry
            o_ref[0, 0, g * SCAN_GROUP:(g + 1) * SCAN_GROUP, :] = hg
            carry = hg[last:last + 1, :]
        h_scr[...] = carry

    @pl.when(d == 0)
    def _():
        scan(True)

    @pl.when(d == 1)
    def _():
        scan(False)


def _lru(p, conv_w, conv_b, wg, bg, lam, nctx):
    b, lt, _ = p.shape
    nblk = lt // TB
    hpb = TB // HALO
    nh = lt // HALO

    def blk(d, j):
        return _scan_block(j, d, nctx, nblk)

    return pl.pallas_call(
        functools.partial(_lru_kernel, nctx=nctx, nblk=nblk),
        grid=(b, 2, nblk),
        in_specs=[pl.BlockSpec((1, TB, LRU_W), lambda bi, d, j: (bi, blk(d, j), 4)),
                  pl.BlockSpec((1, HALO, LRU_W), lambda bi, d, j: (bi, jnp.maximum(blk(d, j) * hpb - 1, 0), 4)),
                  pl.BlockSpec((1, HALO, LRU_W), lambda bi, d, j: (bi, jnp.minimum((blk(d, j) + 1) * hpb, nh - 1), 4)),
                  _const_spec((8, LRU_W)), _const_spec((1, LRU_W)),
                  pl.BlockSpec((1, LRU_W, 2 * LRU_W), lambda bi, d, j: (d, 0, 0)),
                  pl.BlockSpec((1, 8, 2 * LRU_W), lambda bi, d, j: (d, 0, 0)),
                  pl.BlockSpec((1, 8, LRU_W), lambda bi, d, j: (d, 0, 0))],
        out_specs=pl.BlockSpec((1, 1, TB, LRU_W), lambda bi, d, j: (d, bi, blk(d, j), 0)),
        out_shape=jax.ShapeDtypeStruct((2, b, lt, LRU_W), F32),
        scratch_shapes=[pltpu.VMEM((TB + 2 * HALO, LRU_W), F32), pltpu.VMEM((1, LRU_W), F32)],
        compiler_params=_cparams(("parallel", "arbitrary", "arbitrary")),
        name="rglru",
    )(p, p, p, conv_w, conv_b, wg, bg, lam)


def _gelu_tanh(t):
    return 0.5 * t * (1.0 + jnp.tanh(0.7978845608028654 * (t + 0.044715 * t * t * t)))


def _merge_kernel(x_ref, mctx_ref, mb_ref, hgf_ref, hgb_ref, hgg_ref, hgn_ref, hm_ref, mlac_ref, mlax_ref,
                  lf_ref, lb_ref, ly_ref, g1_ref, g2_ref, g3_ref, wbr_ref, wout_ref, o_ref, *, ctx_len):
    i = pl.program_id(1)
    nctx = ctx_len // TB
    o_mla = jnp.where(i < nctx, mlac_ref[0], mlax_ref[0])
    o = hgf_ref[0, 0] + hgb_ref[0, 0]
    ms = jnp.dot((o * o).astype(BF16), hm_ref[...], preferred_element_type=F32)
    o_hg = o * lax.rsqrt(ms + EPS) * hgn_ref[...] * _sigmoid(hgg_ref[0].astype(F32))
    o_lru = (lf_ref[0, 0] + lb_ref[0, 0]) * _gelu_tanh(ly_ref[0].astype(F32))
    y = (_sigmoid(g1_ref[0].astype(F32)) * jnp.dot(o_hg.astype(BF16), wbr_ref[0], preferred_element_type=F32)
         + _sigmoid(g2_ref[0].astype(F32)) * jnp.dot(o_mla, wbr_ref[1], preferred_element_type=F32)
         + _sigmoid(g3_ref[0].astype(F32)) * jnp.dot(o_lru.astype(BF16), wbr_ref[2], preferred_element_type=F32))
    gate = _row_mods(i, TB, ctx_len, mctx_ref, mb_ref, 2)
    o_ref[0] = x_ref[0] + gate * jnp.dot(y.astype(BF16), wout_ref[...], preferred_element_type=F32)


def _merge(xs, mctx, mb, o_hg, p, hg_gain, head_mean, o_mla_c, o_mla_x, h_lru, w_br, w_out, ctx_len):
    b, lt, _ = xs.shape
    nctx = ctx_len // TB

    def pcol(width, c):
        return pl.BlockSpec((1, TB, width), lambda bi, i: (bi, i, c))

    def dirspec(d):
        return pl.BlockSpec((1, 1, TB, 512), lambda bi, i: (d, bi, i, 0))

    return pl.pallas_call(
        functools.partial(_merge_kernel, ctx_len=ctx_len),
        grid=(b, lt // TB),
        in_specs=[pl.BlockSpec((1, TB, D_MODEL), lambda bi, i: (bi, i, 0)),
                  _const_spec((6, D_MODEL)),
                  pl.BlockSpec((1, 6, D_MODEL), lambda bi, i: (bi, 0, 0)),
                  dirspec(0), dirspec(0), pcol(512, 2),
                  _const_spec((1, HG_W)), _const_spec((HG_W, HG_W)),
                  pl.BlockSpec((1, TB, MLA_W), lambda bi, i: (bi, jnp.minimum(i, nctx - 1), 0)),
                  pl.BlockSpec((1, TB, MLA_W), lambda bi, i: (bi, jnp.maximum(i - nctx, 0), 0)),
                  dirspec(0), dirspec(1), pcol(512, 5),
                  pcol(D_MODEL, 3), pcol(D_MODEL, 4), pcol(D_MODEL, 5),
                  _const_spec((3, 512, D_MODEL)), _const_spec((D_MODEL, D_MODEL))],
        out_specs=pl.BlockSpec((1, TB, D_MODEL), lambda bi, i: (bi, i, 0)),
        out_shape=jax.ShapeDtypeStruct((b, lt, D_MODEL), F32),
        compiler_params=_cparams(("parallel", "arbitrary")),
        name="merge",
    )(xs, mctx, mb, o_hg[0], o_hg[1], p, hg_gain, head_mean, o_mla_c, o_mla_x, h_lru, h_lru, p, p, p, p, w_br, w_out)


def _first_row_of_max(vals, row, valid):
    masked = jnp.where(valid, vals, -jnp.inf)
    m = jnp.max(masked, axis=0, keepdims=True)
    idx = jnp.min(jnp.where(masked == m, row, ROUTER_PAD), axis=0, keepdims=True)
    return m, idx


def _router(h, wr_hi, wr_lo, br):
    hi = h.astype(BF16)
    lo = (h - hi.astype(F32)).astype(BF16)
    logits = (jnp.dot(hi, wr_hi, preferred_element_type=F32) + jnp.dot(lo, wr_hi, preferred_element_type=F32)
              + jnp.dot(hi, wr_lo, preferred_element_type=F32))
    biased = (logits + br).T
    logits = logits.T
    row = lax.broadcasted_iota(jnp.int32, logits.shape, 0)
    is_group = jnp.logical_and(row >= N_EXPERTS, row < N_EXPERTS + N_GROUPS)
    _, g_row = _first_row_of_max(biased, row, is_group)
    g_max, _ = _first_row_of_max(logits, row, is_group)
    g_exp = jnp.where(is_group, jnp.exp(logits - g_max), 0.0)
    g_sel_logit = jnp.sum(jnp.where(row == g_row, logits, 0.0), axis=0, keepdims=True)
    p_g = jnp.exp(g_sel_logit - g_max) / jnp.sum(g_exp, axis=0, keepdims=True)
    in_group = jnp.right_shift(row, 2) == (g_row - N_EXPERTS)
    _, i1 = _first_row_of_max(biased, row, in_group)
    _, i2 = _first_row_of_max(biased, row, jnp.logical_and(in_group, row != i1))
    l1 = jnp.sum(jnp.where(row == i1, logits, 0.0), axis=0, keepdims=True)
    l2 = jnp.sum(jnp.where(row == i2, logits, 0.0), axis=0, keepdims=True)
    lm = jnp.maximum(l1, l2)
    e1 = jnp.exp(l1 - lm)
    e2 = jnp.exp(l2 - lm)
    inv = p_g / (e1 + e2)
    comb_t = jnp.where(row == i1, e1 * inv, 0.0) + jnp.where(row == i2, e2 * inv, 0.0)
    return comb_t.T


def _moe_kernel(x_ref, mctx_ref, mb_ref, gain_ref, wrh_ref, wrl_ref, br_ref, w1_ref, w3_ref, w2_ref, o_ref, *,
                ctx_len, tm, tile0):
    i = pl.program_id(1) + tile0
    epb = N_EXPERTS // EXPERT_STEPS
    nrows = min(MOE_RB, tm)
    for rb in range(tm // nrows):
        rows = slice(rb * nrows, (rb + 1) * nrows)
        tile = i * (tm // nrows) + rb
        shift = _row_mods(tile, nrows, ctx_len, mctx_ref, mb_ref, 3)
        scale = _row_mods(tile, nrows, ctx_len, mctx_ref, mb_ref, 4)
        gate = _row_mods(tile, nrows, ctx_len, mctx_ref, mb_ref, 5)
        x = x_ref[0, rows, :]
        h = _norm_modulate(x, gain_ref[...], shift, scale)
        comb = _router(h, wrh_ref[...], wrl_ref[...], br_ref[...])
        hb = h.astype(BF16)
        acc = None
        for es in range(EXPERT_STEPS):
            parts = []
            for e in range(es * epb, (es + 1) * epb):
                h1 = jnp.dot(hb, w1_ref[e], preferred_element_type=F32)
                h3 = jnp.dot(hb, w3_ref[e], preferred_element_type=F32)
                parts.append(h1 * _sigmoid(h1) * h3 * comb[:, e:e + 1])
            y = jnp.dot(jnp.concatenate(parts, axis=-1).astype(BF16), w2_ref[es], preferred_element_type=F32)
            acc = y if acc is None else acc + y
        o_ref[0, rows, :] = x + gate * acc


def _moe(xs, mctx, mb, gain, wr_hi, wr_lo, br, w1, w3, w2, ctx_len, latent_only):
    b, lt, _ = xs.shape
    epb = N_EXPERTS // EXPERT_STEPS
    if latent_only:
        tm, tile0, rows = TB, ctx_len // TB, lt - ctx_len
    else:
        tm, tile0, rows = (MOE_TM if lt % MOE_TM == 0 else TB), 0, lt
    return pl.pallas_call(
        functools.partial(_moe_kernel, ctx_len=ctx_len, tm=tm, tile0=tile0),
        grid=(b, rows // tm),
        in_specs=[pl.BlockSpec((1, tm, D_MODEL), lambda bi, i: (bi, i + tile0, 0)),
                  _const_spec((6, D_MODEL)),
                  pl.BlockSpec((1, 6, D_MODEL), lambda bi, i: (bi, 0, 0)),
                  _const_spec((1, D_MODEL)),
                  _const_spec((D_MODEL, ROUTER_PAD)), _const_spec((D_MODEL, ROUTER_PAD)), _const_spec((1, ROUTER_PAD)),
                  _const_spec((N_EXPERTS, D_MODEL, D_EXPERT)), _const_spec((N_EXPERTS, D_MODEL, D_EXPERT)),
                  _const_spec((EXPERT_STEPS, epb * D_EXPERT, D_MODEL))],
        out_specs=pl.BlockSpec((1, tm, D_MODEL), lambda bi, i: (bi, i, 0)),
        out_shape=jax.ShapeDtypeStruct((b, rows, D_MODEL), F32),
        compiler_params=_cparams(("parallel", "arbitrary")),
        name="moe",
    )(xs, mctx, mb, gain, wr_hi, wr_lo, br, w1, w3, w2)


def _rope_tables(ctx_len, seq):
    half = MLA_ROPE // 2
    rows = seq // GRID_W
    pos_row = np.repeat(np.arange(rows, dtype=np.float32), GRID_W)
    pos_col = np.tile(np.arange(GRID_W, dtype=np.float32), rows)
    inv = (ROPE_BASE ** (-np.arange(0, half, 2, dtype=np.float32) / half)).astype(np.float32)
    ang = np.concatenate([pos_row[:, None] * inv, pos_col[:, None] * inv], axis=-1)
    cos, sin = np.cos(ang), np.sin(ang)
    cos_t = np.zeros((ctx_len + seq, HEAD_PAD), np.float32)
    sin_t = np.zeros((ctx_len + seq, HEAD_PAD), np.float32)
    cos_t[:, :MLA_NOPE] = 1.0
    cos_t[:ctx_len, MLA_NOPE:MLA_QK] = 1.0
    cos_t[ctx_len:, MLA_NOPE:MLA_NOPE + half] = cos
    cos_t[ctx_len:, MLA_NOPE + half:MLA_QK] = cos
    sin_t[ctx_len:, MLA_NOPE:MLA_NOPE + half] = -sin
    sin_t[ctx_len:, MLA_NOPE + half:MLA_QK] = sin
    return cos_t, sin_t


def _rope_key_selectors():
    half = MLA_ROPE // 2
    kr0 = MLA_Q_RANK + MLA_KV_RANK
    sk1 = np.zeros((512, MLA_HEADS * HEAD_PAD), np.float32)
    sk2 = np.zeros((512, MLA_HEADS * HEAD_PAD), np.float32)
    for h in range(MLA_HEADS):
        for i in range(MLA_ROPE):
            sk1[kr0 + i, h * HEAD_PAD + MLA_NOPE + i] = 1.0
            sk2[kr0 + (i + half) % MLA_ROPE, h * HEAD_PAD + MLA_NOPE + i] = 1.0
    return sk1.astype(BF16), sk2.astype(BF16)


def _np_block_diag(block, n):
    a, bb = block.shape
    out = np.zeros((n * a, n * bb), np.float32)
    for i in range(n):
        out[i * a:(i + 1) * a, i * bb:(i + 1) * bb] = block
    return out


def _swap_rope_halves(t):
    half = MLA_ROPE // 2
    return jnp.concatenate([jnp.zeros_like(t[..., :MLA_NOPE]), t[..., MLA_NOPE + half:], t[..., MLA_NOPE:MLA_NOPE + half]],
                           axis=-1)


def _pad_heads(t):
    pad = [(0, 0)] * (t.ndim - 1) + [(0, HEAD_PAD - t.shape[-1])]
    t = jnp.pad(t, pad)
    return t.reshape(t.shape[:-2] + (t.shape[-2] * HEAD_PAD,))


def _mla_weights(w_uq, w_ukv, gq, gk):
    wq = w_uq.reshape(MLA_Q_RANK, MLA_HEADS, MLA_QK)
    wq1 = _pad_heads(wq).astype(BF16)
    wq2 = _pad_heads(_swap_rope_halves(wq)).astype(BF16)
    wkv = w_ukv.reshape(MLA_KV_RANK, MLA_HEADS, MLA_NOPE + MLA_V)
    wk1 = _pad_heads(wkv[..., :MLA_NOPE]).astype(BF16)
    wv = _pad_heads(wkv[..., MLA_NOPE:]).astype(BF16)

    def gains(g):
        g1 = jnp.pad(g, (0, HEAD_PAD - MLA_QK))
        g2 = jnp.pad(_swap_rope_halves(g), (0, HEAD_PAD - MLA_QK))
        return jnp.stack([g1, g2], axis=0)

    return wq1, wq2, wk1, wv, gains(gq), gains(gk)


def _block_diag(w):
    n, a, bb = w.shape
    eye = jnp.eye(n, dtype=w.dtype)
    return (eye[:, None, :, None] * w[:, :, None, :]).reshape(n * a, n * bb)


def _pad_rows(t, rows=8):
    return jnp.pad(t, ((0, rows - t.shape[0]), (0, 0)))


def kernel(x, c, ctx, c_ctx, w_mod, b_mod, norm_mix, norm_ffn, w_in, hg_lb, hg_norm, mla_q_norm, mla_kv_norm, mla_w_uq, mla_w_ukv, mla_qk_gain_q, mla_qk_gain_k, lru_conv_w, lru_conv_b, lru_wa, lru_ba, lru_wx, lru_bx, lru_lambda, w_br_hg, w_br_mla, w_br_lru, w_out, moe_w_rg, moe_b_rg, moe_w_re, moe_b_re, moe_w1, moe_w3, moe_w2):
    bsz, seq, _ = x.shape
    ctx_len = ctx.shape[1]
    depth = w_in.shape[0]
    assert seq % TB == 0 and ctx_len % TB == 0 and seq % GRID_W == 0 and bsz < 8
    nctx = ctx_len // TB

    xs = jnp.concatenate([ctx, x], axis=1)
    cos_t, sin_t = _rope_tables(ctx_len, seq)
    cvec = jnp.zeros((8, D_MODEL), F32).at[:bsz].set(c).at[bsz].set(c_ctx)

    lb_cs = jnp.cumsum(jax.nn.softmax(hg_lb.astype(F32), axis=0), axis=0)
    lb_all = lb_cs - lb_cs[0:1]
    tri_lo = _np_block_diag(np.tril(np.ones((HG_CHUNK, HG_CHUNK), np.float32)), TB // HG_CHUNK)
    tri = np.stack([tri_lo, tri_lo.T], axis=0).astype(BF16)
    head_ones = _np_block_diag(np.ones((HG_DK, HG_DK), np.float32), HG_HEADS)
    head_mean = (head_ones / HG_DK).astype(BF16)
    sk1, sk2 = _rope_key_selectors()

    offs = np.cumsum((HG_W,) * 5 + (MLA_Q_RANK, MLA_KV_RANK, MLA_ROPE, LRU_W, LRU_W) + (D_MODEL,) * 3)[:-1].tolist()

    for l in range(depth):
        mods = _modulation(cvec, w_mod[l], b_mod[l]).reshape(8, 6, D_MODEL)
        mctx, mb = mods[bsz], mods[:bsz]

        q_, ff, fb, i_, g_, dq, dkv, kr, lx, ly, g1, g2, g3 = jnp.split(w_in[l].astype(BF16), offs, axis=-1)
        kr_pad = jnp.zeros((D_MODEL, 512 - MLA_Q_RANK - MLA_KV_RANK - MLA_ROPE), BF16)
        w_all = jnp.concatenate([ff, fb, q_, i_, g_, dq, dkv, kr, kr_pad, lx, ly, g1, g2, g3], axis=-1)
        f, p = _inproj(xs, mctx, mb, norm_mix[l][None, :], w_all, ctx_len)

        lb = lb_all[l]
        lbc = jnp.stack([jnp.log(lb), jnp.log1p(-lb), 1.0 - lb] + [jnp.zeros_like(lb)] * 5, axis=1)
        o_hg = [_hgrn(p, f, tri, lbc, head_ones, nctx, reverse) for reverse in (False, True)]

        wq1, wq2, wk1, wv, gq, gk = _mla_weights(mla_w_uq[l], mla_w_ukv[l], mla_qk_gain_q[l], mla_qk_gain_k[l])
        qt, kh, vt, ksq = _mla_prep(p, cos_t, sin_t, mla_q_norm[l][None, :], mla_kv_norm[l][None, :], gq, gk,
                                    wq1, wq2, wk1, sk1, sk2, wv, nctx)
        kmax = jnp.sqrt(jnp.max(ksq, axis=(1, 3)))
        tq = next(t for t in (512, 256) if seq % t == 0)
        o_mla_x = _flash(kmax, qt, kh, vt, seq, ctx_len + seq, tq, 0, "mla_attention")
        if l < depth - 1:
            o_mla_c = _flash(kmax, qt, kh, vt, ctx_len, ctx_len, TB, seq, "mla_attention_ctx")
        else:
            o_mla_c = o_mla_x[:, :ctx_len]

        wg = jnp.stack([jnp.concatenate([_block_diag(lru_wa[l, d]), _block_diag(lru_wx[l, d])], axis=-1)
                        for d in range(2)], axis=0).astype(BF16)
        bg = jnp.concatenate([lru_ba[l], lru_bx[l]], axis=-1)[:, None, :] * jnp.ones((1, 8, 1), F32)
        lam = lru_lambda[l][:, None, :] * jnp.ones((1, 8, 1), F32)
        h_lru = _lru(p, _pad_rows(lru_conv_w[l]), lru_conv_b[l][None, :], wg, bg, lam, nctx)

        w_br = jnp.stack([w_br_hg[l], w_br_mla[l], w_br_lru[l]], axis=0).astype(BF16)
        hg_gain = jnp.tile(hg_norm[l], HG_HEADS)[None, :]
        xs = _merge(xs, mctx, mb, o_hg, p, hg_gain, head_mean, o_mla_c, o_mla_x, h_lru, w_br, w_out[l].astype(BF16), ctx_len)

        wr = jnp.pad(jnp.concatenate([moe_w_re[l], moe_w_rg[l]], axis=-1), ((0, 0), (0, ROUTER_PAD - N_EXPERTS - N_GROUPS)))
        br = jnp.pad(jnp.concatenate([moe_b_re[l], moe_b_rg[l]]), (0, ROUTER_PAD - N_EXPERTS - N_GROUPS))[None, :]
        w2 = moe_w2[l].astype(BF16).reshape(EXPERT_STEPS, N_EXPERTS // EXPERT_STEPS * D_EXPERT, D_MODEL)
        wr_hi = wr.astype(BF16)
        wr_lo = (wr - wr_hi.astype(F32)).astype(BF16)
        xs = _moe(xs, mctx, mb, norm_ffn[l][None, :], wr_hi, wr_lo, br, moe_w1[l].astype(BF16), moe_w3[l].astype(BF16), w2,
                  ctx_len, latent_only=(l == depth - 1))

    return xs
```

```python
import functools

import numpy as np
import jax
import jax.numpy as jnp
from jax import lax
from jax.experimental import pallas as pl
from jax.experimental.pallas import tpu as pltpu

F32 = jnp.float32
BF16 = jnp.bfloat16
HIGHEST = lax.Precision.HIGHEST

D_MODEL = 1024
GRID_W = 64
EPS = 1e-6

HG_HEADS = 8
HG_DK = 64
HG_W = 512
HG_CHUNK = 64
HG_MID = HG_CHUNK // 2
HG_MAX_LOG_RANGE = 80.0

MLA_HEADS = 8
MLA_Q_RANK = 256
MLA_KV_RANK = 128
MLA_NOPE = 64
MLA_ROPE = 32
MLA_V = 64
MLA_QK = MLA_NOPE + MLA_ROPE
MLA_W = MLA_HEADS * MLA_V
HEAD_PAD = 128
V_ROWS = 80
SCORE_SCALE = MLA_QK ** -0.5 * 1.4426950408889634
MAX_UNSHIFTED_SCORE = 57.0
ROPE_BASE = 10000.0

LRU_W = 512
LRU_BLOCKS = 8
LRU_BD = LRU_W // LRU_BLOCKS
CONV_W = 4
LRU_C = 8.0

N_GROUPS = 4
EXP_PER_GROUP = 4
N_EXPERTS = N_GROUPS * EXP_PER_GROUP
D_EXPERT = 256
EXPERT_STEPS = 4
ROUTER_PAD = 128
MOE_TM = 768
MOE_RB = 256
INPROJ_TM = 768

TB = 256
HALO = 16
SCAN_GROUP = 8
P_WIDTH = 6144
F_WIDTH = 1024
VMEM_LIMIT = 56 * 1024 * 1024

NT_DIMS = (((1,), (1,)), ((), ()))


def _cparams(sem):
    return pltpu.CompilerParams(dimension_semantics=sem, vmem_limit_bytes=VMEM_LIMIT)


def _const_spec(shape):
    nd = len(shape)
    return pl.BlockSpec(shape, lambda *_: (0,) * nd, pipeline_mode=pl.Buffered(1))


def _sigmoid(t):
    return 0.5 * jnp.tanh(0.5 * t) + 0.5


def _scan_block(j, d, nctx, nblk):
    fwd = j
    bwd = jnp.where(j < nctx, nctx - 1 - j, nblk - 1 - (j - nctx))
    return jnp.where(d == 0, fwd, bwd)


def _mod_kernel(c_ref, w_ref, b_ref, o_ref):
    c = c_ref[...]
    s = c * _sigmoid(c)
    o_ref[...] = jnp.dot(s, w_ref[...], precision=HIGHEST, preferred_element_type=F32) + b_ref[...]


def _modulation(cvec, w_mod, b_mod):
    n = w_mod.shape[1]
    tn = 1024
    return pl.pallas_call(
        _mod_kernel,
        grid=(n // tn,),
        in_specs=[pl.BlockSpec((8, D_MODEL), lambda j: (0, 0)),
                  pl.BlockSpec((D_MODEL, tn), lambda j: (0, j)),
                  pl.BlockSpec((1, tn), lambda j: (0, j))],
        out_specs=pl.BlockSpec((8, tn), lambda j: (0, j)),
        out_shape=jax.ShapeDtypeStruct((8, n), F32),
        compiler_params=_cparams(("arbitrary",)),
        name="modulation",
    )(cvec, w_mod, b_mod.reshape(1, n))


def _row_mods(i, tm, ctx_len, mctx_ref, mb_ref, k):
    row = i * tm + lax.broadcasted_iota(jnp.int32, (tm, 1), 0)
    return jnp.where(row < ctx_len, mctx_ref[k:k + 1, :], mb_ref[0, k:k + 1, :])


def _norm_modulate(x, gain, shift, scale):
    ms = jnp.mean(x * x, axis=-1, keepdims=True)
    xn = x * lax.rsqrt(ms + EPS) * gain
    return xn * (1.0 + scale) + shift


def _inproj_kernel(x_ref, mctx_ref, mb_ref, gain_ref, w_ref, f_ref, p_ref, *, ctx_len, tm):
    i = pl.program_id(1)
    shift = _row_mods(i, tm, ctx_len, mctx_ref, mb_ref, 0)
    scale = _row_mods(i, tm, ctx_len, mctx_ref, mb_ref, 1)
    h = _norm_modulate(x_ref[0], gain_ref[...], shift, scale).astype(BF16)
    f_ref[0] = jnp.dot(h, w_ref[:, 0:F_WIDTH], preferred_element_type=F32)
    cw = 512
    for j in range(P_WIDTH // cw):
        lo = F_WIDTH + j * cw
        p_ref[0, :, j * cw:(j + 1) * cw] = jnp.dot(
            h, w_ref[:, lo:lo + cw], preferred_element_type=F32).astype(BF16)


def _inproj(xs, mctx, mb, gain, w_all, ctx_len):
    b, lt, _ = xs.shape
    tm = INPROJ_TM if lt % INPROJ_TM == 0 else TB
    return pl.pallas_call(
        functools.partial(_inproj_kernel, ctx_len=ctx_len, tm=tm),
        grid=(b, lt // tm),
        in_specs=[pl.BlockSpec((1, tm, D_MODEL), lambda bi, i: (bi, i, 0)),
                  _const_spec((6, D_MODEL)),
                  pl.BlockSpec((1, 6, D_MODEL), lambda bi, i: (bi, 0, 0)),
                  _const_spec((1, D_MODEL)),
                  _const_spec((D_MODEL, F_WIDTH + P_WIDTH))],
        out_specs=[pl.BlockSpec((1, tm, F_WIDTH), lambda bi, i: (bi, i, 0)),
                   pl.BlockSpec((1, tm, P_WIDTH), lambda bi, i: (bi, i, 0))],
        out_shape=[jax.ShapeDtypeStruct((b, lt, F_WIDTH), F32),
                   jax.ShapeDtypeStruct((b, lt, P_WIDTH), BF16)],
        compiler_params=_cparams(("parallel", "arbitrary")),
        name="inproj",
    )(xs, mctx, mb, gain, w_all)


def _hgrn_direction(q_ref, v_ref, f_ref, tri_ref, lbc_ref, bd_ref, o_ref, st_ref, c_scr, k_scr, v_scr, reverse):
    d = 1 if reverse else 0
    nch = TB // HG_CHUNK
    npair = HG_W // 128
    first, last = (HG_CHUNK - 1, 0) if reverse else (0, HG_CHUNK - 1)

    log_lb = lbc_ref[d, 0:1, :]
    log_1mlb = lbc_ref[d, 1:2, :]
    one_mlb = lbc_ref[d, 2:3, :]
    z = f_ref[0]
    q = q_ref[0].astype(F32)
    v = v_ref[0].astype(F32)
    e = jnp.exp(-jnp.abs(z))
    log_sig = jnp.minimum(z, 0.0) - jnp.log(1.0 + e)
    t = log_1mlb + log_sig
    logf = jnp.maximum(log_lb, t) + jnp.log(1.0 + jnp.exp(-jnp.abs(log_lb - t)))
    k = one_mlb * (jnp.where(z >= 0, e, 1.0) / (1.0 + e))
    hi = logf.astype(BF16)
    lo = (logf - hi.astype(F32)).astype(BF16)
    tri = tri_ref[d]
    c = jnp.dot(tri, hi, preferred_element_type=F32) + jnp.dot(tri, lo, preferred_element_type=F32)

    span = jnp.zeros((1, HG_W), F32)
    for ci in range(nch):
        r0 = ci * HG_CHUNK
        c_mid = c[r0 + HG_MID:r0 + HG_MID + 1, :]
        span = jnp.maximum(span, jnp.maximum(c[r0 + first:r0 + first + 1, :] - c_mid, c_mid - c[r0 + last:r0 + last + 1, :]))
    span = jnp.max(span)

    lane = lax.broadcasted_iota(jnp.int32, (HG_CHUNK, HG_W), 1)
    low_head = (lane & (HG_DK * 2 - 1)) < HG_DK
    ri = lax.broadcasted_iota(jnp.int32, (HG_CHUNK, 128), 0)
    si = lax.broadcasted_iota(jnp.int32, (HG_CHUNK, 128), 1) & (HG_DK - 1)
    visited = (si >= ri) if reverse else (si <= ri)
    r2 = lax.broadcasted_iota(jnp.int32, (128, 128), 0)
    l2 = lax.broadcasted_iota(jnp.int32, (128, 128), 1)
    same_head = (r2 < HG_DK) == (l2 < HG_DK)
    rowi = lax.broadcasted_iota(jnp.int32, (HG_CHUNK, 1), 0)

    def split_heads(t, transpose=False):
        top = jnp.where(low_head, t, 0.0)
        bot = jnp.where(low_head, 0.0, t)
        blocks = [jnp.concatenate([top[:, 128 * g:128 * (g + 1)], bot[:, 128 * g:128 * (g + 1)]], axis=0) for g in range(npair)]
        return [(blk.T if transpose else blk).astype(BF16) for blk in blocks]

    def exact_intra(qq, kk, vv, cc):
        c_scr[...] = cc
        k_scr[...] = kk
        v_scr[...] = vv
        bd = bd_ref[...]

        def key_row(s, acc):
            c_s = c_scr[pl.ds(s, 1), :]
            w = qq * jnp.exp(jnp.minimum(cc - c_s, 0.0)) * k_scr[pl.ds(s, 1), :]
            ws = jnp.dot(w, bd, precision=HIGHEST, preferred_element_type=F32)
            ok = (rowi <= s) if reverse else (rowi >= s)
            return acc + jnp.where(ok, ws, 0.0) * v_scr[pl.ds(s, 1), :]

        return lax.fori_loop(0, HG_CHUNK, key_row, jnp.zeros((HG_CHUNK, HG_W), F32))

    def chunk(ci, exact):
        r0 = ci * HG_CHUNK
        rows = slice(r0, r0 + HG_CHUNK)
        cc, qq, kk, vv = c[rows], q[rows], k[rows], v[rows]
        c_mid = cc[HG_MID:HG_MID + 1, :]
        c_end = cc[last:last + 1, :]
        qs = (qq * jnp.exp(cc)).astype(BF16)
        ks = (kk * jnp.exp(c_end - cc)).astype(BF16)
        dec = jnp.exp(c_end)
        if exact:
            intra = exact_intra(qq, kk, vv, cc)
        else:
            qd = (qq * jnp.exp(cc - c_mid)).astype(BF16)
            kd_blocks = split_heads(kk * jnp.exp(c_mid - cc), transpose=True)
            v_blocks = split_heads(vv)
        pieces = []
        for g in range(npair):
            ls = slice(128 * g, 128 * (g + 1))
            st = st_ref[d, g]
            o_g = jnp.dot(qs[:, ls], st.T.astype(BF16), preferred_element_type=F32)
            if not exact:
                a = jnp.dot(qd[:, ls], kd_blocks[g], preferred_element_type=F32)
                a = jnp.where(visited, a, 0.0).astype(BF16)
                o_g = o_g + jnp.dot(a, v_blocks[g], preferred_element_type=F32)
            vt = vv[:, ls].T.astype(BF16)
            kv = jnp.dot(vt, ks[:, ls], preferred_element_type=F32)
            st_ref[d, g] = st * dec[:, ls] + jnp.where(same_head, kv, 0.0)
            pieces.append(o_g)
        o = jnp.concatenate(pieces, axis=-1)
        o_ref[0, 0, rows, :] = o + intra if exact else o

    order = list(range(nch - 1, -1, -1) if reverse else range(nch))
    return span, lambda n, exact: chunk(order[n], exact)


def _hgrn_kernel(qf_ref, vf_ref, ff_ref, qb_ref, vb_ref, fb_ref, tri_ref, lbc_ref, bd_ref, of_ref, ob_ref,
                 st_ref, c_scr, k_scr, v_scr):
    @pl.when(pl.program_id(1) == 0)
    def _():
        st_ref[...] = jnp.zeros_like(st_ref)

    span_f, chunk_f = _hgrn_direction(qf_ref, vf_ref, ff_ref, tri_ref, lbc_ref, bd_ref, of_ref, st_ref,
                                      c_scr, k_scr, v_scr, False)
    span_b, chunk_b = _hgrn_direction(qb_ref, vb_ref, fb_ref, tri_ref, lbc_ref, bd_ref, ob_ref, st_ref,
                                      c_scr, k_scr, v_scr, True)
    span = jnp.maximum(span_f, span_b)

    def run(exact):
        for n in range(TB // HG_CHUNK):
            chunk_f(n, exact)
            chunk_b(n, exact)

    @pl.when(span <= HG_MAX_LOG_RANGE)
    def _matmul_form():
        run(False)

    @pl.when(span > HG_MAX_LOG_RANGE)
    def _exact_form():
        run(True)


def _hgrn(p, f, tri, lbc, bd, nctx):
    b, lt, _ = p.shape
    nblk = lt // TB

    def spec(d, col):
        return pl.BlockSpec((1, TB, HG_W), lambda bi, j: (bi, _scan_block(j, d, nctx, nblk), col))

    def out_spec(d):
        return pl.BlockSpec((1, 1, TB, HG_W), lambda bi, j: (0, bi, _scan_block(j, d, nctx, nblk), 0))

    out = jax.ShapeDtypeStruct((1, b, lt, HG_W), F32)
    return pl.pallas_call(
        _hgrn_kernel,
        grid=(b, nblk),
        in_specs=[spec(0, 0), spec(0, 1), spec(0, 0),
                  spec(1, 0), spec(1, 1), spec(1, 1),
                  _const_spec((2, TB, TB)), _const_spec((2, 8, HG_W)), _const_spec((HG_W, HG_W))],
        out_specs=[out_spec(0), out_spec(1)],
        out_shape=[out, out],
        scratch_shapes=[pltpu.VMEM((2, HG_W // 128, 128, 128), F32),
                        pltpu.VMEM((HG_CHUNK, HG_W), F32),
                        pltpu.VMEM((HG_CHUNK, HG_W), F32),
                        pltpu.VMEM((HG_CHUNK, HG_W), F32)],
        compiler_params=_cparams(("parallel", "arbitrary")),
        name="hgrn2",
    )(p, p, f, p, p, f, tri, lbc, bd)


def _rms_rows(t, width):
    return lax.rsqrt(jnp.sum(t * t, axis=-1, keepdims=True) * (1.0 / width) + EPS)


def _mla_prep_kernel(pm_ref, cos_ref, sin_ref, qn_ref, kvn_ref, gq_ref, gk_ref,
                     wq1_ref, wq2_ref, wk1_ref, sk1_ref, sk2_ref, wv_ref,
                     qt_ref, k_ref, vt_ref, kn_ref):
    pm = pm_ref[0]
    dq = pm[:, 0:MLA_Q_RANK].astype(F32)
    dkv = pm[:, MLA_Q_RANK:MLA_Q_RANK + MLA_KV_RANK].astype(F32)
    dqn = (dq * _rms_rows(dq, MLA_Q_RANK) * qn_ref[...]).astype(BF16)
    dkvn = (dkv * _rms_rows(dkv, MLA_KV_RANK) * kvn_ref[...]).astype(BF16)
    q1 = jnp.dot(dqn, wq1_ref[...], preferred_element_type=F32)
    q2 = jnp.dot(dqn, wq2_ref[...], preferred_element_type=F32)
    k1 = (jnp.dot(dkvn, wk1_ref[...], preferred_element_type=F32)
          + jnp.dot(pm, sk1_ref[...], preferred_element_type=F32))
    k2 = jnp.dot(pm, sk2_ref[...], preferred_element_type=F32)
    vv = jnp.dot(dkvn, wv_ref[...], preferred_element_type=F32)
    cos = cos_ref[...]
    sin = sin_ref[...]
    gq_c = gq_ref[0:1, :] * cos
    gq_s = gq_ref[1:2, :] * sin
    gk_c = gk_ref[0:1, :] * cos
    gk_s = gk_ref[1:2, :] * sin
    vrow = lax.broadcasted_iota(jnp.int32, (V_ROWS, TB), 0)
    kn = []
    for h in range(MLA_HEADS):
        sl = slice(HEAD_PAD * h, HEAD_PAD * (h + 1))
        qh = q1[:, sl]
        qo = (qh * gq_c + q2[:, sl] * gq_s) * (_rms_rows(qh, MLA_QK) * SCORE_SCALE)
        qt_ref[0, h] = qo.T.astype(BF16)
        kh = k1[:, sl]
        ko = (kh * gk_c + k2[:, sl] * gk_s) * _rms_rows(kh, MLA_QK)
        k_ref[0, h] = ko.astype(BF16)
        ksq = jnp.max(jnp.sum(ko * ko, axis=-1, keepdims=True), axis=0, keepdims=True)
        kn.append(jnp.broadcast_to(ksq, (1, HEAD_PAD)))
        vt = vv[:, sl].T[0:V_ROWS, :]
        vt_ref[0, h] = jnp.where(vrow == MLA_V, 1.0, vt).astype(BF16)
    kn_ref[0, 0] = jnp.concatenate(kn, axis=0)


def _mla_prep(p, cos_t, sin_t, qn, kvn, gq, gk, wq1, wq2, wk1, sk1, sk2, wv, nctx):
    b, lt, _ = p.shape
    hw = MLA_HEADS * HEAD_PAD
    nblk = lt // TB
    return pl.pallas_call(
        _mla_prep_kernel,
        grid=(b, lt // TB),
        in_specs=[pl.BlockSpec((1, TB, 512), lambda bi, i: (bi, i, 3)),
                  pl.BlockSpec((TB, HEAD_PAD), lambda bi, i: (i, 0)),
                  pl.BlockSpec((TB, HEAD_PAD), lambda bi, i: (i, 0)),
                  _const_spec((1, MLA_Q_RANK)), _const_spec((1, MLA_KV_RANK)),
                  _const_spec((2, HEAD_PAD)), _const_spec((2, HEAD_PAD)),
                  _const_spec((MLA_Q_RANK, hw)), _const_spec((MLA_Q_RANK, hw)),
                  _const_spec((MLA_KV_RANK, hw)), _const_spec((512, hw)), _const_spec((512, hw)),
                  _const_spec((MLA_KV_RANK, hw))],
        out_specs=[pl.BlockSpec((1, MLA_HEADS, HEAD_PAD, TB), lambda bi, i: (bi, 0, 0, (i + nblk - nctx) % nblk)),
                   pl.BlockSpec((1, MLA_HEADS, TB, HEAD_PAD), lambda bi, i: (bi, 0, i, 0)),
                   pl.BlockSpec((1, MLA_HEADS, V_ROWS, TB), lambda bi, i: (bi, 0, 0, i)),
                   pl.BlockSpec((1, 1, MLA_HEADS, HEAD_PAD), lambda bi, i: (bi, i, 0, 0))],
        out_shape=[jax.ShapeDtypeStruct((b, MLA_HEADS, HEAD_PAD, lt), BF16),
                   jax.ShapeDtypeStruct((b, MLA_HEADS, lt, HEAD_PAD), BF16),
                   jax.ShapeDtypeStruct((b, MLA_HEADS, V_ROWS, lt), BF16),
                   jax.ShapeDtypeStruct((b, lt // TB, MLA_HEADS, HEAD_PAD), F32)],
        compiler_params=_cparams(("parallel", "arbitrary")),
        name="mla_prep",
    )(p, cos_t, sin_t, qn, kvn, gq, gk, wq1, wq2, wk1, sk1, sk2, wv)


def _attend_bounded(qts, k_ref, vt_ref, s_scr, nkeys, tk):
    n = nkeys // tk

    def scores(kb, slot):
        for hh in range(2):
            s_scr[slot, hh, 0:tk, :] = jnp.dot(k_ref[0, hh, kb * tk:(kb + 1) * tk, :], qts[hh], preferred_element_type=F32)

    def accumulate(kb, slot, accs):
        out = []
        for hh in range(2):
            p = jnp.exp2(s_scr[slot, hh, 0:tk, :]).astype(BF16)
            pv = jnp.dot(vt_ref[0, hh, :, kb * tk:(kb + 1) * tk], p, preferred_element_type=F32)
            out.append(pv if accs is None else accs[hh] + pv)
        return out

    scores(0, 0)
    accs = None
    for kb in range(n):
        if kb + 1 < n:
            scores(kb + 1, (kb + 1) % 2)
        accs = accumulate(kb, kb % 2, accs)
    return accs


def _attend_online(qts, k_ref, vt_ref, nkeys, tk):
    def kv_step(kb, carry):
        r0 = pl.multiple_of(kb * tk, tk)
        out = []
        for hh in range(2):
            m, acc = carry[hh]
            s = jnp.dot(k_ref[0, hh, pl.ds(r0, tk), :], qts[hh], preferred_element_type=F32)
            m_new = jnp.maximum(m, jnp.max(s, axis=0, keepdims=True))
            p = jnp.exp2(s - m_new).astype(BF16)
            acc = jnp.exp2(m - m_new) * acc + jnp.dot(vt_ref[0, hh, :, pl.ds(r0, tk)], p, preferred_element_type=F32)
            out.append((m_new, acc))
        return tuple(out)

    tq = qts[0].shape[1]
    init = (jnp.full((1, tq), -jnp.inf, F32), jnp.zeros((V_ROWS, tq), F32))
    res = lax.fori_loop(0, nkeys // tk, kv_step, (init, init))
    return res[0][1], res[1][1]


def _flash_kernel(kmax_ref, qt_ref, k_ref, vt_ref, o_ref, s_scr, *, nkeys, tk):
    bi = pl.program_id(0)
    g = pl.program_id(1)

    qts = [qt_ref[0, hh] for hh in range(2)]
    worst = 0.0
    for hh in range(2):
        qt = qts[hh].astype(F32)
        qnorm = jnp.sqrt(jnp.max(jnp.sum(qt * qt, axis=0, keepdims=True)))
        worst = jnp.maximum(worst, qnorm * kmax_ref[bi, 2 * g + hh])

    def finish(accs):
        outs = [(acc[0:MLA_V, :] / acc[MLA_V:MLA_V + 1, :]).T for acc in accs]
        o_ref[0] = jnp.concatenate(outs, axis=-1).astype(BF16)

    @pl.when(worst <= MAX_UNSHIFTED_SCORE)
    def _():
        finish(_attend_bounded(qts, k_ref, vt_ref, s_scr, nkeys, tk))

    @pl.when(worst > MAX_UNSHIFTED_SCORE)
    def _():
        finish(_attend_online(qts, k_ref, vt_ref, nkeys, tk))


def _flash(kmax, qt, k, vt, nq, nkeys, tq, q_col0, name):
    b = k.shape[0]
    tk = next(t for t in (768, 512, 256) if nkeys % t == 0)
    cb0 = q_col0 // tq
    return pl.pallas_call(
        functools.partial(_flash_kernel, nkeys=nkeys, tk=tk),
        grid=(b, MLA_HEADS // 2, nq // tq),
        in_specs=[pl.BlockSpec(memory_space=pltpu.SMEM),
                  pl.BlockSpec((1, 2, HEAD_PAD, tq), lambda bi, g, i: (bi, g, 0, cb0 + i)),
                  pl.BlockSpec((1, 2, nkeys, HEAD_PAD), lambda bi, g, i: (bi, g, 0, 0)),
                  pl.BlockSpec((1, 2, V_ROWS, nkeys), lambda bi, g, i: (bi, g, 0, 0))],
        out_specs=pl.BlockSpec((1, tq, HEAD_PAD), lambda bi, g, i: (bi, i, g)),
        out_shape=jax.ShapeDtypeStruct((b, nq, MLA_W), BF16),
        scratch_shapes=[pltpu.VMEM((2, 2, tk, tq), F32)],
        compiler_params=_cparams(("parallel", "parallel", "arbitrary")),
        name=name,
    )(kmax, qt, k, vt)


def _lru_kernel(x_ref, xp_ref, xn_ref, cw_ref, cb_ref, wg_ref, bg_ref, lam_ref, o_ref, xe_scr, h_scr, *, nctx, nblk):
    d = pl.program_id(1)
    j = pl.program_id(2)
    blk = _scan_block(j, d, nctx, nblk)

    @pl.when(j == 0)
    def _():
        h_scr[...] = jnp.zeros_like(h_scr)

    keep_prev = jnp.logical_and(blk != 0, blk != nctx)
    keep_next = jnp.logical_and(blk != nctx - 1, blk != nblk - 1)
    xe_scr[0:HALO, :] = jnp.where(keep_prev, xp_ref[0].astype(F32), 0.0)
    xe_scr[HALO:HALO + TB, :] = x_ref[0].astype(F32)
    xe_scr[HALO + TB:2 * HALO + TB, :] = jnp.where(keep_next, xn_ref[0].astype(F32), 0.0)
    left = CONV_W // 2
    u = jnp.broadcast_to(cb_ref[...], (TB, LRU_W))
    for tap in range(CONV_W):
        u = u + xe_scr[pl.ds(HALO - left + tap, TB), :] * cw_ref[tap:tap + 1, :]

    g = jnp.dot(u.astype(BF16), wg_ref[0], preferred_element_type=F32) + bg_ref[0, 0:1, :]
    r = _sigmoid(g[:, 0:LRU_W])
    ig = _sigmoid(g[:, LRU_W:2 * LRU_W])
    neg_lam = -lam_ref[0, 0:1, :]
    softplus = jnp.maximum(neg_lam, 0.0) + jnp.log1p(jnp.exp(-jnp.abs(neg_lam)))
    a = jnp.exp(-LRU_C * r * softplus)
    bb = jnp.sqrt(1.0 - a * a) * (ig * u)
    def scan(forward):
        ngroups = TB // SCAN_GROUP
        aa = a.reshape(ngroups, SCAN_GROUP, LRU_W)
        hh = bb.reshape(ngroups, SCAN_GROUP, LRU_W)
        rg = lax.broadcasted_iota(jnp.int32, (1, SCAN_GROUP, 1), 1)
        k = 1
        while k < SCAN_GROUP:
            shift = k if forward else SCAN_GROUP - k
            valid = (rg >= k) if forward else (rg < SCAN_GROUP - k)
            a_sh = jnp.where(valid, pltpu.roll(aa, shift, 1), 1.0)
            h_sh = jnp.where(valid, pltpu.roll(hh, shift, 1), 0.0)
            hh = aa * h_sh + hh
            aa = aa * a_sh
            k *= 2
        carry = h_scr[...]
        last = SCAN_GROUP - 1 if forward else 0
        for g in (range(ngroups) if forward else range(ngroups - 1, -1, -1)):
            hg = hh[g] + aa[g] * carry
            o_ref[0, 0, g * SCAN_GROUP:(g + 1) * SCAN_GROUP, :] = hg
            carry = hg[last:last + 1, :]
        h_scr[...] = carry

    @pl.when(d == 0)
    def _():
        scan(True)

    @pl.when(d == 1)
    def _():
        scan(False)


def _lru(p, conv_w, conv_b, wg, bg, lam, nctx):
    b, lt, _ = p.shape
    nblk = lt // TB
    hpb = TB // HALO
    nh = lt // HALO

    def blk(d, j):
        return _scan_block(j, d, nctx, nblk)

    return pl.pallas_call(
        functools.partial(_lru_kernel, nctx=nctx, nblk=nblk),
        grid=(b, 2, nblk),
        in_specs=[pl.BlockSpec((1, TB, LRU_W), lambda bi, d, j: (bi, blk(d, j), 4)),
                  pl.BlockSpec((1, HALO, LRU_W), lambda bi, d, j: (bi, jnp.maximum(blk(d, j) * hpb - 1, 0), 4)),
                  pl.BlockSpec((1, HALO, LRU_W), lambda bi, d, j: (bi, jnp.minimum((blk(d, j) + 1) * hpb, nh - 1), 4)),
                  _const_spec((8, LRU_W)), _const_spec((1, LRU_W)),
                  pl.BlockSpec((1, LRU_W, 2 * LRU_W), lambda bi, d, j: (d, 0, 0)),
                  pl.BlockSpec((1, 8, 2 * LRU_W), lambda bi, d, j: (d, 0, 0)),
                  pl.BlockSpec((1, 8, LRU_W), lambda bi, d, j: (d, 0, 0))],
        out_specs=pl.BlockSpec((1, 1, TB, LRU_W), lambda bi, d, j: (d, bi, blk(d, j), 0)),
        out_shape=jax.ShapeDtypeStruct((2, b, lt, LRU_W), F32),
        scratch_shapes=[pltpu.VMEM((TB + 2 * HALO, LRU_W), F32), pltpu.VMEM((1, LRU_W), F32)],
        compiler_params=_cparams(("parallel", "arbitrary", "arbitrary")),
        name="rglru",
    )(p, p, p, conv_w, conv_b, wg, bg, lam)


def _gelu_tanh(t):
    return 0.5 * t * (1.0 + jnp.tanh(0.7978845608028654 * (t + 0.044715 * t * t * t)))


def _merge_kernel(x_ref, mctx_ref, mb_ref, hgf_ref, hgb_ref, hgg_ref, hgn_ref, hm_ref, mlac_ref, mlax_ref,
                  lf_ref, lb_ref, ly_ref, g1_ref, g2_ref, g3_ref, wbr_ref, wout_ref, o_ref, *, ctx_len):
    i = pl.program_id(1)
    nctx = ctx_len // TB
    o_mla = jnp.where(i < nctx, mlac_ref[0], mlax_ref[0])
    o = hgf_ref[0, 0] + hgb_ref[0, 0]
    ms = jnp.dot((o * o).astype(BF16), hm_ref[...], preferred_element_type=F32)
    o_hg = o * lax.rsqrt(ms + EPS) * hgn_ref[...] * _sigmoid(hgg_ref[0].astype(F32))
    o_lru = (lf_ref[0, 0] + lb_ref[0, 0]) * _gelu_tanh(ly_ref[0].astype(F32))
    y = (_sigmoid(g1_ref[0].astype(F32)) * jnp.dot(o_hg.astype(BF16), wbr_ref[0], preferred_element_type=F32)
         + _sigmoid(g2_ref[0].astype(F32)) * jnp.dot(o_mla, wbr_ref[1], preferred_element_type=F32)
         + _sigmoid(g3_ref[0].astype(F32)) * jnp.dot(o_lru.astype(BF16), wbr_ref[2], preferred_element_type=F32))
    gate = _row_mods(i, TB, ctx_len, mctx_ref, mb_ref, 2)
    o_ref[0] = x_ref[0] + gate * jnp.dot(y.astype(BF16), wout_ref[...], preferred_element_type=F32)


def _merge(xs, mctx, mb, o_hg, p, hg_gain, head_mean, o_mla_c, o_mla_x, h_lru, w_br, w_out, ctx_len):
    b, lt, _ = xs.shape
    nctx = ctx_len // TB

    def pcol(width, c):
        return pl.BlockSpec((1, TB, width), lambda bi, i: (bi, i, c))

    def dirspec(d):
        return pl.BlockSpec((1, 1, TB, 512), lambda bi, i: (d, bi, i, 0))

    return pl.pallas_call(
        functools.partial(_merge_kernel, ctx_len=ctx_len),
        grid=(b, lt // TB),
        in_specs=[pl.BlockSpec((1, TB, D_MODEL), lambda bi, i: (bi, i, 0)),
                  _const_spec((6, D_MODEL)),
                  pl.BlockSpec((1, 6, D_MODEL), lambda bi, i: (bi, 0, 0)),
                  dirspec(0), dirspec(0), pcol(512, 2),
                  _const_spec((1, HG_W)), _const_spec((HG_W, HG_W)),
                  pl.BlockSpec((1, TB, MLA_W), lambda bi, i: (bi, jnp.minimum(i, nctx - 1), 0)),
                  pl.BlockSpec((1, TB, MLA_W), lambda bi, i: (bi, jnp.maximum(i - nctx, 0), 0)),
                  dirspec(0), dirspec(1), pcol(512, 5),
                  pcol(D_MODEL, 3), pcol(D_MODEL, 4), pcol(D_MODEL, 5),
                  _const_spec((3, 512, D_MODEL)), _const_spec((D_MODEL, D_MODEL))],
        out_specs=pl.BlockSpec((1, TB, D_MODEL), lambda bi, i: (bi, i, 0)),
        out_shape=jax.ShapeDtypeStruct((b, lt, D_MODEL), F32),
        compiler_params=_cparams(("parallel", "arbitrary")),
        name="merge",
    )(xs, mctx, mb, o_hg[0], o_hg[1], p, hg_gain, head_mean, o_mla_c, o_mla_x, h_lru, h_lru, p, p, p, p, w_br, w_out)


def _first_row_of_max(vals, row, valid):
    masked = jnp.where(valid, vals, -jnp.inf)
    m = jnp.max(masked, axis=0, keepdims=True)
    idx = jnp.min(jnp.where(masked == m, row, ROUTER_PAD), axis=0, keepdims=True)
    return m, idx


def _router(h, wr_hi, wr_lo, br):
    hi = h.astype(BF16)
    lo = (h - hi.astype(F32)).astype(BF16)
    logits = (jnp.dot(hi, wr_hi, preferred_element_type=F32) + jnp.dot(lo, wr_hi, preferred_element_type=F32)
              + jnp.dot(hi, wr_lo, preferred_element_type=F32))
    biased = (logits + br).T
    logits = logits.T
    row = lax.broadcasted_iota(jnp.int32, logits.shape, 0)
    is_group = jnp.logical_and(row >= N_EXPERTS, row < N_EXPERTS + N_GROUPS)
    _, g_row = _first_row_of_max(biased, row, is_group)
    g_max, _ = _first_row_of_max(logits, row, is_group)
    g_exp = jnp.where(is_group, jnp.exp(logits - g_max), 0.0)
    g_sel_logit = jnp.sum(jnp.where(row == g_row, logits, 0.0), axis=0, keepdims=True)
    p_g = jnp.exp(g_sel_logit - g_max) / jnp.sum(g_exp, axis=0, keepdims=True)
    in_group = jnp.right_shift(row, 2) == (g_row - N_EXPERTS)
    _, i1 = _first_row_of_max(biased, row, in_group)
    _, i2 = _first_row_of_max(biased, row, jnp.logical_and(in_group, row != i1))
    l1 = jnp.sum(jnp.where(row == i1, logits, 0.0), axis=0, keepdims=True)
    l2 = jnp.sum(jnp.where(row == i2, logits, 0.0), axis=0, keepdims=True)
    lm = jnp.maximum(l1, l2)
    e1 = jnp.exp(l1 - lm)
    e2 = jnp.exp(l2 - lm)
    inv = p_g / (e1 + e2)
    comb_t = jnp.where(row == i1, e1 * inv, 0.0) + jnp.where(row == i2, e2 * inv, 0.0)
    return comb_t.T


def _moe_kernel(x_ref, mctx_ref, mb_ref, gain_ref, wrh_ref, wrl_ref, br_ref, w1_ref, w3_ref, w2_ref, o_ref, *,
                ctx_len, tm, tile0):
    i = pl.program_id(1) + tile0
    epb = N_EXPERTS // EXPERT_STEPS
    nrows = min(MOE_RB, tm)
    for rb in range(tm // nrows):
        rows = slice(rb * nrows, (rb + 1) * nrows)
        tile = i * (tm // nrows) + rb
        shift = _row_mods(tile, nrows, ctx_len, mctx_ref, mb_ref, 3)
        scale = _row_mods(tile, nrows, ctx_len, mctx_ref, mb_ref, 4)
        gate = _row_mods(tile, nrows, ctx_len, mctx_ref, mb_ref, 5)
        x = x_ref[0, rows, :]
        h = _norm_modulate(x, gain_ref[...], shift, scale)
        comb = _router(h, wrh_ref[...], wrl_ref[...], br_ref[...])
        hb = h.astype(BF16)
        acc = None
        for es in range(EXPERT_STEPS):
            parts = []
            for e in range(es * epb, (es + 1) * epb):
                h1 = jnp.dot(hb, w1_ref[e], preferred_element_type=F32)
                h3 = jnp.dot(hb, w3_ref[e], preferred_element_type=F32)
                parts.append(h1 * _sigmoid(h1) * h3 * comb[:, e:e + 1])
            y = jnp.dot(jnp.concatenate(parts, axis=-1).astype(BF16), w2_ref[es], preferred_element_type=F32)
            acc = y if acc is None else acc + y
        o_ref[0, rows, :] = x + gate * acc


def _moe(xs, mctx, mb, gain, wr_hi, wr_lo, br, w1, w3, w2, ctx_len, latent_only):
    b, lt, _ = xs.shape
    epb = N_EXPERTS // EXPERT_STEPS
    if latent_only:
        tm, tile0, rows = TB, ctx_len // TB, lt - ctx_len
    else:
        tm, tile0, rows = (MOE_TM if lt % MOE_TM == 0 else TB), 0, lt
    return pl.pallas_call(
        functools.partial(_moe_kernel, ctx_len=ctx_len, tm=tm, tile0=tile0),
        grid=(b, rows // tm),
        in_specs=[pl.BlockSpec((1, tm, D_MODEL), lambda bi, i: (bi, i + tile0, 0)),
                  _const_spec((6, D_MODEL)),
                  pl.BlockSpec((1, 6, D_MODEL), lambda bi, i: (bi, 0, 0)),
                  _const_spec((1, D_MODEL)),
                  _const_spec((D_MODEL, ROUTER_PAD)), _const_spec((D_MODEL, ROUTER_PAD)), _const_spec((1, ROUTER_PAD)),
                  _const_spec((N_EXPERTS, D_MODEL, D_EXPERT)), _const_spec((N_EXPERTS, D_MODEL, D_EXPERT)),
                  _const_spec((EXPERT_STEPS, epb * D_EXPERT, D_MODEL))],
        out_specs=pl.BlockSpec((1, tm, D_MODEL), lambda bi, i: (bi, i, 0)),
        out_shape=jax.ShapeDtypeStruct((b, rows, D_MODEL), F32),
        compiler_params=_cparams(("parallel", "arbitrary")),
        name="moe",
    )(xs, mctx, mb, gain, wr_hi, wr_lo, br, w1, w3, w2)


def _rope_tables(ctx_len, seq):
    half = MLA_ROPE // 2
    rows = seq // GRID_W
    pos_row = np.repeat(np.arange(rows, dtype=np.float32), GRID_W)
    pos_col = np.tile(np.arange(GRID_W, dtype=np.float32), rows)
    inv = (ROPE_BASE ** (-np.arange(0, half, 2, dtype=np.float32) / half)).astype(np.float32)
    ang = np.concatenate([pos_row[:, None] * inv, pos_col[:, None] * inv], axis=-1)
    cos, sin = np.cos(ang), np.sin(ang)
    cos_t = np.zeros((ctx_len + seq, HEAD_PAD), np.float32)
    sin_t = np.zeros((ctx_len + seq, HEAD_PAD), np.float32)
    cos_t[:, :MLA_NOPE] = 1.0
    cos_t[:ctx_len, MLA_NOPE:MLA_QK] = 1.0
    cos_t[ctx_len:, MLA_NOPE:MLA_NOPE + half] = cos
    cos_t[ctx_len:, MLA_NOPE + half:MLA_QK] = cos
    sin_t[ctx_len:, MLA_NOPE:MLA_NOPE + half] = -sin
    sin_t[ctx_len:, MLA_NOPE + half:MLA_QK] = sin
    return cos_t, sin_t


def _rope_key_selectors():
    half = MLA_ROPE // 2
    kr0 = MLA_Q_RANK + MLA_KV_RANK
    sk1 = np.zeros((512, MLA_HEADS * HEAD_PAD), np.float32)
    sk2 = np.zeros((512, MLA_HEADS * HEAD_PAD), np.float32)
    for h in range(MLA_HEADS):
        for i in range(MLA_ROPE):
            sk1[kr0 + i, h * HEAD_PAD + MLA_NOPE + i] = 1.0
            sk2[kr0 + (i + half) % MLA_ROPE, h * HEAD_PAD + MLA_NOPE + i] = 1.0
    return sk1.astype(BF16), sk2.astype(BF16)


def _np_block_diag(block, n):
    a, bb = block.shape
    out = np.zeros((n * a, n * bb), np.float32)
    for i in range(n):
        out[i * a:(i + 1) * a, i * bb:(i + 1) * bb] = block
    return out


def _swap_rope_halves(t):
    half = MLA_ROPE // 2
    return jnp.concatenate([jnp.zeros_like(t[..., :MLA_NOPE]), t[..., MLA_NOPE + half:], t[..., MLA_NOPE:MLA_NOPE + half]],
                           axis=-1)


def _pad_heads(t):
    pad = [(0, 0)] * (t.ndim - 1) + [(0, HEAD_PAD - t.shape[-1])]
    t = jnp.pad(t, pad)
    return t.reshape(t.shape[:-2] + (t.shape[-2] * HEAD_PAD,))


def _mla_weights(w_uq, w_ukv, gq, gk):
    wq = w_uq.reshape(MLA_Q_RANK, MLA_HEADS, MLA_QK)
    wq1 = _pad_heads(wq).astype(BF16)
    wq2 = _pad_heads(_swap_rope_halves(wq)).astype(BF16)
    wkv = w_ukv.reshape(MLA_KV_RANK, MLA_HEADS, MLA_NOPE + MLA_V)
    wk1 = _pad_heads(wkv[..., :MLA_NOPE]).astype(BF16)
    wv = _pad_heads(wkv[..., MLA_NOPE:]).astype(BF16)

    def gains(g):
        g1 = jnp.pad(g, (0, HEAD_PAD - MLA_QK))
        g2 = jnp.pad(_swap_rope_halves(g), (0, HEAD_PAD - MLA_QK))
        return jnp.stack([g1, g2], axis=0)

    return wq1, wq2, wk1, wv, gains(gq), gains(gk)


def _block_diag(w):
    n, a, bb = w.shape
    eye = jnp.eye(n, dtype=w.dtype)
    return (eye[:, None, :, None] * w[:, :, None, :]).reshape(n * a, n * bb)


def _pad_rows(t, rows=8):
    return jnp.pad(t, ((0, rows - t.shape[0]), (0, 0)))


def kernel(x, c, ctx, c_ctx, w_mod, b_mod, norm_mix, norm_ffn, w_in, hg_lb, hg_norm, mla_q_norm, mla_kv_norm, mla_w_uq, mla_w_ukv, mla_qk_gain_q, mla_qk_gain_k, lru_conv_w, lru_conv_b, lru_wa, lru_ba, lru_wx, lru_bx, lru_lambda, w_br_hg, w_br_mla, w_br_lru, w_out, moe_w_rg, moe_b_rg, moe_w_re, moe_b_re, moe_w1, moe_w3, moe_w2):
    bsz, seq, _ = x.shape
    ctx_len = ctx.shape[1]
    depth = w_in.shape[0]
    assert seq % TB == 0 and ctx_len % TB == 0 and seq % GRID_W == 0 and bsz < 8
    nctx = ctx_len // TB

    xs = jnp.concatenate([ctx, x], axis=1)
    cos_t, sin_t = _rope_tables(ctx_len, seq)
    cvec = jnp.zeros((8, D_MODEL), F32).at[:bsz].set(c).at[bsz].set(c_ctx)

    lb_cs = jnp.cumsum(jax.nn.softmax(hg_lb.astype(F32), axis=0), axis=0)
    lb_all = lb_cs - lb_cs[0:1]
    tri_lo = _np_block_diag(np.tril(np.ones((HG_CHUNK, HG_CHUNK), np.float32)), TB // HG_CHUNK)
    tri = np.stack([tri_lo, tri_lo.T], axis=0).astype(BF16)
    head_ones = _np_block_diag(np.ones((HG_DK, HG_DK), np.float32), HG_HEADS)
    head_mean = (head_ones / HG_DK).astype(BF16)
    sk1, sk2 = _rope_key_selectors()

    offs = np.cumsum((HG_W,) * 5 + (MLA_Q_RANK, MLA_KV_RANK, MLA_ROPE, LRU_W, LRU_W) + (D_MODEL,) * 3)[:-1].tolist()

    for l in range(depth):
        mods = _modulation(cvec, w_mod[l], b_mod[l]).reshape(8, 6, D_MODEL)
        mctx, mb = mods[bsz], mods[:bsz]

        q_, ff, fb, i_, g_, dq, dkv, kr, lx, ly, g1, g2, g3 = jnp.split(w_in[l].astype(BF16), offs, axis=-1)
        kr_pad = jnp.zeros((D_MODEL, 512 - MLA_Q_RANK - MLA_KV_RANK - MLA_ROPE), BF16)
        w_all = jnp.concatenate([ff, fb, q_, i_, g_, dq, dkv, kr, kr_pad, lx, ly, g1, g2, g3], axis=-1)
        f, p = _inproj(xs, mctx, mb, norm_mix[l][None, :], w_all, ctx_len)

        lb = lb_all[l]
        lbc = jnp.stack([jnp.log(lb), jnp.log1p(-lb), 1.0 - lb] + [jnp.zeros_like(lb)] * 5, axis=1)
        o_hg = _hgrn(p, f, tri, lbc, head_ones, nctx)

        wq1, wq2, wk1, wv, gq, gk = _mla_weights(mla_w_uq[l], mla_w_ukv[l], mla_qk_gain_q[l], mla_qk_gain_k[l])
        qt, kh, vt, ksq = _mla_prep(p, cos_t, sin_t, mla_q_norm[l][None, :], mla_kv_norm[l][None, :], gq, gk,
                                    wq1, wq2, wk1, sk1, sk2, wv, nctx)
        kmax = jnp.sqrt(jnp.max(ksq, axis=(1, 3)))
        tq = next(t for t in (512, 256) if seq % t == 0)
        o_mla_x = _flash(kmax, qt, kh, vt, seq, ctx_len + seq, tq, 0, "mla_attention")
        if l < depth - 1:
            o_mla_c = _flash(kmax, qt, kh, vt, ctx_len, ctx_len, TB, seq, "mla_attention_ctx")
        else:
            o_mla_c = o_mla_x[:, :ctx_len]

        wg = jnp.stack([jnp.concatenate([_block_diag(lru_wa[l, d]), _block_diag(lru_wx[l, d])], axis=-1)
                        for d in range(2)], axis=0).astype(BF16)
        bg = jnp.concatenate([lru_ba[l], lru_bx[l]], axis=-1)[:, None, :] * jnp.ones((1, 8, 1), F32)
        lam = lru_lambda[l][:, None, :] * jnp.ones((1, 8, 1), F32)
        h_lru = _lru(p, _pad_rows(lru_conv_w[l]), lru_conv_b[l][None, :], wg, bg, lam, nctx)

        w_br = jnp.stack([w_br_hg[l], w_br_mla[l], w_br_lru[l]], axis=0).astype(BF16)
        hg_gain = jnp.tile(hg_norm[l], HG_HEADS)[None, :]
        xs = _merge(xs, mctx, mb, o_hg, p, hg_gain, head_mean, o_mla_c, o_mla_x, h_lru, w_br, w_out[l].astype(BF16), ctx_len)

        wr = jnp.pad(jnp.concatenate([moe_w_re[l], moe_w_rg[l]], axis=-1), ((0, 0), (0, ROUTER_PAD - N_EXPERTS - N_GROUPS)))
        br = jnp.pad(jnp.concatenate([moe_b_re[l], moe_b_rg[l]]), (0, ROUTER_PAD - N_EXPERTS - N_GROUPS))[None, :]
        w2 = moe_w2[l].astype(BF16).reshape(EXPERT_STEPS, N_EXPERTS // EXPERT_STEPS * D_EXPERT, D_MODEL)
        wr_hi = wr.astype(BF16)
        wr_lo = (wr - wr_hi.astype(F32)).astype(BF16)
        xs = _moe(xs, mctx, mb, norm_ffn[l][None, :], wr_hi, wr_lo, br, moe_w1[l].astype(BF16), moe_w3[l].astype(BF16), w2,
                  ctx_len, latent_only=(l == depth - 1))

    return xs
```

```python
import functools

import numpy as np
import jax
import jax.numpy as jnp
from jax import lax
from jax.experimental import pallas as pl
from jax.experimental.pallas import tpu as pltpu

F32 = jnp.float32
BF16 = jnp.bfloat16
HIGHEST = lax.Precision.HIGHEST

D_MODEL = 1024
GRID_W = 64
EPS = 1e-6

HG_HEADS = 8
HG_DK = 64
HG_W = 512
HG_CHUNK = 64
HG_MID = HG_CHUNK // 2
HG_MAX_LOG_RANGE = 80.0

MLA_HEADS = 8
MLA_Q_RANK = 256
MLA_KV_RANK = 128
MLA_NOPE = 64
MLA_ROPE = 32
MLA_V = 64
MLA_QK = MLA_NOPE + MLA_ROPE
MLA_W = MLA_HEADS * MLA_V
HEAD_PAD = 128
V_ROWS = 80
SCORE_SCALE = MLA_QK ** -0.5 * 1.4426950408889634
MAX_UNSHIFTED_SCORE = 57.0
ROPE_BASE = 10000.0

LRU_W = 512
LRU_BLOCKS = 8
LRU_BD = LRU_W // LRU_BLOCKS
CONV_W = 4
LRU_C = 8.0

N_GROUPS = 4
EXP_PER_GROUP = 4
N_EXPERTS = N_GROUPS * EXP_PER_GROUP
D_EXPERT = 256
EXPERT_STEPS = 4
ROUTER_PAD = 128
MOE_TM = 768
MOE_RB = 256
INPROJ_TM = 768

TB = 256
HALO = 16
SCAN_GROUP = 8
P_WIDTH = 6144
F_WIDTH = 1024
VMEM_LIMIT = 56 * 1024 * 1024

NT_DIMS = (((1,), (1,)), ((), ()))


def _cparams(sem):
    return pltpu.CompilerParams(dimension_semantics=sem, vmem_limit_bytes=VMEM_LIMIT)


def _const_spec(shape):
    nd = len(shape)
    return pl.BlockSpec(shape, lambda *_: (0,) * nd, pipeline_mode=pl.Buffered(1))


def _sigmoid(t):
    return 0.5 * jnp.tanh(0.5 * t) + 0.5


def _scan_block(j, d, nctx, nblk):
    fwd = j
    bwd = jnp.where(j < nctx, nctx - 1 - j, nblk - 1 - (j - nctx))
    return jnp.where(d == 0, fwd, bwd)


def _mod_kernel(c_ref, w_ref, b_ref, o_ref):
    c = c_ref[...]
    s = c * _sigmoid(c)
    o_ref[0] = jnp.dot(s, w_ref[0], precision=HIGHEST, preferred_element_type=F32) + b_ref[0]


def _modulation(cvec, w_mod, b_mod):
    depth, _, n = w_mod.shape
    tn = 1024
    return pl.pallas_call(
        _mod_kernel,
        grid=(depth, n // tn),
        in_specs=[pl.BlockSpec((8, D_MODEL), lambda l, j: (0, 0)),
                  pl.BlockSpec((1, D_MODEL, tn), lambda l, j: (l, 0, j)),
                  pl.BlockSpec((1, 1, tn), lambda l, j: (l, 0, j))],
        out_specs=pl.BlockSpec((1, 8, tn), lambda l, j: (l, 0, j)),
        out_shape=jax.ShapeDtypeStruct((depth, 8, n), F32),
        compiler_params=_cparams(("arbitrary", "arbitrary")),
        name="modulation",
    )(cvec, w_mod, b_mod.reshape(depth, 1, n))


def _row_mods(i, tm, ctx_len, mctx_ref, mb_ref, k):
    row = i * tm + lax.broadcasted_iota(jnp.int32, (tm, 1), 0)
    return jnp.where(row < ctx_len, mctx_ref[k:k + 1, :], mb_ref[0, k:k + 1, :])


def _norm_modulate(x, gain, shift, scale):
    ms = jnp.mean(x * x, axis=-1, keepdims=True)
    xn = x * lax.rsqrt(ms + EPS) * gain
    return xn * (1.0 + scale) + shift


def _stream_specs(xs, tm, nctx):
    if isinstance(xs, tuple):
        assert tm == TB
        return ([pl.BlockSpec((1, tm, D_MODEL), lambda bi, i: (bi, jnp.minimum(i, nctx - 1), 0)),
                 pl.BlockSpec((1, tm, D_MODEL), lambda bi, i: (bi, jnp.maximum(i - nctx, 0), 0))], list(xs))
    return [pl.BlockSpec((1, tm, D_MODEL), lambda bi, i: (bi, i, 0))], [xs]


def _stream_tile(x_refs, i, nctx):
    if len(x_refs) == 2:
        return jnp.where(i < nctx, x_refs[0][0], x_refs[1][0])
    return x_refs[0][0]


def _win_layout_kernel(w_ref, o_ref):
    def put(dst, src, width):
        o_ref[0, :, dst:dst + width] = w_ref[0, :, src:src + width].astype(BF16)

    mla_w = MLA_Q_RANK + MLA_KV_RANK + MLA_ROPE
    mla0 = 5 * HG_W
    put(0, HG_W, 2 * HG_W)
    put(F_WIDTH, 0, HG_W)
    put(F_WIDTH + HG_W, 3 * HG_W, 2 * HG_W)
    put(F_WIDTH + 3 * HG_W, mla0, mla_w)
    o_ref[0, :, F_WIDTH + 3 * HG_W + mla_w:F_WIDTH + 4 * HG_W] = jnp.zeros((o_ref.shape[1], HG_W - mla_w), BF16)
    rest = w_ref.shape[2] - mla0 - mla_w
    put(F_WIDTH + 4 * HG_W, mla0 + mla_w, rest)


def _win_layout(w_in):
    depth, d, n = w_in.shape
    tr = 256
    return pl.pallas_call(
        _win_layout_kernel,
        grid=(depth, d // tr),
        in_specs=[pl.BlockSpec((1, tr, n), lambda l, i: (l, i, 0))],
        out_specs=pl.BlockSpec((1, tr, F_WIDTH + P_WIDTH), lambda l, i: (l, i, 0)),
        out_shape=jax.ShapeDtypeStruct((depth, d, F_WIDTH + P_WIDTH), BF16),
        compiler_params=_cparams(("arbitrary", "arbitrary")),
        name="w_in_layout",
    )(w_in)


def _inproj_kernel(*refs, ctx_len, tm, nstream):
    x_refs = refs[:nstream]
    mctx_ref, mb_ref, gain_ref, w_ref, f_ref, p_ref = refs[nstream:]
    w_ref = w_ref.at[0]
    i = pl.program_id(1)
    shift = _row_mods(i, tm, ctx_len, mctx_ref, mb_ref, 0)
    scale = _row_mods(i, tm, ctx_len, mctx_ref, mb_ref, 1)
    h = _norm_modulate(_stream_tile(x_refs, i, ctx_len // TB), gain_ref[...], shift, scale).astype(BF16)
    f_ref[0] = jnp.dot(h, w_ref[:, 0:F_WIDTH], preferred_element_type=F32)
    cw = 512
    for j in range(P_WIDTH // cw):
        lo = F_WIDTH + j * cw
        p_ref[0, :, j * cw:(j + 1) * cw] = jnp.dot(
            h, w_ref[:, lo:lo + cw], preferred_element_type=F32).astype(BF16)


def _inproj(xs, mctx, mb, gain, w_all, layer, ctx_len):
    split = isinstance(xs, tuple)
    b = (xs[0] if split else xs).shape[0]
    lt = sum(t.shape[1] for t in xs) if split else xs.shape[1]
    tm = INPROJ_TM if lt % INPROJ_TM == 0 and not split else TB
    x_specs, x_args = _stream_specs(xs, tm, ctx_len // TB)
    return pl.pallas_call(
        functools.partial(_inproj_kernel, ctx_len=ctx_len, tm=tm, nstream=len(x_args)),
        grid=(b, lt // tm),
        in_specs=x_specs + [
                  _const_spec((6, D_MODEL)),
                  pl.BlockSpec((1, 6, D_MODEL), lambda bi, i: (bi, 0, 0)),
                  _const_spec((1, D_MODEL)),
                  pl.BlockSpec((1, D_MODEL, F_WIDTH + P_WIDTH), lambda bi, i: (layer, 0, 0), pipeline_mode=pl.Buffered(1))],
        out_specs=[pl.BlockSpec((1, tm, F_WIDTH), lambda bi, i: (bi, i, 0)),
                   pl.BlockSpec((1, tm, P_WIDTH), lambda bi, i: (bi, i, 0))],
        out_shape=[jax.ShapeDtypeStruct((b, lt, F_WIDTH), F32),
                   jax.ShapeDtypeStruct((b, lt, P_WIDTH), BF16)],
        compiler_params=_cparams(("parallel", "arbitrary")),
        name="inproj",
    )(*x_args, mctx, mb, gain, w_all)


def _hgrn_direction(q_ref, v_ref, f_ref, tri_ref, lbc_ref, bd_ref, o_ref, st_ref, c_scr, k_scr, v_scr, reverse):
    d = 1 if reverse else 0
    nch = TB // HG_CHUNK
    npair = HG_W // 128
    first, last = (HG_CHUNK - 1, 0) if reverse else (0, HG_CHUNK - 1)

    log_lb = lbc_ref[d, 0:1, :]
    log_1mlb = lbc_ref[d, 1:2, :]
    one_mlb = lbc_ref[d, 2:3, :]
    z = f_ref[0]
    q = q_ref[0].astype(F32)
    v = v_ref[0].astype(F32)
    e = jnp.exp(-jnp.abs(z))
    log_sig = jnp.minimum(z, 0.0) - jnp.log(1.0 + e)
    t = log_1mlb + log_sig
    logf = jnp.maximum(log_lb, t) + jnp.log(1.0 + jnp.exp(-jnp.abs(log_lb - t)))
    k = one_mlb * (jnp.where(z >= 0, e, 1.0) / (1.0 + e))
    hi = logf.astype(BF16)
    lo = (logf - hi.astype(F32)).astype(BF16)
    tri = tri_ref[d]
    c = jnp.dot(tri, hi, preferred_element_type=F32) + jnp.dot(tri, lo, preferred_element_type=F32)

    span = jnp.zeros((1, HG_W), F32)
    for ci in range(nch):
        r0 = ci * HG_CHUNK
        c_mid = c[r0 + HG_MID:r0 + HG_MID + 1, :]
        span = jnp.maximum(span, jnp.maximum(c[r0 + first:r0 + first + 1, :] - c_mid, c_mid - c[r0 + last:r0 + last + 1, :]))
    span = jnp.max(span)

    lane = lax.broadcasted_iota(jnp.int32, (HG_CHUNK, HG_W), 1)
    low_head = (lane & (HG_DK * 2 - 1)) < HG_DK
    ri = lax.broadcasted_iota(jnp.int32, (HG_CHUNK, 128), 0)
    si = lax.broadcasted_iota(jnp.int32, (HG_CHUNK, 128), 1) & (HG_DK - 1)
    visited = (si >= ri) if reverse else (si <= ri)
    r2 = lax.broadcasted_iota(jnp.int32, (128, 128), 0)
    l2 = lax.broadcasted_iota(jnp.int32, (128, 128), 1)
    same_head = (r2 < HG_DK) == (l2 < HG_DK)
    rowi = lax.broadcasted_iota(jnp.int32, (HG_CHUNK, 1), 0)

    def split_heads(t, transpose=False):
        top = jnp.where(low_head, t, 0.0)
        bot = jnp.where(low_head, 0.0, t)
        blocks = [jnp.concatenate([top[:, 128 * g:128 * (g + 1)], bot[:, 128 * g:128 * (g + 1)]], axis=0) for g in range(npair)]
        return [(blk.T if transpose else blk).astype(BF16) for blk in blocks]

    def exact_intra(qq, kk, vv, cc):
        c_scr[...] = cc
        k_scr[...] = kk
        v_scr[...] = vv
        bd = bd_ref[...]

        def key_row(s, acc):
            c_s = c_scr[pl.ds(s, 1), :]
            w = qq * jnp.exp(jnp.minimum(cc - c_s, 0.0)) * k_scr[pl.ds(s, 1), :]
            ws = jnp.dot(w, bd, precision=HIGHEST, preferred_element_type=F32)
            ok = (rowi <= s) if reverse else (rowi >= s)
            return acc + jnp.where(ok, ws, 0.0) * v_scr[pl.ds(s, 1), :]

        return lax.fori_loop(0, HG_CHUNK, key_row, jnp.zeros((HG_CHUNK, HG_W), F32))

    def chunk(ci, exact):
        r0 = ci * HG_CHUNK
        rows = slice(r0, r0 + HG_CHUNK)
        cc, qq, kk, vv = c[rows], q[rows], k[rows], v[rows]
        c_mid = cc[HG_MID:HG_MID + 1, :]
        c_end = cc[last:last + 1, :]
        qs = (qq * jnp.exp(cc)).astype(BF16)
        ks = (kk * jnp.exp(c_end - cc)).astype(BF16)
        dec = jnp.exp(c_end)
        if exact:
            intra = exact_intra(qq, kk, vv, cc)
        else:
            qd = (qq * jnp.exp(cc - c_mid)).astype(BF16)
            kd_blocks = split_heads(kk * jnp.exp(c_mid - cc), transpose=True)
            v_blocks = split_heads(vv)
        pieces = []
        for g in range(npair):
            ls = slice(128 * g, 128 * (g + 1))
            st = st_ref[d, g]
            o_g = jnp.dot(qs[:, ls], st.T.astype(BF16), preferred_element_type=F32)
            if not exact:
                a = jnp.dot(qd[:, ls], kd_blocks[g], preferred_element_type=F32)
                a = jnp.where(visited, a, 0.0).astype(BF16)
                o_g = o_g + jnp.dot(a, v_blocks[g], preferred_element_type=F32)
            vt = vv[:, ls].T.astype(BF16)
            kv = jnp.dot(vt, ks[:, ls], preferred_element_type=F32)
            st_ref[d, g] = st * dec[:, ls] + jnp.where(same_head, kv, 0.0)
            pieces.append(o_g)
        o = jnp.concatenate(pieces, axis=-1)
        o_ref[0, 0, rows, :] = o + intra if exact else o

    order = list(range(nch - 1, -1, -1) if reverse else range(nch))
    return span, lambda n, exact: chunk(order[n], exact)


def _hgrn_kernel(qf_ref, vf_ref, ff_ref, qb_ref, vb_ref, fb_ref, tri_ref, lbc_ref, bd_ref, of_ref, ob_ref,
                 st_ref, c_scr, k_scr, v_scr):
    @pl.when(pl.program_id(1) == 0)
    def _():
        st_ref[...] = jnp.zeros_like(st_ref)

    span_f, chunk_f = _hgrn_direction(qf_ref, vf_ref, ff_ref, tri_ref, lbc_ref, bd_ref, of_ref, st_ref,
                                      c_scr, k_scr, v_scr, False)
    span_b, chunk_b = _hgrn_direction(qb_ref, vb_ref, fb_ref, tri_ref, lbc_ref, bd_ref, ob_ref, st_ref,
                                      c_scr, k_scr, v_scr, True)
    span = jnp.maximum(span_f, span_b)

    def run(exact):
        for n in range(TB // HG_CHUNK):
            chunk_f(n, exact)
            chunk_b(n, exact)

    @pl.when(span <= HG_MAX_LOG_RANGE)
    def _matmul_form():
        run(False)

    @pl.when(span > HG_MAX_LOG_RANGE)
    def _exact_form():
        run(True)


def _hgrn(p, f, tri, lbc, bd, nctx):
    b, lt, _ = p.shape
    nblk = lt // TB

    def spec(d, col):
        return pl.BlockSpec((1, TB, HG_W), lambda bi, j: (bi, _scan_block(j, d, nctx, nblk), col))

    def out_spec(d):
        return pl.BlockSpec((1, 1, TB, HG_W), lambda bi, j: (0, bi, _scan_block(j, d, nctx, nblk), 0))

    out = jax.ShapeDtypeStruct((1, b, lt, HG_W), F32)
    return pl.pallas_call(
        _hgrn_kernel,
        grid=(b, nblk),
        in_specs=[spec(0, 0), spec(0, 1), spec(0, 0),
                  spec(1, 0), spec(1, 1), spec(1, 1),
                  _const_spec((2, TB, TB)), _const_spec((2, 8, HG_W)), _const_spec((HG_W, HG_W))],
        out_specs=[out_spec(0), out_spec(1)],
        out_shape=[out, out],
        scratch_shapes=[pltpu.VMEM((2, HG_W // 128, 128, 128), F32),
                        pltpu.VMEM((HG_CHUNK, HG_W), F32),
                        pltpu.VMEM((HG_CHUNK, HG_W), F32),
                        pltpu.VMEM((HG_CHUNK, HG_W), F32)],
        compiler_params=_cparams(("parallel", "arbitrary")),
        name="hgrn2",
    )(p, p, f, p, p, f, tri, lbc, bd)


def _rms_rows(t, width):
    return lax.rsqrt(jnp.sum(t * t, axis=-1, keepdims=True) * (1.0 / width) + EPS)


def _mla_prep_kernel(pm_ref, cos_ref, sin_ref, qn_ref, kvn_ref, gq_ref, gk_ref,
                     wq1_ref, wq2_ref, wk1_ref, sk1_ref, sk2_ref, wv_ref,
                     qt_ref, k_ref, vt_ref, kn_ref):
    pm = pm_ref[0]
    dq = pm[:, 0:MLA_Q_RANK].astype(F32)
    dkv = pm[:, MLA_Q_RANK:MLA_Q_RANK + MLA_KV_RANK].astype(F32)
    dqn = (dq * _rms_rows(dq, MLA_Q_RANK) * qn_ref[...]).astype(BF16)
    dkvn = (dkv * _rms_rows(dkv, MLA_KV_RANK) * kvn_ref[...]).astype(BF16)
    q1 = jnp.dot(dqn, wq1_ref[...], preferred_element_type=F32)
    q2 = jnp.dot(dqn, wq2_ref[...], preferred_element_type=F32)
    k1 = (jnp.dot(dkvn, wk1_ref[...], preferred_element_type=F32)
          + jnp.dot(pm, sk1_ref[...], preferred_element_type=F32))
    k2 = jnp.dot(pm, sk2_ref[...], preferred_element_type=F32)
    vv = jnp.dot(dkvn, wv_ref[...], preferred_element_type=F32)
    cos = cos_ref[...]
    sin = sin_ref[...]
    gq_c = gq_ref[0:1, :] * cos
    gq_s = gq_ref[1:2, :] * sin
    gk_c = gk_ref[0:1, :] * cos
    gk_s = gk_ref[1:2, :] * sin
    vrow = lax.broadcasted_iota(jnp.int32, (V_ROWS, TB), 0)
    kn = []
    for h in range(MLA_HEADS):
        sl = slice(HEAD_PAD * h, HEAD_PAD * (h + 1))
        qh = q1[:, sl]
        qo = (qh * gq_c + q2[:, sl] * gq_s) * (_rms_rows(qh, MLA_QK) * SCORE_SCALE)
        qt_ref[0, h] = qo.T.astype(BF16)
        kh = k1[:, sl]
        ko = (kh * gk_c + k2[:, sl] * gk_s) * _rms_rows(kh, MLA_QK)
        k_ref[0, h] = ko.astype(BF16)
        ksq = jnp.max(jnp.sum(ko * ko, axis=-1, keepdims=True), axis=0, keepdims=True)
        kn.append(jnp.broadcast_to(ksq, (1, HEAD_PAD)))
        vt = vv[:, sl].T[0:V_ROWS, :]
        vt_ref[0, h] = jnp.where(vrow == MLA_V, 1.0, vt).astype(BF16)
    kn_ref[0, 0] = jnp.concatenate(kn, axis=0)


def _mla_prep(p, cos_t, sin_t, qn, kvn, gq, gk, wq1, wq2, wk1, sk1, sk2, wv, nctx):
    b, lt, _ = p.shape
    hw = MLA_HEADS * HEAD_PAD
    nblk = lt // TB
    return pl.pallas_call(
        _mla_prep_kernel,
        grid=(b, lt // TB),
        in_specs=[pl.BlockSpec((1, TB, 512), lambda bi, i: (bi, i, 3)),
                  pl.BlockSpec((TB, HEAD_PAD), lambda bi, i: (i, 0)),
                  pl.BlockSpec((TB, HEAD_PAD), lambda bi, i: (i, 0)),
                  _const_spec((1, MLA_Q_RANK)), _const_spec((1, MLA_KV_RANK)),
                  _const_spec((2, HEAD_PAD)), _const_spec((2, HEAD_PAD)),
                  _const_spec((MLA_Q_RANK, hw)), _const_spec((MLA_Q_RANK, hw)),
                  _const_spec((MLA_KV_RANK, hw)), _const_spec((512, hw)), _const_spec((512, hw)),
                  _const_spec((MLA_KV_RANK, hw))],
        out_specs=[pl.BlockSpec((1, MLA_HEADS, HEAD_PAD, TB), lambda bi, i: (bi, 0, 0, (i + nblk - nctx) % nblk)),
                   pl.BlockSpec((1, MLA_HEADS, TB, HEAD_PAD), lambda bi, i: (bi, 0, i, 0)),
                   pl.BlockSpec((1, MLA_HEADS, V_ROWS, TB), lambda bi, i: (bi, 0, 0, i)),
                   pl.BlockSpec((1, 1, MLA_HEADS, HEAD_PAD), lambda bi, i: (bi, i, 0, 0))],
        out_shape=[jax.ShapeDtypeStruct((b, MLA_HEADS, HEAD_PAD, lt), BF16),
                   jax.ShapeDtypeStruct((b, MLA_HEADS, lt, HEAD_PAD), BF16),
                   jax.ShapeDtypeStruct((b, MLA_HEADS, V_ROWS, lt), BF16),
                   jax.ShapeDtypeStruct((b, lt // TB, MLA_HEADS, HEAD_PAD), F32)],
        compiler_params=_cparams(("parallel", "arbitrary")),
        name="mla_prep",
    )(p, cos_t, sin_t, qn, kvn, gq, gk, wq1, wq2, wk1, sk1, sk2, wv)


def _attend_bounded(qts, k_ref, vt_ref, s_scr, nkeys, tk):
    n = nkeys // tk

    def scores(kb, slot):
        for hh in range(2):
            s_scr[slot, hh, 0:tk, :] = jnp.dot(k_ref[0, hh, kb * tk:(kb + 1) * tk, :], qts[hh], preferred_element_type=F32)

    def accumulate(kb, slot, accs):
        out = []
        for hh in range(2):
            p = jnp.exp2(s_scr[slot, hh, 0:tk, :]).astype(BF16)
            pv = jnp.dot(vt_ref[0, hh, :, kb * tk:(kb + 1) * tk], p, preferred_element_type=F32)
            out.append(pv if accs is None else accs[hh] + pv)
        return out

    scores(0, 0)
    accs = None
    for kb in range(n):
        if kb + 1 < n:
            scores(kb + 1, (kb + 1) % 2)
        accs = accumulate(kb, kb % 2, accs)
    return accs


def _attend_online(qts, k_ref, vt_ref, nkeys, tk):
    def kv_step(kb, carry):
        r0 = pl.multiple_of(kb * tk, tk)
        out = []
        for hh in range(2):
            m, acc = carry[hh]
            s = jnp.dot(k_ref[0, hh, pl.ds(r0, tk), :], qts[hh], preferred_element_type=F32)
            m_new = jnp.maximum(m, jnp.max(s, axis=0, keepdims=True))
            p = jnp.exp2(s - m_new).astype(BF16)
            acc = jnp.exp2(m - m_new) * acc + jnp.dot(vt_ref[0, hh, :, pl.ds(r0, tk)], p, preferred_element_type=F32)
            out.append((m_new, acc))
        return tuple(out)

    tq = qts[0].shape[1]
    init = (jnp.full((1, tq), -jnp.inf, F32), jnp.zeros((V_ROWS, tq), F32))
    res = lax.fori_loop(0, nkeys // tk, kv_step, (init, init))
    return res[0][1], res[1][1]


def _flash_kernel(kmax_ref, qt_ref, k_ref, vt_ref, o_ref, s_scr, *, nkeys, tk):
    bi = pl.program_id(0)
    g = pl.program_id(1)

    qts = [qt_ref[0, hh] for hh in range(2)]
    worst = 0.0
    for hh in range(2):
        qt = qts[hh].astype(F32)
        qnorm = jnp.sqrt(jnp.max(jnp.sum(qt * qt, axis=0, keepdims=True)))
        worst = jnp.maximum(worst, qnorm * kmax_ref[bi, 2 * g + hh])

    def finish(accs):
        outs = [(acc[0:MLA_V, :] / acc[MLA_V:MLA_V + 1, :]).T for acc in accs]
        o_ref[0] = jnp.concatenate(outs, axis=-1).astype(BF16)

    @pl.when(worst <= MAX_UNSHIFTED_SCORE)
    def _():
        finish(_attend_bounded(qts, k_ref, vt_ref, s_scr, nkeys, tk))

    @pl.when(worst > MAX_UNSHIFTED_SCORE)
    def _():
        finish(_attend_online(qts, k_ref, vt_ref, nkeys, tk))


def _flash(kmax, qt, k, vt, nq, nkeys, tq, q_col0, name):
    b = k.shape[0]
    tk = next(t for t in (768, 512, 256) if nkeys % t == 0)
    cb0 = q_col0 // tq
    return pl.pallas_call(
        functools.partial(_flash_kernel, nkeys=nkeys, tk=tk),
        grid=(b, MLA_HEADS // 2, nq // tq),
        in_specs=[pl.BlockSpec(memory_space=pltpu.SMEM),
                  pl.BlockSpec((1, 2, HEAD_PAD, tq), lambda bi, g, i: (bi, g, 0, cb0 + i)),
                  pl.BlockSpec((1, 2, nkeys, HEAD_PAD), lambda bi, g, i: (bi, g, 0, 0)),
                  pl.BlockSpec((1, 2, V_ROWS, nkeys), lambda bi, g, i: (bi, g, 0, 0))],
        out_specs=pl.BlockSpec((1, tq, HEAD_PAD), lambda bi, g, i: (bi, i, g)),
        out_shape=jax.ShapeDtypeStruct((b, nq, MLA_W), BF16),
        scratch_shapes=[pltpu.VMEM((2, 2, tk, tq), F32)],
        compiler_params=_cparams(("parallel", "parallel", "arbitrary")),
        name=name,
    )(kmax, qt, k, vt)


def _lru_kernel(x_ref, xp_ref, xn_ref, cw_ref, cb_ref, wg_ref, bg_ref, lam_ref, o_ref, xe_scr, h_scr, *, nctx, nblk):
    d = pl.program_id(1)
    j = pl.program_id(2)
    blk = _scan_block(j, d, nctx, nblk)

    @pl.when(j == 0)
    def _():
        h_scr[...] = jnp.zeros_like(h_scr)

    keep_prev = jnp.logical_and(blk != 0, blk != nctx)
    keep_next = jnp.logical_and(blk != nctx - 1, blk != nblk - 1)
    xe_scr[0:HALO, :] = jnp.where(keep_prev, xp_ref[0].astype(F32), 0.0)
    xe_scr[HALO:HALO + TB, :] = x_ref[0].astype(F32)
    xe_scr[HALO + TB:2 * HALO + TB, :] = jnp.where(keep_next, xn_ref[0].astype(F32), 0.0)
    left = CONV_W // 2
    u = jnp.broadcast_to(cb_ref[...], (TB, LRU_W))
    for tap in range(CONV_W):
        u = u + xe_scr[pl.ds(HALO - left + tap, TB), :] * cw_ref[tap:tap + 1, :]

    g = jnp.dot(u.astype(BF16), wg_ref[0], preferred_element_type=F32) + bg_ref[0, 0:1, :]
    r = _sigmoid(g[:, 0:LRU_W])
    ig = _sigmoid(g[:, LRU_W:2 * LRU_W])
    neg_lam = -lam_ref[0, 0:1, :]
    softplus = jnp.maximum(neg_lam, 0.0) + jnp.log1p(jnp.exp(-jnp.abs(neg_lam)))
    a = jnp.exp(-LRU_C * r * softplus)
    bb = jnp.sqrt(1.0 - a * a) * (ig * u)
    def scan(forward):
        ngroups = TB // SCAN_GROUP
        aa = a.reshape(ngroups, SCAN_GROUP, LRU_W)
        hh = bb.reshape(ngroups, SCAN_GROUP, LRU_W)
        rg = lax.broadcasted_iota(jnp.int32, (1, SCAN_GROUP, 1), 1)
        k = 1
        while k < SCAN_GROUP:
            shift = k if forward else SCAN_GROUP - k
            valid = (rg >= k) if forward else (rg < SCAN_GROUP - k)
            a_sh = jnp.where(valid, pltpu.roll(aa, shift, 1), 1.0)
            h_sh = jnp.where(valid, pltpu.roll(hh, shift, 1), 0.0)
            hh = aa * h_sh + hh
            aa = aa * a_sh
            k *= 2
        carry = h_scr[...]
        last = SCAN_GROUP - 1 if forward else 0
        for g in (range(ngroups) if forward else range(ngroups - 1, -1, -1)):
            hg = hh[g] + aa[g] * carry
            o_ref[0, 0, g * SCAN_GROUP:(g + 1) * SCAN_GROUP, :] = hg
            carry = hg[last:last + 1, :]
        h_scr[...] = carry

    @pl.when(d == 0)
    def _():
        scan(True)

    @pl.when(d == 1)
    def _():
        scan(False)


def _lru(p, conv_w, conv_b, wg, bg, lam, nctx):
    b, lt, _ = p.shape
    nblk = lt // TB
    hpb = TB // HALO
    nh = lt // HALO

    def blk(d, j):
        return _scan_block(j, d, nctx, nblk)

    return pl.pallas_call(
        functools.partial(_lru_kernel, nctx=nctx, nblk=nblk),
        grid=(b, 2, nblk),
        in_specs=[pl.BlockSpec((1, TB, LRU_W), lambda bi, d, j: (bi, blk(d, j), 4)),
                  pl.BlockSpec((1, HALO, LRU_W), lambda bi, d, j: (bi, jnp.maximum(blk(d, j) * hpb - 1, 0), 4)),
                  pl.BlockSpec((1, HALO, LRU_W), lambda bi, d, j: (bi, jnp.minimum((blk(d, j) + 1) * hpb, nh - 1), 4)),
                  _const_spec((8, LRU_W)), _const_spec((1, LRU_W)),
                  pl.BlockSpec((1, LRU_W, 2 * LRU_W), lambda bi, d, j: (d, 0, 0)),
                  pl.BlockSpec((1, 8, 2 * LRU_W), lambda bi, d, j: (d, 0, 0)),
                  pl.BlockSpec((1, 8, LRU_W), lambda bi, d, j: (d, 0, 0))],
        out_specs=pl.BlockSpec((1, 1, TB, LRU_W), lambda bi, d, j: (d, bi, blk(d, j), 0)),
        out_shape=jax.ShapeDtypeStruct((2, b, lt, LRU_W), F32),
        scratch_shapes=[pltpu.VMEM((TB + 2 * HALO, LRU_W), F32), pltpu.VMEM((1, LRU_W), F32)],
        compiler_params=_cparams(("parallel", "arbitrary", "arbitrary")),
        name="rglru",
    )(p, p, p, conv_w, conv_b, wg, bg, lam)


def _gelu_tanh(t):
    return 0.5 * t * (1.0 + jnp.tanh(0.7978845608028654 * (t + 0.044715 * t * t * t)))


def _merge_kernel(*refs, ctx_len, nstream):
    x_refs = refs[:nstream]
    (mctx_ref, mb_ref, hgf_ref, hgb_ref, hgg_ref, hgn_ref, hm_ref, mlac_ref, mlax_ref,
     lf_ref, lb_ref, ly_ref, g1_ref, g2_ref, g3_ref, wbr_ref, wout_ref, o_ref) = refs[nstream:]
    i = pl.program_id(1)
    nctx = ctx_len // TB
    o_mla = jnp.where(i < nctx, mlac_ref[0], mlax_ref[0])
    o = hgf_ref[0, 0] + hgb_ref[0, 0]
    ms = jnp.dot((o * o).astype(BF16), hm_ref[...], preferred_element_type=F32)
    o_hg = o * lax.rsqrt(ms + EPS) * hgn_ref[...] * _sigmoid(hgg_ref[0].astype(F32))
    o_lru = (lf_ref[0, 0] + lb_ref[0, 0]) * _gelu_tanh(ly_ref[0].astype(F32))
    y = (_sigmoid(g1_ref[0].astype(F32)) * jnp.dot(o_hg.astype(BF16), wbr_ref[0], preferred_element_type=F32)
         + _sigmoid(g2_ref[0].astype(F32)) * jnp.dot(o_mla, wbr_ref[1], preferred_element_type=F32)
         + _sigmoid(g3_ref[0].astype(F32)) * jnp.dot(o_lru.astype(BF16), wbr_ref[2], preferred_element_type=F32))
    gate = _row_mods(i, TB, ctx_len, mctx_ref, mb_ref, 2)
    o_ref[0] = _stream_tile(x_refs, i, nctx) + gate * jnp.dot(y.astype(BF16), wout_ref[...], preferred_element_type=F32)


def _merge(xs, mctx, mb, o_hg, p, hg_gain, head_mean, o_mla_c, o_mla_x, h_lru, w_br, w_out, ctx_len):
    b, lt, _ = p.shape
    nctx = ctx_len // TB
    x_specs, x_args = _stream_specs(xs, TB, nctx)

    def pcol(width, c):
        return pl.BlockSpec((1, TB, width), lambda bi, i: (bi, i, c))

    def dirspec(d):
        return pl.BlockSpec((1, 1, TB, 512), lambda bi, i: (d, bi, i, 0))

    return pl.pallas_call(
        functools.partial(_merge_kernel, ctx_len=ctx_len, nstream=len(x_args)),
        grid=(b, lt // TB),
        in_specs=x_specs + [
                  _const_spec((6, D_MODEL)),
                  pl.BlockSpec((1, 6, D_MODEL), lambda bi, i: (bi, 0, 0)),
                  dirspec(0), dirspec(0), pcol(512, 2),
                  _const_spec((1, HG_W)), _const_spec((HG_W, HG_W)),
                  pl.BlockSpec((1, TB, MLA_W), lambda bi, i: (bi, jnp.minimum(i, nctx - 1), 0)),
                  pl.BlockSpec((1, TB, MLA_W), lambda bi, i: (bi, jnp.maximum(i - nctx, 0), 0)),
                  dirspec(0), dirspec(1), pcol(512, 5),
                  pcol(D_MODEL, 3), pcol(D_MODEL, 4), pcol(D_MODEL, 5),
                  _const_spec((3, 512, D_MODEL)), _const_spec((D_MODEL, D_MODEL))],
        out_specs=pl.BlockSpec((1, TB, D_MODEL), lambda bi, i: (bi, i, 0)),
        out_shape=jax.ShapeDtypeStruct((b, lt, D_MODEL), F32),
        compiler_params=_cparams(("parallel", "arbitrary")),
        name="merge",
    )(*x_args, mctx, mb, o_hg[0], o_hg[1], p, hg_gain, head_mean, o_mla_c, o_mla_x, h_lru, h_lru, p, p, p, p, w_br, w_out)


def _first_row_of_max(vals, row, valid):
    masked = jnp.where(valid, vals, -jnp.inf)
    m = jnp.max(masked, axis=0, keepdims=True)
    idx = jnp.min(jnp.where(masked == m, row, ROUTER_PAD), axis=0, keepdims=True)
    return m, idx


def _router(h, wr_hi, wr_lo, br):
    hi = h.astype(BF16)
    lo = (h - hi.astype(F32)).astype(BF16)
    logits = (jnp.dot(hi, wr_hi, preferred_element_type=F32) + jnp.dot(lo, wr_hi, preferred_element_type=F32)
              + jnp.dot(hi, wr_lo, preferred_element_type=F32))
    biased = (logits + br).T
    logits = logits.T
    row = lax.broadcasted_iota(jnp.int32, logits.shape, 0)
    is_group = jnp.logical_and(row >= N_EXPERTS, row < N_EXPERTS + N_GROUPS)
    _, g_row = _first_row_of_max(biased, row, is_group)
    g_max, _ = _first_row_of_max(logits, row, is_group)
    g_exp = jnp.where(is_group, jnp.exp(logits - g_max), 0.0)
    g_sel_logit = jnp.sum(jnp.where(row == g_row, logits, 0.0), axis=0, keepdims=True)
    p_g = jnp.exp(g_sel_logit - g_max) / jnp.sum(g_exp, axis=0, keepdims=True)
    in_group = jnp.right_shift(row, 2) == (g_row - N_EXPERTS)
    _, i1 = _first_row_of_max(biased, row, in_group)
    _, i2 = _first_row_of_max(biased, row, jnp.logical_and(in_group, row != i1))
    l1 = jnp.sum(jnp.where(row == i1, logits, 0.0), axis=0, keepdims=True)
    l2 = jnp.sum(jnp.where(row == i2, logits, 0.0), axis=0, keepdims=True)
    lm = jnp.maximum(l1, l2)
    e1 = jnp.exp(l1 - lm)
    e2 = jnp.exp(l2 - lm)
    inv = p_g / (e1 + e2)
    comb_t = jnp.where(row == i1, e1 * inv, 0.0) + jnp.where(row == i2, e2 * inv, 0.0)
    return comb_t.T


def _moe_kernel(x_ref, mctx_ref, mb_ref, gain_ref, wrh_ref, wrl_ref, br_ref, w1_ref, w3_ref, w2_ref, o_ref, *,
                ctx_len, tm, tile0):
    i = pl.program_id(1) + tile0
    epb = N_EXPERTS // EXPERT_STEPS
    nrows = min(MOE_RB, tm)
    for rb in range(tm // nrows):
        rows = slice(rb * nrows, (rb + 1) * nrows)
        tile = i * (tm // nrows) + rb
        shift = _row_mods(tile, nrows, ctx_len, mctx_ref, mb_ref, 3)
        scale = _row_mods(tile, nrows, ctx_len, mctx_ref, mb_ref, 4)
        gate = _row_mods(tile, nrows, ctx_len, mctx_ref, mb_ref, 5)
        x = x_ref[0, rows, :]
        h = _norm_modulate(x, gain_ref[...], shift, scale)
        comb = _router(h, wrh_ref[...], wrl_ref[...], br_ref[...])
        hb = h.astype(BF16)
        acc = None
        for es in range(EXPERT_STEPS):
            parts = []
            for e in range(es * epb, (es + 1) * epb):
                h1 = jnp.dot(hb, w1_ref[e], preferred_element_type=F32)
                h3 = jnp.dot(hb, w3_ref[e], preferred_element_type=F32)
                parts.append(h1 * _sigmoid(h1) * h3 * comb[:, e:e + 1])
            y = jnp.dot(jnp.concatenate(parts, axis=-1).astype(BF16), w2_ref[es], preferred_element_type=F32)
            acc = y if acc is None else acc + y
        o_ref[0, rows, :] = x + gate * acc


def _moe(xs, mctx, mb, gain, wr_hi, wr_lo, br, w1, w3, w2, ctx_len, latent_only):
    b, lt, _ = xs.shape
    epb = N_EXPERTS // EXPERT_STEPS
    if latent_only:
        tm, tile0, rows = TB, ctx_len // TB, lt - ctx_len
    else:
        tm, tile0, rows = (MOE_TM if lt % MOE_TM == 0 else TB), 0, lt
    return pl.pallas_call(
        functools.partial(_moe_kernel, ctx_len=ctx_len, tm=tm, tile0=tile0),
        grid=(b, rows // tm),
        in_specs=[pl.BlockSpec((1, tm, D_MODEL), lambda bi, i: (bi, i + tile0, 0)),
                  _const_spec((6, D_MODEL)),
                  pl.BlockSpec((1, 6, D_MODEL), lambda bi, i: (bi, 0, 0)),
                  _const_spec((1, D_MODEL)),
                  _const_spec((D_MODEL, ROUTER_PAD)), _const_spec((D_MODEL, ROUTER_PAD)), _const_spec((1, ROUTER_PAD)),
                  _const_spec((N_EXPERTS, D_MODEL, D_EXPERT)), _const_spec((N_EXPERTS, D_MODEL, D_EXPERT)),
                  _const_spec((EXPERT_STEPS, epb * D_EXPERT, D_MODEL))],
        out_specs=pl.BlockSpec((1, tm, D_MODEL), lambda bi, i: (bi, i, 0)),
        out_shape=jax.ShapeDtypeStruct((b, rows, D_MODEL), F32),
        compiler_params=_cparams(("parallel", "arbitrary")),
        name="moe",
    )(xs, mctx, mb, gain, wr_hi, wr_lo, br, w1, w3, w2)


def _rope_tables(ctx_len, seq):
    half = MLA_ROPE // 2
    rows = seq // GRID_W
    pos_row = np.repeat(np.arange(rows, dtype=np.float32), GRID_W)
    pos_col = np.tile(np.arange(GRID_W, dtype=np.float32), rows)
    inv = (ROPE_BASE ** (-np.arange(0, half, 2, dtype=np.float32) / half)).astype(np.float32)
    ang = np.concatenate([pos_row[:, None] * inv, pos_col[:, None] * inv], axis=-1)
    cos, sin = np.cos(ang), np.sin(ang)
    cos_t = np.zeros((ctx_len + seq, HEAD_PAD), np.float32)
    sin_t = np.zeros((ctx_len + seq, HEAD_PAD), np.float32)
    cos_t[:, :MLA_NOPE] = 1.0
    cos_t[:ctx_len, MLA_NOPE:MLA_QK] = 1.0
    cos_t[ctx_len:, MLA_NOPE:MLA_NOPE + half] = cos
    cos_t[ctx_len:, MLA_NOPE + half:MLA_QK] = cos
    sin_t[ctx_len:, MLA_NOPE:MLA_NOPE + half] = -sin
    sin_t[ctx_len:, MLA_NOPE + half:MLA_QK] = sin
    return cos_t, sin_t


def _rope_key_selectors():
    half = MLA_ROPE // 2
    kr0 = MLA_Q_RANK + MLA_KV_RANK
    sk1 = np.zeros((512, MLA_HEADS * HEAD_PAD), np.float32)
    sk2 = np.zeros((512, MLA_HEADS * HEAD_PAD), np.float32)
    for h in range(MLA_HEADS):
        for i in range(MLA_ROPE):
            sk1[kr0 + i, h * HEAD_PAD + MLA_NOPE + i] = 1.0
            sk2[kr0 + (i + half) % MLA_ROPE, h * HEAD_PAD + MLA_NOPE + i] = 1.0
    return sk1.astype(BF16), sk2.astype(BF16)


def _np_block_diag(block, n):
    a, bb = block.shape
    out = np.zeros((n * a, n * bb), np.float32)
    for i in range(n):
        out[i * a:(i + 1) * a, i * bb:(i + 1) * bb] = block
    return out


def _swap_rope_halves(t):
    half = MLA_ROPE // 2
    return jnp.concatenate([jnp.zeros_like(t[..., :MLA_NOPE]), t[..., MLA_NOPE + half:], t[..., MLA_NOPE:MLA_NOPE + half]],
                           axis=-1)


def _pad_heads(t):
    pad = [(0, 0)] * (t.ndim - 1) + [(0, HEAD_PAD - t.shape[-1])]
    t = jnp.pad(t, pad)
    return t.reshape(t.shape[:-2] + (t.shape[-2] * HEAD_PAD,))


def _mla_weights(w_uq, w_ukv, gq, gk):
    wq = w_uq.reshape(MLA_Q_RANK, MLA_HEADS, MLA_QK)
    wq1 = _pad_heads(wq).astype(BF16)
    wq2 = _pad_heads(_swap_rope_halves(wq)).astype(BF16)
    wkv = w_ukv.reshape(MLA_KV_RANK, MLA_HEADS, MLA_NOPE + MLA_V)
    wk1 = _pad_heads(wkv[..., :MLA_NOPE]).astype(BF16)
    wv = _pad_heads(wkv[..., MLA_NOPE:]).astype(BF16)

    def gains(g):
        g1 = jnp.pad(g, (0, HEAD_PAD - MLA_QK))
        g2 = jnp.pad(_swap_rope_halves(g), (0, HEAD_PAD - MLA_QK))
        return jnp.stack([g1, g2], axis=0)

    return wq1, wq2, wk1, wv, gains(gq), gains(gk)


def _block_diag(w):
    n, a, bb = w.shape
    eye = jnp.eye(n, dtype=w.dtype)
    return (eye[:, None, :, None] * w[:, :, None, :]).reshape(n * a, n * bb)


def _pad_rows(t, rows=8):
    return jnp.pad(t, ((0, rows - t.shape[0]), (0, 0)))


def kernel(x, c, ctx, c_ctx, w_mod, b_mod, norm_mix, norm_ffn, w_in, hg_lb, hg_norm, mla_q_norm, mla_kv_norm, mla_w_uq, mla_w_ukv, mla_qk_gain_q, mla_qk_gain_k, lru_conv_w, lru_conv_b, lru_wa, lru_ba, lru_wx, lru_bx, lru_lambda, w_br_hg, w_br_mla, w_br_lru, w_out, moe_w_rg, moe_b_rg, moe_w_re, moe_b_re, moe_w1, moe_w3, moe_w2):
    bsz, seq, _ = x.shape
    ctx_len = ctx.shape[1]
    depth = w_in.shape[0]
    assert seq % TB == 0 and ctx_len % TB == 0 and seq % GRID_W == 0 and bsz < 8
    nctx = ctx_len // TB

    xs = (ctx, x)
    cos_t, sin_t = _rope_tables(ctx_len, seq)
    cvec = jnp.zeros((8, D_MODEL), F32).at[:bsz].set(c).at[bsz].set(c_ctx)

    lb_cs = jnp.cumsum(jax.nn.softmax(hg_lb.astype(F32), axis=0), axis=0)
    lb_all = lb_cs - lb_cs[0:1]
    tri_lo = _np_block_diag(np.tril(np.ones((HG_CHUNK, HG_CHUNK), np.float32)), TB // HG_CHUNK)
    tri = np.stack([tri_lo, tri_lo.T], axis=0).astype(BF16)
    head_ones = _np_block_diag(np.ones((HG_DK, HG_DK), np.float32), HG_HEADS)
    head_mean = (head_ones / HG_DK).astype(BF16)
    sk1, sk2 = _rope_key_selectors()

    w_all = _win_layout(w_in)

    mods = _modulation(cvec, w_mod, b_mod).reshape(depth, 8, 6, D_MODEL)

    for l in range(depth):
        mctx, mb = mods[l, bsz], mods[l, :bsz]

        f, p = _inproj(xs, mctx, mb, norm_mix[l][None, :], w_all, l, ctx_len)

        lb = lb_all[l]
        lbc = jnp.stack([jnp.log(lb), jnp.log1p(-lb), 1.0 - lb] + [jnp.zeros_like(lb)] * 5, axis=1)
        o_hg = _hgrn(p, f, tri, lbc, head_ones, nctx)

        wq1, wq2, wk1, wv, gq, gk = _mla_weights(mla_w_uq[l], mla_w_ukv[l], mla_qk_gain_q[l], mla_qk_gain_k[l])
        qt, kh, vt, ksq = _mla_prep(p, cos_t, sin_t, mla_q_norm[l][None, :], mla_kv_norm[l][None, :], gq, gk,
                                    wq1, wq2, wk1, sk1, sk2, wv, nctx)
        kmax = jnp.sqrt(jnp.max(ksq, axis=(1, 3)))
        tq = next(t for t in (512, 256) if seq % t == 0)
        o_mla_x = _flash(kmax, qt, kh, vt, seq, ctx_len + seq, tq, 0, "mla_attention")
        if l < depth - 1:
            o_mla_c = _flash(kmax, qt, kh, vt, ctx_len, ctx_len, TB, seq, "mla_attention_ctx")
        else:
            o_mla_c = o_mla_x[:, :ctx_len]

        wg = jnp.stack([jnp.concatenate([_block_diag(lru_wa[l, d]), _block_diag(lru_wx[l, d])], axis=-1)
                        for d in range(2)], axis=0).astype(BF16)
        bg = jnp.concatenate([lru_ba[l], lru_bx[l]], axis=-1)[:, None, :] * jnp.ones((1, 8, 1), F32)
        lam = lru_lambda[l][:, None, :] * jnp.ones((1, 8, 1), F32)
        h_lru = _lru(p, _pad_rows(lru_conv_w[l]), lru_conv_b[l][None, :], wg, bg, lam, nctx)

        w_br = jnp.stack([w_br_hg[l], w_br_mla[l], w_br_lru[l]], axis=0).astype(BF16)
        hg_gain = jnp.tile(hg_norm[l], HG_HEADS)[None, :]
        xs = _merge(xs, mctx, mb, o_hg, p, hg_gain, head_mean, o_mla_c, o_mla_x, h_lru, w_br, w_out[l].astype(BF16), ctx_len)

        wr = jnp.pad(jnp.concatenate([moe_w_re[l], moe_w_rg[l]], axis=-1), ((0, 0), (0, ROUTER_PAD - N_EXPERTS - N_GROUPS)))
        br = jnp.pad(jnp.concatenate([moe_b_re[l], moe_b_rg[l]]), (0, ROUTER_PAD - N_EXPERTS - N_GROUPS))[None, :]
        w2 = moe_w2[l].astype(BF16).reshape(EXPERT_STEPS, N_EXPERTS // EXPERT_STEPS * D_EXPERT, D_MODEL)
        wr_hi = wr.astype(BF16)
        wr_lo = (wr - wr_hi.astype(F32)).astype(BF16)
        xs = _moe(xs, mctx, mb, norm_ffn[l][None, :], wr_hi, wr_lo, br, moe_w1[l].astype(BF16), moe_w3[l].astype(BF16), w2,
                  ctx_len, latent_only=(l == depth - 1))

    return xs
```

```python
import functools

import numpy as np
import jax
import jax.numpy as jnp
from jax import lax
from jax.experimental import pallas as pl
from jax.experimental.pallas import tpu as pltpu

F32 = jnp.float32
BF16 = jnp.bfloat16
HIGHEST = lax.Precision.HIGHEST

D_MODEL = 1024
GRID_W = 64
EPS = 1e-6

HG_HEADS = 8
HG_DK = 64
HG_W = 512
HG_CHUNK = 64
HG_MID = HG_CHUNK // 2
HG_MAX_LOG_RANGE = 80.0

MLA_HEADS = 8
MLA_Q_RANK = 256
MLA_KV_RANK = 128
MLA_NOPE = 64
MLA_ROPE = 32
MLA_V = 64
MLA_QK = MLA_NOPE + MLA_ROPE
MLA_W = MLA_HEADS * MLA_V
HEAD_PAD = 128
V_ROWS = 80
SCORE_SCALE = MLA_QK ** -0.5 * 1.4426950408889634
MAX_UNSHIFTED_SCORE = 57.0
ROPE_BASE = 10000.0

LRU_W = 512
LRU_BLOCKS = 8
LRU_BD = LRU_W // LRU_BLOCKS
CONV_W = 4
LRU_C = 8.0

N_GROUPS = 4
EXP_PER_GROUP = 4
N_EXPERTS = N_GROUPS * EXP_PER_GROUP
D_EXPERT = 256
EXPERT_STEPS = 4
ROUTER_PAD = 128
MOE_TM = 768
MOE_RB = 256
INPROJ_TM = 768

TB = 256
HALO = 16
SCAN_GROUP = 8
P_WIDTH = 6144
F_WIDTH = 1024
VMEM_LIMIT = 56 * 1024 * 1024

NT_DIMS = (((1,), (1,)), ((), ()))


def _cparams(sem):
    return pltpu.CompilerParams(dimension_semantics=sem, vmem_limit_bytes=VMEM_LIMIT)


def _const_spec(shape):
    nd = len(shape)
    return pl.BlockSpec(shape, lambda *_: (0,) * nd, pipeline_mode=pl.Buffered(1))


def _sigmoid(t):
    return 0.5 * jnp.tanh(0.5 * t) + 0.5


def _scan_block(j, d, nctx, nblk):
    fwd = j
    bwd = jnp.where(j < nctx, nctx - 1 - j, nblk - 1 - (j - nctx))
    return jnp.where(d == 0, fwd, bwd)


def _mod_kernel(c_ref, w_ref, b_ref, o_ref):
    c = c_ref[...]
    s = c * _sigmoid(c)
    o_ref[0] = jnp.dot(s, w_ref[0], precision=HIGHEST, preferred_element_type=F32) + b_ref[0]


def _modulation(cvec, w_mod, b_mod):
    depth, _, n = w_mod.shape
    tn = 1024
    return pl.pallas_call(
        _mod_kernel,
        grid=(depth, n // tn),
        in_specs=[pl.BlockSpec((8, D_MODEL), lambda l, j: (0, 0)),
                  pl.BlockSpec((1, D_MODEL, tn), lambda l, j: (l, 0, j)),
                  pl.BlockSpec((1, 1, tn), lambda l, j: (l, 0, j))],
        out_specs=pl.BlockSpec((1, 8, tn), lambda l, j: (l, 0, j)),
        out_shape=jax.ShapeDtypeStruct((depth, 8, n), F32),
        compiler_params=_cparams(("arbitrary", "arbitrary")),
        name="modulation",
    )(cvec, w_mod, b_mod.reshape(depth, 1, n))


def _row_mods(i, tm, ctx_len, mctx_ref, mb_ref, k):
    row = i * tm + lax.broadcasted_iota(jnp.int32, (tm, 1), 0)
    return jnp.where(row < ctx_len, mctx_ref[k:k + 1, :], mb_ref[0, k:k + 1, :])


def _norm_modulate(x, gain, shift, scale):
    ms = jnp.mean(x * x, axis=-1, keepdims=True)
    xn = x * lax.rsqrt(ms + EPS) * gain
    return xn * (1.0 + scale) + shift


def _stream_specs(xs, tm, nctx):
    if isinstance(xs, tuple):
        assert tm == TB
        return ([pl.BlockSpec((1, tm, D_MODEL), lambda bi, i: (bi, jnp.minimum(i, nctx - 1), 0)),
                 pl.BlockSpec((1, tm, D_MODEL), lambda bi, i: (bi, jnp.maximum(i - nctx, 0), 0))], list(xs))
    return [pl.BlockSpec((1, tm, D_MODEL), lambda bi, i: (bi, i, 0))], [xs]


def _stream_tile(x_refs, i, nctx):
    if len(x_refs) == 2:
        return jnp.where(i < nctx, x_refs[0][0], x_refs[1][0])
    return x_refs[0][0]


def _win_layout_kernel(w_ref, o_ref):
    def put(dst, src, width):
        o_ref[0, :, dst:dst + width] = w_ref[0, :, src:src + width].astype(BF16)

    mla_w = MLA_Q_RANK + MLA_KV_RANK + MLA_ROPE
    mla0 = 5 * HG_W
    put(0, HG_W, 2 * HG_W)
    put(F_WIDTH, 0, HG_W)
    put(F_WIDTH + HG_W, 3 * HG_W, 2 * HG_W)
    put(F_WIDTH + 3 * HG_W, mla0, mla_w)
    o_ref[0, :, F_WIDTH + 3 * HG_W + mla_w:F_WIDTH + 4 * HG_W] = jnp.zeros((o_ref.shape[1], HG_W - mla_w), BF16)
    rest = w_ref.shape[2] - mla0 - mla_w
    put(F_WIDTH + 4 * HG_W, mla0 + mla_w, rest)


def _win_layout(w_in):
    depth, d, n = w_in.shape
    tr = 256
    return pl.pallas_call(
        _win_layout_kernel,
        grid=(depth, d // tr),
        in_specs=[pl.BlockSpec((1, tr, n), lambda l, i: (l, i, 0))],
        out_specs=pl.BlockSpec((1, tr, F_WIDTH + P_WIDTH), lambda l, i: (l, i, 0)),
        out_shape=jax.ShapeDtypeStruct((depth, d, F_WIDTH + P_WIDTH), BF16),
        compiler_params=_cparams(("arbitrary", "arbitrary")),
        name="w_in_layout",
    )(w_in)


def _inproj_kernel(*refs, ctx_len, tm, nstream):
    x_refs = refs[:nstream]
    mctx_ref, mb_ref, gain_ref, w_ref, f_ref, p_ref = refs[nstream:]
    w_ref = w_ref.at[0]
    i = pl.program_id(1)
    shift = _row_mods(i, tm, ctx_len, mctx_ref, mb_ref, 0)
    scale = _row_mods(i, tm, ctx_len, mctx_ref, mb_ref, 1)
    h = _norm_modulate(_stream_tile(x_refs, i, ctx_len // TB), gain_ref[...], shift, scale).astype(BF16)
    f_ref[0] = jnp.dot(h, w_ref[:, 0:F_WIDTH], preferred_element_type=F32)
    cw = 512
    for j in range(P_WIDTH // cw):
        lo = F_WIDTH + j * cw
        p_ref[0, :, j * cw:(j + 1) * cw] = jnp.dot(
            h, w_ref[:, lo:lo + cw], preferred_element_type=F32).astype(BF16)


def _inproj(xs, mctx, mb, gain, w_all, layer, ctx_len):
    split = isinstance(xs, tuple)
    b = (xs[0] if split else xs).shape[0]
    lt = sum(t.shape[1] for t in xs) if split else xs.shape[1]
    tm = INPROJ_TM if lt % INPROJ_TM == 0 and not split else TB
    x_specs, x_args = _stream_specs(xs, tm, ctx_len // TB)
    return pl.pallas_call(
        functools.partial(_inproj_kernel, ctx_len=ctx_len, tm=tm, nstream=len(x_args)),
        grid=(b, lt // tm),
        in_specs=x_specs + [
                  _const_spec((6, D_MODEL)),
                  pl.BlockSpec((1, 6, D_MODEL), lambda bi, i: (bi, 0, 0)),
                  _const_spec((1, D_MODEL)),
                  pl.BlockSpec((1, D_MODEL, F_WIDTH + P_WIDTH), lambda bi, i: (layer, 0, 0), pipeline_mode=pl.Buffered(1))],
        out_specs=[pl.BlockSpec((1, tm, F_WIDTH), lambda bi, i: (bi, i, 0)),
                   pl.BlockSpec((1, tm, P_WIDTH), lambda bi, i: (bi, i, 0))],
        out_shape=[jax.ShapeDtypeStruct((b, lt, F_WIDTH), F32),
                   jax.ShapeDtypeStruct((b, lt, P_WIDTH), BF16)],
        compiler_params=_cparams(("parallel", "arbitrary")),
        name="inproj",
    )(*x_args, mctx, mb, gain, w_all)


def _hgrn_direction(q_ref, v_ref, f_ref, tri_ref, lbc_ref, bd_ref, o_ref, st_ref, c_scr, k_scr, v_scr, reverse):
    d = 1 if reverse else 0
    nch = TB // HG_CHUNK
    npair = HG_W // 128
    first, last = (HG_CHUNK - 1, 0) if reverse else (0, HG_CHUNK - 1)

    log_lb = lbc_ref[d, 0:1, :]
    log_1mlb = lbc_ref[d, 1:2, :]
    one_mlb = lbc_ref[d, 2:3, :]
    z = f_ref[0]
    q = q_ref[0].astype(F32)
    v = v_ref[0].astype(F32)
    e = jnp.exp(-jnp.abs(z))
    log_sig = jnp.minimum(z, 0.0) - jnp.log(1.0 + e)
    t = log_1mlb + log_sig
    logf = jnp.maximum(log_lb, t) + jnp.log(1.0 + jnp.exp(-jnp.abs(log_lb - t)))
    k = one_mlb * (jnp.where(z >= 0, e, 1.0) / (1.0 + e))
    hi = logf.astype(BF16)
    lo = (logf - hi.astype(F32)).astype(BF16)
    tri = tri_ref[d]
    c = jnp.dot(tri, hi, preferred_element_type=F32) + jnp.dot(tri, lo, preferred_element_type=F32)

    span = jnp.zeros((1, HG_W), F32)
    for ci in range(nch):
        r0 = ci * HG_CHUNK
        c_mid = c[r0 + HG_MID:r0 + HG_MID + 1, :]
        span = jnp.maximum(span, jnp.maximum(c[r0 + first:r0 + first + 1, :] - c_mid, c_mid - c[r0 + last:r0 + last + 1, :]))
    span = jnp.max(span)

    lane = lax.broadcasted_iota(jnp.int32, (HG_CHUNK, HG_W), 1)
    low_head = (lane & (HG_DK * 2 - 1)) < HG_DK
    ri = lax.broadcasted_iota(jnp.int32, (HG_CHUNK, 128), 0)
    si = lax.broadcasted_iota(jnp.int32, (HG_CHUNK, 128), 1) & (HG_DK - 1)
    visited = (si >= ri) if reverse else (si <= ri)
    r2 = lax.broadcasted_iota(jnp.int32, (128, 128), 0)
    l2 = lax.broadcasted_iota(jnp.int32, (128, 128), 1)
    same_head = (r2 < HG_DK) == (l2 < HG_DK)
    rowi = lax.broadcasted_iota(jnp.int32, (HG_CHUNK, 1), 0)

    def split_heads(t, transpose=False):
        top = jnp.where(low_head, t, 0.0)
        bot = jnp.where(low_head, 0.0, t)
        blocks = [jnp.concatenate([top[:, 128 * g:128 * (g + 1)], bot[:, 128 * g:128 * (g + 1)]], axis=0) for g in range(npair)]
        return [(blk.T if transpose else blk).astype(BF16) for blk in blocks]

    def exact_intra(qq, kk, vv, cc):
        c_scr[...] = cc
        k_scr[...] = kk
        v_scr[...] = vv
        bd = bd_ref[...]

        def key_row(s, acc):
            c_s = c_scr[pl.ds(s, 1), :]
            w = qq * jnp.exp(jnp.minimum(cc - c_s, 0.0)) * k_scr[pl.ds(s, 1), :]
            ws = jnp.dot(w, bd, precision=HIGHEST, preferred_element_type=F32)
            ok = (rowi <= s) if reverse else (rowi >= s)
            return acc + jnp.where(ok, ws, 0.0) * v_scr[pl.ds(s, 1), :]

        return lax.fori_loop(0, HG_CHUNK, key_row, jnp.zeros((HG_CHUNK, HG_W), F32))

    def chunk(ci, exact):
        r0 = ci * HG_CHUNK
        rows = slice(r0, r0 + HG_CHUNK)
        cc, qq, kk, vv = c[rows], q[rows], k[rows], v[rows]
        c_mid = cc[HG_MID:HG_MID + 1, :]
        c_end = cc[last:last + 1, :]
        qs = (qq * jnp.exp(cc)).astype(BF16)
        ks = (kk * jnp.exp(c_end - cc)).astype(BF16)
        dec = jnp.exp(c_end)
        if exact:
            intra = exact_intra(qq, kk, vv, cc)
        else:
            qd = (qq * jnp.exp(cc - c_mid)).astype(BF16)
            kd_blocks = split_heads(kk * jnp.exp(c_mid - cc), transpose=True)
            v_blocks = split_heads(vv)
        pieces = []
        for g in range(npair):
            ls = slice(128 * g, 128 * (g + 1))
            st = st_ref[d, g]
            o_g = jnp.dot(qs[:, ls], st.T.astype(BF16), preferred_element_type=F32)
            if not exact:
                a = jnp.dot(qd[:, ls], kd_blocks[g], preferred_element_type=F32)
                a = jnp.where(visited, a, 0.0).astype(BF16)
                o_g = o_g + jnp.dot(a, v_blocks[g], preferred_element_type=F32)
            vt = vv[:, ls].T.astype(BF16)
            kv = jnp.dot(vt, ks[:, ls], preferred_element_type=F32)
            st_ref[d, g] = st * dec[:, ls] + jnp.where(same_head, kv, 0.0)
            pieces.append(o_g)
        o = jnp.concatenate(pieces, axis=-1)
        o_ref[0, 0, rows, :] = o + intra if exact else o

    order = list(range(nch - 1, -1, -1) if reverse else range(nch))
    return span, lambda n, exact: chunk(order[n], exact)


def _hgrn_kernel(qf_ref, vf_ref, ff_ref, qb_ref, vb_ref, fb_ref, tri_ref, lbc_ref, bd_ref, of_ref, ob_ref,
                 st_ref, c_scr, k_scr, v_scr):
    @pl.when(pl.program_id(1) == 0)
    def _():
        st_ref[...] = jnp.zeros_like(st_ref)

    span_f, chunk_f = _hgrn_direction(qf_ref, vf_ref, ff_ref, tri_ref, lbc_ref, bd_ref, of_ref, st_ref,
                                      c_scr, k_scr, v_scr, False)
    span_b, chunk_b = _hgrn_direction(qb_ref, vb_ref, fb_ref, tri_ref, lbc_ref, bd_ref, ob_ref, st_ref,
                                      c_scr, k_scr, v_scr, True)
    span = jnp.maximum(span_f, span_b)

    def run(exact):
        for n in range(TB // HG_CHUNK):
            chunk_f(n, exact)
            chunk_b(n, exact)

    @pl.when(span <= HG_MAX_LOG_RANGE)
    def _matmul_form():
        run(False)

    @pl.when(span > HG_MAX_LOG_RANGE)
    def _exact_form():
        run(True)


def _hgrn(p, f, tri, lbc, bd, nctx):
    b, lt, _ = p.shape
    nblk = lt // TB

    def spec(d, col):
        return pl.BlockSpec((1, TB, HG_W), lambda bi, j: (bi, _scan_block(j, d, nctx, nblk), col))

    def out_spec(d):
        return pl.BlockSpec((1, 1, TB, HG_W), lambda bi, j: (0, bi, _scan_block(j, d, nctx, nblk), 0))

    out = jax.ShapeDtypeStruct((1, b, lt, HG_W), F32)
    return pl.pallas_call(
        _hgrn_kernel,
        grid=(b, nblk),
        in_specs=[spec(0, 0), spec(0, 1), spec(0, 0),
                  spec(1, 0), spec(1, 1), spec(1, 1),
                  _const_spec((2, TB, TB)), _const_spec((2, 8, HG_W)), _const_spec((HG_W, HG_W))],
        out_specs=[out_spec(0), out_spec(1)],
        out_shape=[out, out],
        scratch_shapes=[pltpu.VMEM((2, HG_W // 128, 128, 128), F32),
                        pltpu.VMEM((HG_CHUNK, HG_W), F32),
                        pltpu.VMEM((HG_CHUNK, HG_W), F32),
                        pltpu.VMEM((HG_CHUNK, HG_W), F32)],
        compiler_params=_cparams(("parallel", "arbitrary")),
        name="hgrn2",
    )(p, p, f, p, p, f, tri, lbc, bd)


def _rms_rows(t, width):
    return lax.rsqrt(jnp.sum(t * t, axis=-1, keepdims=True) * (1.0 / width) + EPS)


def _mla_prep_kernel(pm_ref, cos_ref, sin_ref, qn_ref, kvn_ref, gq_ref, gk_ref,
                     wq1_ref, wq2_ref, wk1_ref, sk1_ref, sk2_ref, wv_ref,
                     qt_ref, k_ref, vt_ref, qsq_ref, ksq_ref):
    pm = pm_ref[0]
    dq = pm[:, 0:MLA_Q_RANK].astype(F32)
    dkv = pm[:, MLA_Q_RANK:MLA_Q_RANK + MLA_KV_RANK].astype(F32)
    dqn = (dq * _rms_rows(dq, MLA_Q_RANK) * qn_ref[...]).astype(BF16)
    dkvn = (dkv * _rms_rows(dkv, MLA_KV_RANK) * kvn_ref[...]).astype(BF16)
    q1 = jnp.dot(dqn, wq1_ref[...], preferred_element_type=F32)
    q2 = jnp.dot(dqn, wq2_ref[...], preferred_element_type=F32)
    k1 = (jnp.dot(dkvn, wk1_ref[...], preferred_element_type=F32)
          + jnp.dot(pm, sk1_ref[...], preferred_element_type=F32))
    k2 = jnp.dot(pm, sk2_ref[...], preferred_element_type=F32)
    vv = jnp.dot(dkvn, wv_ref[...], preferred_element_type=F32)
    cos = cos_ref[...]
    sin = sin_ref[...]
    gq_c = gq_ref[0:1, :] * cos
    gq_s = gq_ref[1:2, :] * sin
    gk_c = gk_ref[0:1, :] * cos
    gk_s = gk_ref[1:2, :] * sin
    vrow = lax.broadcasted_iota(jnp.int32, (V_ROWS, TB), 0)
    qn, kn = [], []

    def max_sq_norm(t):
        return jnp.broadcast_to(jnp.max(jnp.sum(t * t, axis=-1, keepdims=True), axis=0, keepdims=True), (1, HEAD_PAD))

    for h in range(MLA_HEADS):
        sl = slice(HEAD_PAD * h, HEAD_PAD * (h + 1))
        qh = q1[:, sl]
        qo = (qh * gq_c + q2[:, sl] * gq_s) * (_rms_rows(qh, MLA_QK) * SCORE_SCALE)
        qt_ref[0, h] = qo.T.astype(BF16)
        qn.append(max_sq_norm(qo))
        kh = k1[:, sl]
        ko = (kh * gk_c + k2[:, sl] * gk_s) * _rms_rows(kh, MLA_QK)
        k_ref[0, h] = ko.astype(BF16)
        kn.append(max_sq_norm(ko))
        vt = vv[:, sl].T[0:V_ROWS, :]
        vt_ref[0, h] = jnp.where(vrow == MLA_V, 1.0, vt).astype(BF16)
    qsq_ref[0, 0] = jnp.concatenate(qn, axis=0)
    ksq_ref[0, 0] = jnp.concatenate(kn, axis=0)


def _mla_prep(p, cos_t, sin_t, qn, kvn, gq, gk, wq1, wq2, wk1, sk1, sk2, wv, nctx):
    b, lt, _ = p.shape
    hw = MLA_HEADS * HEAD_PAD
    nblk = lt // TB
    return pl.pallas_call(
        _mla_prep_kernel,
        grid=(b, lt // TB),
        in_specs=[pl.BlockSpec((1, TB, 512), lambda bi, i: (bi, i, 3)),
                  pl.BlockSpec((TB, HEAD_PAD), lambda bi, i: (i, 0)),
                  pl.BlockSpec((TB, HEAD_PAD), lambda bi, i: (i, 0)),
                  _const_spec((1, MLA_Q_RANK)), _const_spec((1, MLA_KV_RANK)),
                  _const_spec((2, HEAD_PAD)), _const_spec((2, HEAD_PAD)),
                  _const_spec((MLA_Q_RANK, hw)), _const_spec((MLA_Q_RANK, hw)),
                  _const_spec((MLA_KV_RANK, hw)), _const_spec((512, hw)), _const_spec((512, hw)),
                  _const_spec((MLA_KV_RANK, hw))],
        out_specs=[pl.BlockSpec((1, MLA_HEADS, HEAD_PAD, TB), lambda bi, i: (bi, 0, 0, (i + nblk - nctx) % nblk)),
                   pl.BlockSpec((1, MLA_HEADS, TB, HEAD_PAD), lambda bi, i: (bi, 0, i, 0)),
                   pl.BlockSpec((1, MLA_HEADS, V_ROWS, TB), lambda bi, i: (bi, 0, 0, i)),
                   pl.BlockSpec((1, 1, MLA_HEADS, HEAD_PAD), lambda bi, i: (bi, i, 0, 0)),
                   pl.BlockSpec((1, 1, MLA_HEADS, HEAD_PAD), lambda bi, i: (bi, i, 0, 0))],
        out_shape=[jax.ShapeDtypeStruct((b, MLA_HEADS, HEAD_PAD, lt), BF16),
                   jax.ShapeDtypeStruct((b, MLA_HEADS, lt, HEAD_PAD), BF16),
                   jax.ShapeDtypeStruct((b, MLA_HEADS, V_ROWS, lt), BF16),
                   jax.ShapeDtypeStruct((b, lt // TB, MLA_HEADS, HEAD_PAD), F32),
                   jax.ShapeDtypeStruct((b, lt // TB, MLA_HEADS, HEAD_PAD), F32)],
        compiler_params=_cparams(("parallel", "arbitrary")),
        name="mla_prep",
    )(p, cos_t, sin_t, qn, kvn, gq, gk, wq1, wq2, wk1, sk1, sk2, wv)


def _attend_bounded(qts, k_ref, vt_ref, s_scr, nkeys, tk):
    n = nkeys // tk

    def scores(kb, slot):
        for hh in range(2):
            s_scr[slot, hh, 0:tk, :] = jnp.dot(k_ref[0, hh, kb * tk:(kb + 1) * tk, :], qts[hh], preferred_element_type=F32)

    def accumulate(kb, slot, accs):
        out = []
        for hh in range(2):
            p = jnp.exp2(s_scr[slot, hh, 0:tk, :]).astype(BF16)
            pv = jnp.dot(vt_ref[0, hh, :, kb * tk:(kb + 1) * tk], p, preferred_element_type=F32)
            out.append(pv if accs is None else accs[hh] + pv)
        return out

    scores(0, 0)
    accs = None
    for kb in range(n):
        if kb + 1 < n:
            scores(kb + 1, (kb + 1) % 2)
        accs = accumulate(kb, kb % 2, accs)
    return accs


def _attend_online(qts, k_ref, vt_ref, nkeys, tk):
    def kv_step(kb, carry):
        r0 = pl.multiple_of(kb * tk, tk)
        out = []
        for hh in range(2):
            m, acc = carry[hh]
            s = jnp.dot(k_ref[0, hh, pl.ds(r0, tk), :], qts[hh], preferred_element_type=F32)
            m_new = jnp.maximum(m, jnp.max(s, axis=0, keepdims=True))
            p = jnp.exp2(s - m_new).astype(BF16)
            acc = jnp.exp2(m - m_new) * acc + jnp.dot(vt_ref[0, hh, :, pl.ds(r0, tk)], p, preferred_element_type=F32)
            out.append((m_new, acc))
        return tuple(out)

    tq = qts[0].shape[1]
    init = (jnp.full((1, tq), -jnp.inf, F32), jnp.zeros((V_ROWS, tq), F32))
    res = lax.fori_loop(0, nkeys // tk, kv_step, (init, init))
    return res[0][1], res[1][1]


def _flash_kernel(bound_ref, qt_ref, k_ref, vt_ref, o_ref, s_scr, *, nkeys, tk):
    qts = [qt_ref[0, hh] for hh in range(2)]
    worst = bound_ref[pl.program_id(0), pl.program_id(1)]

    def finish(accs):
        outs = [(acc[0:MLA_V, :] / acc[MLA_V:MLA_V + 1, :]).T for acc in accs]
        o_ref[0] = jnp.concatenate(outs, axis=-1).astype(BF16)

    @pl.when(worst <= MAX_UNSHIFTED_SCORE)
    def _():
        finish(_attend_bounded(qts, k_ref, vt_ref, s_scr, nkeys, tk))

    @pl.when(worst > MAX_UNSHIFTED_SCORE)
    def _():
        finish(_attend_online(qts, k_ref, vt_ref, nkeys, tk))


def _flash(bound, qt, k, vt, nq, nkeys, tq, q_col0, name):
    b = k.shape[0]
    tk = next(t for t in (768, 512, 256) if nkeys % t == 0)
    cb0 = q_col0 // tq
    return pl.pallas_call(
        functools.partial(_flash_kernel, nkeys=nkeys, tk=tk),
        grid=(b, MLA_HEADS // 2, nq // tq),
        in_specs=[pl.BlockSpec(memory_space=pltpu.SMEM),
                  pl.BlockSpec((1, 2, HEAD_PAD, tq), lambda bi, g, i: (bi, g, 0, cb0 + i)),
                  pl.BlockSpec((1, 2, nkeys, HEAD_PAD), lambda bi, g, i: (bi, g, 0, 0)),
                  pl.BlockSpec((1, 2, V_ROWS, nkeys), lambda bi, g, i: (bi, g, 0, 0))],
        out_specs=pl.BlockSpec((1, tq, HEAD_PAD), lambda bi, g, i: (bi, i, g)),
        out_shape=jax.ShapeDtypeStruct((b, nq, MLA_W), BF16),
        scratch_shapes=[pltpu.VMEM((2, 2, tk, tq), F32)],
        compiler_params=_cparams(("parallel", "parallel", "arbitrary")),
        name=name,
    )(bound, qt, k, vt)


def _lru_kernel(x_ref, xp_ref, xn_ref, cw_ref, cb_ref, wg_ref, bg_ref, lam_ref, o_ref, xe_scr, h_scr, *, nctx, nblk):
    d = pl.program_id(1)
    j = pl.program_id(2)
    blk = _scan_block(j, d, nctx, nblk)

    @pl.when(j == 0)
    def _():
        h_scr[...] = jnp.zeros_like(h_scr)

    keep_prev = jnp.logical_and(blk != 0, blk != nctx)
    keep_next = jnp.logical_and(blk != nctx - 1, blk != nblk - 1)
    xe_scr[0:HALO, :] = jnp.where(keep_prev, xp_ref[0].astype(F32), 0.0)
    xe_scr[HALO:HALO + TB, :] = x_ref[0].astype(F32)
    xe_scr[HALO + TB:2 * HALO + TB, :] = jnp.where(keep_next, xn_ref[0].astype(F32), 0.0)
    left = CONV_W // 2
    u = jnp.broadcast_to(cb_ref[...], (TB, LRU_W))
    for tap in range(CONV_W):
        u = u + xe_scr[pl.ds(HALO - left + tap, TB), :] * cw_ref[tap:tap + 1, :]

    g = jnp.dot(u.astype(BF16), wg_ref[0], preferred_element_type=F32) + bg_ref[0, 0:1, :]
    r = _sigmoid(g[:, 0:LRU_W])
    ig = _sigmoid(g[:, LRU_W:2 * LRU_W])
    neg_lam = -lam_ref[0, 0:1, :]
    softplus = jnp.maximum(neg_lam, 0.0) + jnp.log1p(jnp.exp(-jnp.abs(neg_lam)))
    a = jnp.exp(-LRU_C * r * softplus)
    bb = jnp.sqrt(1.0 - a * a) * (ig * u)
    def scan(forward):
        ngroups = TB // SCAN_GROUP
        aa = a.reshape(ngroups, SCAN_GROUP, LRU_W)
        hh = bb.reshape(ngroups, SCAN_GROUP, LRU_W)
        rg = lax.broadcasted_iota(jnp.int32, (1, SCAN_GROUP, 1), 1)
        k = 1
        while k < SCAN_GROUP:
            shift = k if forward else SCAN_GROUP - k
            valid = (rg >= k) if forward else (rg < SCAN_GROUP - k)
            a_sh = jnp.where(valid, pltpu.roll(aa, shift, 1), 1.0)
            h_sh = jnp.where(valid, pltpu.roll(hh, shift, 1), 0.0)
            hh = aa * h_sh + hh
            aa = aa * a_sh
            k *= 2
        carry = h_scr[...]
        last = SCAN_GROUP - 1 if forward else 0
        for g in (range(ngroups) if forward else range(ngroups - 1, -1, -1)):
            hg = hh[g] + aa[g] * carry
            o_ref[0, 0, g * SCAN_GROUP:(g + 1) * SCAN_GROUP, :] = hg
            carry = hg[last:last + 1, :]
        h_scr[...] = carry

    @pl.when(d == 0)
    def _():
        scan(True)

    @pl.when(d == 1)
    def _():
        scan(False)


def _lru(p, conv_w, conv_b, wg, bg, lam, nctx):
    b, lt, _ = p.shape
    nblk = lt // TB
    hpb = TB // HALO
    nh = lt // HALO

    def blk(d, j):
        return _scan_block(j, d, nctx, nblk)

    return pl.pallas_call(
        functools.partial(_lru_kernel, nctx=nctx, nblk=nblk),
        grid=(b, 2, nblk),
        in_specs=[pl.BlockSpec((1, TB, LRU_W), lambda bi, d, j: (bi, blk(d, j), 4)),
                  pl.BlockSpec((1, HALO, LRU_W), lambda bi, d, j: (bi, jnp.maximum(blk(d, j) * hpb - 1, 0), 4)),
                  pl.BlockSpec((1, HALO, LRU_W), lambda bi, d, j: (bi, jnp.minimum((blk(d, j) + 1) * hpb, nh - 1), 4)),
                  _const_spec((8, LRU_W)), _const_spec((1, LRU_W)),
                  pl.BlockSpec((1, LRU_W, 2 * LRU_W), lambda bi, d, j: (d, 0, 0)),
                  pl.BlockSpec((1, 8, 2 * LRU_W), lambda bi, d, j: (d, 0, 0)),
                  pl.BlockSpec((1, 8, LRU_W), lambda bi, d, j: (d, 0, 0))],
        out_specs=pl.BlockSpec((1, 1, TB, LRU_W), lambda bi, d, j: (d, bi, blk(d, j), 0)),
        out_shape=jax.ShapeDtypeStruct((2, b, lt, LRU_W), F32),
        scratch_shapes=[pltpu.VMEM((TB + 2 * HALO, LRU_W), F32), pltpu.VMEM((1, LRU_W), F32)],
        compiler_params=_cparams(("parallel", "arbitrary", "arbitrary")),
        name="rglru",
    )(p, p, p, conv_w, conv_b, wg, bg, lam)


def _gelu_tanh(t):
    return 0.5 * t * (1.0 + jnp.tanh(0.7978845608028654 * (t + 0.044715 * t * t * t)))


def _merge_kernel(*refs, ctx_len, nstream):
    x_refs = refs[:nstream]
    (mctx_ref, mb_ref, hgf_ref, hgb_ref, hgg_ref, hgn_ref, hm_ref, mlac_ref, mlax_ref,
     lf_ref, lb_ref, ly_ref, g1_ref, g2_ref, g3_ref, wbr_ref, wout_ref, o_ref) = refs[nstream:]
    i = pl.program_id(1)
    nctx = ctx_len // TB
    o_mla = jnp.where(i < nctx, mlac_ref[0], mlax_ref[0])
    o = hgf_ref[0, 0] + hgb_ref[0, 0]
    ms = jnp.dot((o * o).astype(BF16), hm_ref[...], preferred_element_type=F32)
    o_hg = o * lax.rsqrt(ms + EPS) * hgn_ref[...] * _sigmoid(hgg_ref[0].astype(F32))
    o_lru = (lf_ref[0, 0] + lb_ref[0, 0]) * _gelu_tanh(ly_ref[0].astype(F32))
    y = (_sigmoid(g1_ref[0].astype(F32)) * jnp.dot(o_hg.astype(BF16), wbr_ref[0], preferred_element_type=F32)
         + _sigmoid(g2_ref[0].astype(F32)) * jnp.dot(o_mla, wbr_ref[1], preferred_element_type=F32)
         + _sigmoid(g3_ref[0].astype(F32)) * jnp.dot(o_lru.astype(BF16), wbr_ref[2], preferred_element_type=F32))
    gate = _row_mods(i, TB, ctx_len, mctx_ref, mb_ref, 2)
    o_ref[0] = _stream_tile(x_refs, i, nctx) + gate * jnp.dot(y.astype(BF16), wout_ref[...], preferred_element_type=F32)


def _merge(xs, mctx, mb, o_hg, p, hg_gain, head_mean, o_mla_c, o_mla_x, h_lru, w_br, w_out, ctx_len):
    b, lt, _ = p.shape
    nctx = ctx_len // TB
    x_specs, x_args = _stream_specs(xs, TB, nctx)

    def pcol(width, c):
        return pl.BlockSpec((1, TB, width), lambda bi, i: (bi, i, c))

    def dirspec(d):
        return pl.BlockSpec((1, 1, TB, 512), lambda bi, i: (d, bi, i, 0))

    return pl.pallas_call(
        functools.partial(_merge_kernel, ctx_len=ctx_len, nstream=len(x_args)),
        grid=(b, lt // TB),
        in_specs=x_specs + [
                  _const_spec((6, D_MODEL)),
                  pl.BlockSpec((1, 6, D_MODEL), lambda bi, i: (bi, 0, 0)),
                  dirspec(0), dirspec(0), pcol(512, 2),
                  _const_spec((1, HG_W)), _const_spec((HG_W, HG_W)),
                  pl.BlockSpec((1, TB, MLA_W), lambda bi, i: (bi, jnp.minimum(i, nctx - 1), 0)),
                  pl.BlockSpec((1, TB, MLA_W), lambda bi, i: (bi, jnp.maximum(i - nctx, 0), 0)),
                  dirspec(0), dirspec(1), pcol(512, 5),
                  pcol(D_MODEL, 3), pcol(D_MODEL, 4), pcol(D_MODEL, 5),
                  _const_spec((3, 512, D_MODEL)), _const_spec((D_MODEL, D_MODEL))],
        out_specs=pl.BlockSpec((1, TB, D_MODEL), lambda bi, i: (bi, i, 0)),
        out_shape=jax.ShapeDtypeStruct((b, lt, D_MODEL), F32),
        compiler_params=_cparams(("parallel", "arbitrary")),
        name="merge",
    )(*x_args, mctx, mb, o_hg[0], o_hg[1], p, hg_gain, head_mean, o_mla_c, o_mla_x, h_lru, h_lru, p, p, p, p, w_br, w_out)


def _first_row_of_max(vals, row, valid):
    masked = jnp.where(valid, vals, -jnp.inf)
    m = jnp.max(masked, axis=0, keepdims=True)
    idx = jnp.min(jnp.where(masked == m, row, ROUTER_PAD), axis=0, keepdims=True)
    return m, idx


def _router(h, wr_hi, wr_lo, br):
    hi = h.astype(BF16)
    lo = (h - hi.astype(F32)).astype(BF16)
    logits = (jnp.dot(hi, wr_hi, preferred_element_type=F32) + jnp.dot(lo, wr_hi, preferred_element_type=F32)
              + jnp.dot(hi, wr_lo, preferred_element_type=F32))
    biased = (logits + br).T
    logits = logits.T
    row = lax.broadcasted_iota(jnp.int32, logits.shape, 0)
    is_group = jnp.logical_and(row >= N_EXPERTS, row < N_EXPERTS + N_GROUPS)
    _, g_row = _first_row_of_max(biased, row, is_group)
    g_max, _ = _first_row_of_max(logits, row, is_group)
    g_exp = jnp.where(is_group, jnp.exp(logits - g_max), 0.0)
    g_sel_logit = jnp.sum(jnp.where(row == g_row, logits, 0.0), axis=0, keepdims=True)
    p_g = jnp.exp(g_sel_logit - g_max) / jnp.sum(g_exp, axis=0, keepdims=True)
    in_group = jnp.right_shift(row, 2) == (g_row - N_EXPERTS)
    _, i1 = _first_row_of_max(biased, row, in_group)
    _, i2 = _first_row_of_max(biased, row, jnp.logical_and(in_group, row != i1))
    l1 = jnp.sum(jnp.where(row == i1, logits, 0.0), axis=0, keepdims=True)
    l2 = jnp.sum(jnp.where(row == i2, logits, 0.0), axis=0, keepdims=True)
    lm = jnp.maximum(l1, l2)
    e1 = jnp.exp(l1 - lm)
    e2 = jnp.exp(l2 - lm)
    inv = p_g / (e1 + e2)
    comb_t = jnp.where(row == i1, e1 * inv, 0.0) + jnp.where(row == i2, e2 * inv, 0.0)
    return comb_t.T


def _moe_kernel(x_ref, mctx_ref, mb_ref, gain_ref, wrh_ref, wrl_ref, br_ref, w1_ref, w3_ref, w2_ref, o_ref, *,
                ctx_len, tm, tile0):
    i = pl.program_id(1) + tile0
    epb = N_EXPERTS // EXPERT_STEPS
    nrows = min(MOE_RB, tm)
    for rb in range(tm // nrows):
        rows = slice(rb * nrows, (rb + 1) * nrows)
        tile = i * (tm // nrows) + rb
        shift = _row_mods(tile, nrows, ctx_len, mctx_ref, mb_ref, 3)
        scale = _row_mods(tile, nrows, ctx_len, mctx_ref, mb_ref, 4)
        gate = _row_mods(tile, nrows, ctx_len, mctx_ref, mb_ref, 5)
        x = x_ref[0, rows, :]
        h = _norm_modulate(x, gain_ref[...], shift, scale)
        comb = _router(h, wrh_ref[...], wrl_ref[...], br_ref[...])
        hb = h.astype(BF16)
        acc = None
        for es in range(EXPERT_STEPS):
            parts = []
            for e in range(es * epb, (es + 1) * epb):
                h1 = jnp.dot(hb, w1_ref[0, e], preferred_element_type=F32)
                h3 = jnp.dot(hb, w3_ref[0, e], preferred_element_type=F32)
                parts.append(h1 * _sigmoid(h1) * h3 * comb[:, e:e + 1])
            y = jnp.dot(jnp.concatenate(parts, axis=-1).astype(BF16), w2_ref[0, es], preferred_element_type=F32)
            acc = y if acc is None else acc + y
        o_ref[0, rows, :] = x + gate * acc


def _moe(xs, mctx, mb, gain, wr_hi, wr_lo, br, w1, w3, w2, layer, ctx_len, latent_only):
    b, lt, _ = xs.shape
    epb = N_EXPERTS // EXPERT_STEPS
    def layer_spec(shape):
        return pl.BlockSpec((1,) + shape, lambda bi, i: (layer,) + (0,) * len(shape), pipeline_mode=pl.Buffered(1))

    if latent_only:
        tm, tile0, rows = TB, ctx_len // TB, lt - ctx_len
    else:
        tm, tile0, rows = (MOE_TM if lt % MOE_TM == 0 else TB), 0, lt
    return pl.pallas_call(
        functools.partial(_moe_kernel, ctx_len=ctx_len, tm=tm, tile0=tile0),
        grid=(b, rows // tm),
        in_specs=[pl.BlockSpec((1, tm, D_MODEL), lambda bi, i: (bi, i + tile0, 0)),
                  _const_spec((6, D_MODEL)),
                  pl.BlockSpec((1, 6, D_MODEL), lambda bi, i: (bi, 0, 0)),
                  _const_spec((1, D_MODEL)),
                  _const_spec((D_MODEL, ROUTER_PAD)), _const_spec((D_MODEL, ROUTER_PAD)), _const_spec((1, ROUTER_PAD)),
                  layer_spec((N_EXPERTS, D_MODEL, D_EXPERT)), layer_spec((N_EXPERTS, D_MODEL, D_EXPERT)),
                  layer_spec((EXPERT_STEPS, epb * D_EXPERT, D_MODEL))],
        out_specs=pl.BlockSpec((1, tm, D_MODEL), lambda bi, i: (bi, i, 0)),
        out_shape=jax.ShapeDtypeStruct((b, rows, D_MODEL), F32),
        compiler_params=_cparams(("parallel", "arbitrary")),
        name="moe",
    )(xs, mctx, mb, gain, wr_hi, wr_lo, br, w1, w3, w2)


def _rope_tables(ctx_len, seq):
    half = MLA_ROPE // 2
    rows = seq // GRID_W
    pos_row = np.repeat(np.arange(rows, dtype=np.float32), GRID_W)
    pos_col = np.tile(np.arange(GRID_W, dtype=np.float32), rows)
    inv = (ROPE_BASE ** (-np.arange(0, half, 2, dtype=np.float32) / half)).astype(np.float32)
    ang = np.concatenate([pos_row[:, None] * inv, pos_col[:, None] * inv], axis=-1)
    cos, sin = np.cos(ang), np.sin(ang)
    cos_t = np.zeros((ctx_len + seq, HEAD_PAD), np.float32)
    sin_t = np.zeros((ctx_len + seq, HEAD_PAD), np.float32)
    cos_t[:, :MLA_NOPE] = 1.0
    cos_t[:ctx_len, MLA_NOPE:MLA_QK] = 1.0
    cos_t[ctx_len:, MLA_NOPE:MLA_NOPE + half] = cos
    cos_t[ctx_len:, MLA_NOPE + half:MLA_QK] = cos
    sin_t[ctx_len:, MLA_NOPE:MLA_NOPE + half] = -sin
    sin_t[ctx_len:, MLA_NOPE + half:MLA_QK] = sin
    return cos_t, sin_t


def _rope_key_selectors():
    half = MLA_ROPE // 2
    kr0 = MLA_Q_RANK + MLA_KV_RANK
    sk1 = np.zeros((512, MLA_HEADS * HEAD_PAD), np.float32)
    sk2 = np.zeros((512, MLA_HEADS * HEAD_PAD), np.float32)
    for h in range(MLA_HEADS):
        for i in range(MLA_ROPE):
            sk1[kr0 + i, h * HEAD_PAD + MLA_NOPE + i] = 1.0
            sk2[kr0 + (i + half) % MLA_ROPE, h * HEAD_PAD + MLA_NOPE + i] = 1.0
    return sk1.astype(BF16), sk2.astype(BF16)


def _np_block_diag(block, n):
    a, bb = block.shape
    out = np.zeros((n * a, n * bb), np.float32)
    for i in range(n):
        out[i * a:(i + 1) * a, i * bb:(i + 1) * bb] = block
    return out


def _swap_rope_halves(t):
    half = MLA_ROPE // 2
    return jnp.concatenate([jnp.zeros_like(t[..., :MLA_NOPE]), t[..., MLA_NOPE + half:], t[..., MLA_NOPE:MLA_NOPE + half]],
                           axis=-1)


def _pad_heads(t):
    pad = [(0, 0)] * (t.ndim - 1) + [(0, HEAD_PAD - t.shape[-1])]
    t = jnp.pad(t, pad)
    return t.reshape(t.shape[:-2] + (t.shape[-2] * HEAD_PAD,))


def _mla_weights(w_uq, w_ukv, gq, gk):
    wq = w_uq.reshape(MLA_Q_RANK, MLA_HEADS, MLA_QK)
    wq1 = _pad_heads(wq).astype(BF16)
    wq2 = _pad_heads(_swap_rope_halves(wq)).astype(BF16)
    wkv = w_ukv.reshape(MLA_KV_RANK, MLA_HEADS, MLA_NOPE + MLA_V)
    wk1 = _pad_heads(wkv[..., :MLA_NOPE]).astype(BF16)
    wv = _pad_heads(wkv[..., MLA_NOPE:]).astype(BF16)

    def gains(g):
        g1 = jnp.pad(g, (0, HEAD_PAD - MLA_QK))
        g2 = jnp.pad(_swap_rope_halves(g), (0, HEAD_PAD - MLA_QK))
        return jnp.stack([g1, g2], axis=0)

    return wq1, wq2, wk1, wv, gains(gq), gains(gk)


def _block_diag(w):
    n, a, bb = w.shape
    eye = jnp.eye(n, dtype=w.dtype)
    return (eye[:, None, :, None] * w[:, :, None, :]).reshape(n * a, n * bb)


def _pad_rows(t, rows=8):
    return jnp.pad(t, ((0, rows - t.shape[0]), (0, 0)))


def kernel(x, c, ctx, c_ctx, w_mod, b_mod, norm_mix, norm_ffn, w_in, hg_lb, hg_norm, mla_q_norm, mla_kv_norm, mla_w_uq, mla_w_ukv, mla_qk_gain_q, mla_qk_gain_k, lru_conv_w, lru_conv_b, lru_wa, lru_ba, lru_wx, lru_bx, lru_lambda, w_br_hg, w_br_mla, w_br_lru, w_out, moe_w_rg, moe_b_rg, moe_w_re, moe_b_re, moe_w1, moe_w3, moe_w2):
    bsz, seq, _ = x.shape
    ctx_len = ctx.shape[1]
    depth = w_in.shape[0]
    assert seq % TB == 0 and ctx_len % TB == 0 and seq % GRID_W == 0 and bsz < 8
    nctx = ctx_len // TB

    xs = (ctx, x)
    cos_t, sin_t = _rope_tables(ctx_len, seq)
    cvec = jnp.zeros((8, D_MODEL), F32).at[:bsz].set(c).at[bsz].set(c_ctx)

    lb_cs = jnp.cumsum(jax.nn.softmax(hg_lb.astype(F32), axis=0), axis=0)
    lb_all = lb_cs - lb_cs[0:1]
    tri_lo = _np_block_diag(np.tril(np.ones((HG_CHUNK, HG_CHUNK), np.float32)), TB // HG_CHUNK)
    tri = np.stack([tri_lo, tri_lo.T], axis=0).astype(BF16)
    head_ones = _np_block_diag(np.ones((HG_DK, HG_DK), np.float32), HG_HEADS)
    head_mean = (head_ones / HG_DK).astype(BF16)
    sk1, sk2 = _rope_key_selectors()

    w_all = _win_layout(w_in)
    moe_w1b, moe_w3b = moe_w1.astype(BF16), moe_w3.astype(BF16)
    moe_w2b = moe_w2.astype(BF16).reshape(depth, EXPERT_STEPS, N_EXPERTS // EXPERT_STEPS * D_EXPERT, D_MODEL)

    mods = _modulation(cvec, w_mod, b_mod).reshape(depth, 8, 6, D_MODEL)

    for l in range(depth):
        mctx, mb = mods[l, bsz], mods[l, :bsz]

        f, p = _inproj(xs, mctx, mb, norm_mix[l][None, :], w_all, l, ctx_len)

        lb = lb_all[l]
        lbc = jnp.stack([jnp.log(lb), jnp.log1p(-lb), 1.0 - lb] + [jnp.zeros_like(lb)] * 5, axis=1)
        o_hg = _hgrn(p, f, tri, lbc, head_ones, nctx)

        wq1, wq2, wk1, wv, gq, gk = _mla_weights(mla_w_uq[l], mla_w_ukv[l], mla_qk_gain_q[l], mla_qk_gain_k[l])
        qt, kh, vt, qsq, ksq = _mla_prep(p, cos_t, sin_t, mla_q_norm[l][None, :], mla_kv_norm[l][None, :], gq, gk,
                                    wq1, wq2, wk1, sk1, sk2, wv, nctx)
        bound = jnp.sqrt(jnp.max(qsq, axis=(1, 3)) * jnp.max(ksq, axis=(1, 3))).reshape(bsz, MLA_HEADS // 2, 2).max(axis=-1)
        tq = next(t for t in (512, 256) if seq % t == 0)
        o_mla_x = _flash(bound, qt, kh, vt, seq, ctx_len + seq, tq, 0, "mla_attention")
        if l < depth - 1:
            o_mla_c = _flash(bound, qt, kh, vt, ctx_len, ctx_len, TB, seq, "mla_attention_ctx")
        else:
            o_mla_c = o_mla_x[:, :ctx_len]

        wg = jnp.stack([jnp.concatenate([_block_diag(lru_wa[l, d]), _block_diag(lru_wx[l, d])], axis=-1)
                        for d in range(2)], axis=0).astype(BF16)
        bg = jnp.concatenate([lru_ba[l], lru_bx[l]], axis=-1)[:, None, :] * jnp.ones((1, 8, 1), F32)
        lam = lru_lambda[l][:, None, :] * jnp.ones((1, 8, 1), F32)
        h_lru = _lru(p, _pad_rows(lru_conv_w[l]), lru_conv_b[l][None, :], wg, bg, lam, nctx)

        w_br = jnp.stack([w_br_hg[l], w_br_mla[l], w_br_lru[l]], axis=0).astype(BF16)
        hg_gain = jnp.tile(hg_norm[l], HG_HEADS)[None, :]
        xs = _merge(xs, mctx, mb, o_hg, p, hg_gain, head_mean, o_mla_c, o_mla_x, h_lru, w_br, w_out[l].astype(BF16), ctx_len)

        wr = jnp.pad(jnp.concatenate([moe_w_re[l], moe_w_rg[l]], axis=-1), ((0, 0), (0, ROUTER_PAD - N_EXPERTS - N_GROUPS)))
        br = jnp.pad(jnp.concatenate([moe_b_re[l], moe_b_rg[l]]), (0, ROUTER_PAD - N_EXPERTS - N_GROUPS))[None, :]
        wr_hi = wr.astype(BF16)
        wr_lo = (wr - wr_hi.astype(F32)).astype(BF16)
        xs = _moe(xs, mctx, mb, norm_ffn[l][None, :], wr_hi, wr_lo, br, moe_w1b, moe_w3b, moe_w2b, l,
                  ctx_len, latent_only=(l == depth - 1))

    return xs
```

```python
import functools

import numpy as np
import jax
import jax.numpy as jnp
from jax import lax
from jax.experimental import pallas as pl
from jax.experimental.pallas import tpu as pltpu

F32 = jnp.float32
BF16 = jnp.bfloat16
HIGHEST = lax.Precision.HIGHEST

D_MODEL = 1024
GRID_W = 64
EPS = 1e-6

HG_HEADS = 8
HG_DK = 64
HG_W = 512
HG_CHUNK = 64
HG_MID = HG_CHUNK // 2
HG_MAX_LOG_RANGE = 80.0

MLA_HEADS = 8
MLA_Q_RANK = 256
MLA_KV_RANK = 128
MLA_NOPE = 64
MLA_ROPE = 32
MLA_V = 64
MLA_QK = MLA_NOPE + MLA_ROPE
MLA_W = MLA_HEADS * MLA_V
HEAD_PAD = 128
V_ROWS = 80
SCORE_SCALE = MLA_QK ** -0.5 * 1.4426950408889634
MAX_UNSHIFTED_SCORE = 57.0
ROPE_BASE = 10000.0

LRU_W = 512
LRU_BLOCKS = 8
LRU_BD = LRU_W // LRU_BLOCKS
CONV_W = 4
LRU_C = 8.0

N_GROUPS = 4
EXP_PER_GROUP = 4
N_EXPERTS = N_GROUPS * EXP_PER_GROUP
D_EXPERT = 256
EXPERT_STEPS = 4
ROUTER_PAD = 128
MOE_TM = 768
MOE_RB = 256
INPROJ_TM = 768

TB = 256
HALO = 16
SCAN_GROUP = 8
P_WIDTH = 6144
F_WIDTH = 1024
VMEM_LIMIT = 56 * 1024 * 1024

NT_DIMS = (((1,), (1,)), ((), ()))


def _cparams(sem):
    return pltpu.CompilerParams(dimension_semantics=sem, vmem_limit_bytes=VMEM_LIMIT)


def _const_spec(shape):
    nd = len(shape)
    return pl.BlockSpec(shape, lambda *_: (0,) * nd, pipeline_mode=pl.Buffered(1))


def _sigmoid(t):
    return 0.5 * jnp.tanh(0.5 * t) + 0.5


def _scan_block(j, d, nctx, nblk):
    fwd = j
    bwd = jnp.where(j < nctx, nctx - 1 - j, nblk - 1 - (j - nctx))
    return jnp.where(d == 0, fwd, bwd)


def _mod_kernel(c_ref, w_ref, b_ref, o_ref):
    c = c_ref[...]
    s = c * _sigmoid(c)
    o_ref[0] = jnp.dot(s, w_ref[0], precision=HIGHEST, preferred_element_type=F32) + b_ref[0]


def _modulation(cvec, w_mod, b_mod):
    depth, _, n = w_mod.shape
    tn = 1024
    return pl.pallas_call(
        _mod_kernel,
        grid=(depth, n // tn),
        in_specs=[pl.BlockSpec((8, D_MODEL), lambda l, j: (0, 0)),
                  pl.BlockSpec((1, D_MODEL, tn), lambda l, j: (l, 0, j)),
                  pl.BlockSpec((1, 1, tn), lambda l, j: (l, 0, j))],
        out_specs=pl.BlockSpec((1, 8, tn), lambda l, j: (l, 0, j)),
        out_shape=jax.ShapeDtypeStruct((depth, 8, n), F32),
        compiler_params=_cparams(("arbitrary", "arbitrary")),
        name="modulation",
    )(cvec, w_mod, b_mod.reshape(depth, 1, n))


def _row_mods(i, tm, ctx_len, mctx_ref, mb_ref, k):
    row = i * tm + lax.broadcasted_iota(jnp.int32, (tm, 1), 0)
    return jnp.where(row < ctx_len, mctx_ref[k:k + 1, :], mb_ref[0, k:k + 1, :])


def _norm_modulate(x, gain, shift, scale):
    ms = jnp.mean(x * x, axis=-1, keepdims=True)
    xn = x * lax.rsqrt(ms + EPS) * gain
    return xn * (1.0 + scale) + shift


def _stream_specs(xs, tm, nctx):
    if isinstance(xs, tuple):
        assert tm == TB
        return ([pl.BlockSpec((1, tm, D_MODEL), lambda bi, i: (bi, jnp.minimum(i, nctx - 1), 0)),
                 pl.BlockSpec((1, tm, D_MODEL), lambda bi, i: (bi, jnp.maximum(i - nctx, 0), 0))], list(xs))
    return [pl.BlockSpec((1, tm, D_MODEL), lambda bi, i: (bi, i, 0))], [xs]


def _stream_tile(x_refs, i, nctx):
    if len(x_refs) == 2:
        return jnp.where(i < nctx, x_refs[0][0], x_refs[1][0])
    return x_refs[0][0]


def _win_layout_kernel(w_ref, o_ref):
    def put(dst, src, width):
        o_ref[0, :, dst:dst + width] = w_ref[0, :, src:src + width].astype(BF16)

    mla_w = MLA_Q_RANK + MLA_KV_RANK + MLA_ROPE
    mla0 = 5 * HG_W
    put(0, HG_W, 2 * HG_W)
    put(F_WIDTH, 0, HG_W)
    put(F_WIDTH + HG_W, 3 * HG_W, 2 * HG_W)
    put(F_WIDTH + 3 * HG_W, mla0, mla_w)
    o_ref[0, :, F_WIDTH + 3 * HG_W + mla_w:F_WIDTH + 4 * HG_W] = jnp.zeros((o_ref.shape[1], HG_W - mla_w), BF16)
    rest = w_ref.shape[2] - mla0 - mla_w
    put(F_WIDTH + 4 * HG_W, mla0 + mla_w, rest)


def _win_layout(w_in):
    depth, d, n = w_in.shape
    tr = 256
    return pl.pallas_call(
        _win_layout_kernel,
        grid=(depth, d // tr),
        in_specs=[pl.BlockSpec((1, tr, n), lambda l, i: (l, i, 0))],
        out_specs=pl.BlockSpec((1, tr, F_WIDTH + P_WIDTH), lambda l, i: (l, i, 0)),
        out_shape=jax.ShapeDtypeStruct((depth, d, F_WIDTH + P_WIDTH), BF16),
        compiler_params=_cparams(("arbitrary", "arbitrary")),
        name="w_in_layout",
    )(w_in)


def _inproj_kernel(*refs, ctx_len, tm, nstream):
    x_refs = refs[:nstream]
    mctx_ref, mb_ref, gain_ref, w_ref, f_ref, p_ref = refs[nstream:]
    w_ref = w_ref.at[0]
    i = pl.program_id(1)
    shift = _row_mods(i, tm, ctx_len, mctx_ref, mb_ref, 0)
    scale = _row_mods(i, tm, ctx_len, mctx_ref, mb_ref, 1)
    h = _norm_modulate(_stream_tile(x_refs, i, ctx_len // TB), gain_ref[...], shift, scale).astype(BF16)
    f_ref[0] = jnp.dot(h, w_ref[:, 0:F_WIDTH], preferred_element_type=F32)
    cw = 512
    for j in range(P_WIDTH // cw):
        lo = F_WIDTH + j * cw
        p_ref[0, :, j * cw:(j + 1) * cw] = jnp.dot(
            h, w_ref[:, lo:lo + cw], preferred_element_type=F32).astype(BF16)


def _inproj(xs, mctx, mb, gain, w_all, layer, ctx_len):
    split = isinstance(xs, tuple)
    b = (xs[0] if split else xs).shape[0]
    lt = sum(t.shape[1] for t in xs) if split else xs.shape[1]
    tm = INPROJ_TM if lt % INPROJ_TM == 0 and not split else TB
    x_specs, x_args = _stream_specs(xs, tm, ctx_len // TB)
    return pl.pallas_call(
        functools.partial(_inproj_kernel, ctx_len=ctx_len, tm=tm, nstream=len(x_args)),
        grid=(b, lt // tm),
        in_specs=x_specs + [
                  _const_spec((6, D_MODEL)),
                  pl.BlockSpec((1, 6, D_MODEL), lambda bi, i: (bi, 0, 0)),
                  _const_spec((1, D_MODEL)),
                  pl.BlockSpec((1, D_MODEL, F_WIDTH + P_WIDTH), lambda bi, i: (layer, 0, 0), pipeline_mode=pl.Buffered(1))],
        out_specs=[pl.BlockSpec((1, tm, F_WIDTH), lambda bi, i: (bi, i, 0)),
                   pl.BlockSpec((1, tm, P_WIDTH), lambda bi, i: (bi, i, 0))],
        out_shape=[jax.ShapeDtypeStruct((b, lt, F_WIDTH), F32),
                   jax.ShapeDtypeStruct((b, lt, P_WIDTH), BF16)],
        compiler_params=_cparams(("parallel", "arbitrary")),
        name="inproj",
    )(*x_args, mctx, mb, gain, w_all)


def _hgrn_direction(q_ref, v_ref, f_ref, tri_ref, lbc_ref, bd_ref, o_ref, st_ref, c_scr, k_scr, v_scr, reverse):
    d = 1 if reverse else 0
    nch = TB // HG_CHUNK
    npair = HG_W // 128
    first, last = (HG_CHUNK - 1, 0) if reverse else (0, HG_CHUNK - 1)

    log_lb = lbc_ref[d, 0:1, :]
    log_1mlb = lbc_ref[d, 1:2, :]
    one_mlb = lbc_ref[d, 2:3, :]
    z = f_ref[0]
    q = q_ref[0].astype(F32)
    v = v_ref[0].astype(F32)
    e = jnp.exp(-jnp.abs(z))
    log_sig = jnp.minimum(z, 0.0) - jnp.log(1.0 + e)
    t = log_1mlb + log_sig
    logf = jnp.maximum(log_lb, t) + jnp.log(1.0 + jnp.exp(-jnp.abs(log_lb - t)))
    k = one_mlb * (jnp.where(z >= 0, e, 1.0) / (1.0 + e))
    hi = logf.astype(BF16)
    lo = (logf - hi.astype(F32)).astype(BF16)
    tri = tri_ref[d]
    c = jnp.dot(tri, hi, preferred_element_type=F32) + jnp.dot(tri, lo, preferred_element_type=F32)

    span = jnp.zeros((1, HG_W), F32)
    for ci in range(nch):
        r0 = ci * HG_CHUNK
        c_mid = c[r0 + HG_MID:r0 + HG_MID + 1, :]
        span = jnp.maximum(span, jnp.maximum(c[r0 + first:r0 + first + 1, :] - c_mid, c_mid - c[r0 + last:r0 + last + 1, :]))
    qmax = jnp.max(jnp.abs(q), axis=0, keepdims=True)
    span = jnp.max(span + jnp.log(jnp.maximum(qmax, 1.0)))

    lane = lax.broadcasted_iota(jnp.int32, (HG_CHUNK, HG_W), 1)
    low_head = (lane & (HG_DK * 2 - 1)) < HG_DK
    ri = lax.broadcasted_iota(jnp.int32, (HG_CHUNK, 128), 0)
    si = lax.broadcasted_iota(jnp.int32, (HG_CHUNK, 128), 1) & (HG_DK - 1)
    visited = (si >= ri) if reverse else (si <= ri)
    r2 = lax.broadcasted_iota(jnp.int32, (128, 128), 0)
    l2 = lax.broadcasted_iota(jnp.int32, (128, 128), 1)
    same_head = (r2 < HG_DK) == (l2 < HG_DK)
    rowi = lax.broadcasted_iota(jnp.int32, (HG_CHUNK, 1), 0)

    def split_heads(t, transpose=False):
        top = jnp.where(low_head, t, 0.0)
        bot = jnp.where(low_head, 0.0, t)
        blocks = [jnp.concatenate([top[:, 128 * g:128 * (g + 1)], bot[:, 128 * g:128 * (g + 1)]], axis=0) for g in range(npair)]
        return [(blk.T if transpose else blk).astype(BF16) for blk in blocks]

    def exact_intra(qq, kk, vv, cc):
        c_scr[...] = cc
        k_scr[...] = kk
        v_scr[...] = vv
        bd = bd_ref[...]

        def key_row(s, acc):
            c_s = c_scr[pl.ds(s, 1), :]
            w = qq * jnp.exp(jnp.minimum(cc - c_s, 0.0)) * k_scr[pl.ds(s, 1), :]
            ws = jnp.dot(w, bd, precision=HIGHEST, preferred_element_type=F32)
            ok = (rowi <= s) if reverse else (rowi >= s)
            return acc + jnp.where(ok, ws, 0.0) * v_scr[pl.ds(s, 1), :]

        return lax.fori_loop(0, HG_CHUNK, key_row, jnp.zeros((HG_CHUNK, HG_W), F32))

    def chunk(ci, exact):
        r0 = ci * HG_CHUNK
        rows = slice(r0, r0 + HG_CHUNK)
        cc, qq, kk, vv = c[rows], q[rows], k[rows], v[rows]
        c_mid = cc[HG_MID:HG_MID + 1, :]
        c_end = cc[last:last + 1, :]
        qs = (qq * jnp.exp(cc)).astype(BF16)
        ks = (kk * jnp.exp(c_end - cc)).astype(BF16)
        dec = jnp.exp(c_end)
        if exact:
            intra = exact_intra(qq, kk, vv, cc)
        else:
            qd = (qq * jnp.exp(cc - c_mid)).astype(BF16)
            kd_blocks = split_heads(kk * jnp.exp(c_mid - cc), transpose=True)
            v_blocks = split_heads(vv)
        pieces = []
        for g in range(npair):
            ls = slice(128 * g, 128 * (g + 1))
            st = st_ref[d, g]
            o_g = jnp.dot(qs[:, ls], st.T.astype(BF16), preferred_element_type=F32)
            if not exact:
                a = jnp.dot(qd[:, ls], kd_blocks[g], preferred_element_type=F32)
                a = jnp.where(visited, a, 0.0).astype(BF16)
                o_g = o_g + jnp.dot(a, v_blocks[g], preferred_element_type=F32)
            vt = vv[:, ls].T.astype(BF16)
            kv = jnp.dot(vt, ks[:, ls], preferred_element_type=F32)
            st_ref[d, g] = st * dec[:, ls] + jnp.where(same_head, kv, 0.0)
            pieces.append(o_g)
        o = jnp.concatenate(pieces, axis=-1)
        o_ref[0, 0, rows, :] = o + intra if exact else o

    order = list(range(nch - 1, -1, -1) if reverse else range(nch))
    return span, lambda n, exact: chunk(order[n], exact)


def _hgrn_kernel(qf_ref, vf_ref, ff_ref, qb_ref, vb_ref, fb_ref, tri_ref, lbc_ref, bd_ref, of_ref, ob_ref,
                 st_ref, c_scr, k_scr, v_scr):
    @pl.when(pl.program_id(1) == 0)
    def _():
        st_ref[...] = jnp.zeros_like(st_ref)

    span_f, chunk_f = _hgrn_direction(qf_ref, vf_ref, ff_ref, tri_ref, lbc_ref, bd_ref, of_ref, st_ref,
                                      c_scr, k_scr, v_scr, False)
    span_b, chunk_b = _hgrn_direction(qb_ref, vb_ref, fb_ref, tri_ref, lbc_ref, bd_ref, ob_ref, st_ref,
                                      c_scr, k_scr, v_scr, True)
    span = jnp.maximum(span_f, span_b)

    def run(exact):
        for n in range(TB // HG_CHUNK):
            chunk_f(n, exact)
            chunk_b(n, exact)

    @pl.when(span <= HG_MAX_LOG_RANGE)
    def _matmul_form():
        run(False)

    @pl.when(span > HG_MAX_LOG_RANGE)
    def _exact_form():
        run(True)


def _hgrn(p, f, tri, lbc, bd, nctx):
    b, lt, _ = p.shape
    nblk = lt // TB

    def spec(d, col):
        return pl.BlockSpec((1, TB, HG_W), lambda bi, j: (bi, _scan_block(j, d, nctx, nblk), col))

    def out_spec(d):
        return pl.BlockSpec((1, 1, TB, HG_W), lambda bi, j: (0, bi, _scan_block(j, d, nctx, nblk), 0))

    out = jax.ShapeDtypeStruct((1, b, lt, HG_W), F32)
    return pl.pallas_call(
        _hgrn_kernel,
        grid=(b, nblk),
        in_specs=[spec(0, 0), spec(0, 1), spec(0, 0),
                  spec(1, 0), spec(1, 1), spec(1, 1),
                  _const_spec((2, TB, TB)), _const_spec((2, 8, HG_W)), _const_spec((HG_W, HG_W))],
        out_specs=[out_spec(0), out_spec(1)],
        out_shape=[out, out],
        scratch_shapes=[pltpu.VMEM((2, HG_W // 128, 128, 128), F32),
                        pltpu.VMEM((HG_CHUNK, HG_W), F32),
                        pltpu.VMEM((HG_CHUNK, HG_W), F32),
                        pltpu.VMEM((HG_CHUNK, HG_W), F32)],
        compiler_params=_cparams(("parallel", "arbitrary")),
        name="hgrn2",
    )(p, p, f, p, p, f, tri, lbc, bd)


def _rms_rows(t, width):
    return lax.rsqrt(jnp.sum(t * t, axis=-1, keepdims=True) * (1.0 / width) + EPS)


def _mla_prep_kernel(pm_ref, cos_ref, sin_ref, qn_ref, kvn_ref, gq_ref, gk_ref,
                     wq1_ref, wq2_ref, wk1_ref, sk1_ref, sk2_ref, wv_ref,
                     qt_ref, k_ref, vt_ref, qsq_ref, ksq_ref):
    pm = pm_ref[0]
    dq = pm[:, 0:MLA_Q_RANK].astype(F32)
    dkv = pm[:, MLA_Q_RANK:MLA_Q_RANK + MLA_KV_RANK].astype(F32)
    dqn = (dq * _rms_rows(dq, MLA_Q_RANK) * qn_ref[...]).astype(BF16)
    dkvn = (dkv * _rms_rows(dkv, MLA_KV_RANK) * kvn_ref[...]).astype(BF16)
    q1 = jnp.dot(dqn, wq1_ref[...], preferred_element_type=F32)
    q2 = jnp.dot(dqn, wq2_ref[...], preferred_element_type=F32)
    k1 = (jnp.dot(dkvn, wk1_ref[...], preferred_element_type=F32)
          + jnp.dot(pm, sk1_ref[...], preferred_element_type=F32))
    k2 = jnp.dot(pm, sk2_ref[...], preferred_element_type=F32)
    vv = jnp.dot(dkvn, wv_ref[...], preferred_element_type=F32)
    cos = cos_ref[...]
    sin = sin_ref[...]
    gq_c = gq_ref[0:1, :] * cos
    gq_s = gq_ref[1:2, :] * sin
    gk_c = gk_ref[0:1, :] * cos
    gk_s = gk_ref[1:2, :] * sin
    vrow = lax.broadcasted_iota(jnp.int32, (V_ROWS, TB), 0)
    qn, kn = [], []

    def max_sq_norm(t):
        return jnp.broadcast_to(jnp.max(jnp.sum(t * t, axis=-1, keepdims=True), axis=0, keepdims=True), (1, HEAD_PAD))

    for h in range(MLA_HEADS):
        sl = slice(HEAD_PAD * h, HEAD_PAD * (h + 1))
        qh = q1[:, sl]
        qo = (qh * gq_c + q2[:, sl] * gq_s) * (_rms_rows(qh, MLA_QK) * SCORE_SCALE)
        qt_ref[0, h] = qo.T.astype(BF16)
        qn.append(max_sq_norm(qo))
        kh = k1[:, sl]
        ko = (kh * gk_c + k2[:, sl] * gk_s) * _rms_rows(kh, MLA_QK)
        k_ref[0, h] = ko.astype(BF16)
        kn.append(max_sq_norm(ko))
        vt = vv[:, sl].T[0:V_ROWS, :]
        vt_ref[0, h] = jnp.where(vrow == MLA_V, 1.0, vt).astype(BF16)
    qsq_ref[0, 0] = jnp.concatenate(qn, axis=0)
    ksq_ref[0, 0] = jnp.concatenate(kn, axis=0)


def _mla_prep(p, cos_t, sin_t, qn, kvn, gq, gk, wq1, wq2, wk1, sk1, sk2, wv, nctx):
    b, lt, _ = p.shape
    hw = MLA_HEADS * HEAD_PAD
    nblk = lt // TB
    return pl.pallas_call(
        _mla_prep_kernel,
        grid=(b, lt // TB),
        in_specs=[pl.BlockSpec((1, TB, 512), lambda bi, i: (bi, i, 3)),
                  pl.BlockSpec((TB, HEAD_PAD), lambda bi, i: (i, 0)),
                  pl.BlockSpec((TB, HEAD_PAD), lambda bi, i: (i, 0)),
                  _const_spec((1, MLA_Q_RANK)), _const_spec((1, MLA_KV_RANK)),
                  _const_spec((2, HEAD_PAD)), _const_spec((2, HEAD_PAD)),
                  _const_spec((MLA_Q_RANK, hw)), _const_spec((MLA_Q_RANK, hw)),
                  _const_spec((MLA_KV_RANK, hw)), _const_spec((512, hw)), _const_spec((512, hw)),
                  _const_spec((MLA_KV_RANK, hw))],
        out_specs=[pl.BlockSpec((1, MLA_HEADS, HEAD_PAD, TB), lambda bi, i: (bi, 0, 0, (i + nblk - nctx) % nblk)),
                   pl.BlockSpec((1, MLA_HEADS, TB, HEAD_PAD), lambda bi, i: (bi, 0, i, 0)),
                   pl.BlockSpec((1, MLA_HEADS, V_ROWS, TB), lambda bi, i: (bi, 0, 0, i)),
                   pl.BlockSpec((1, 1, MLA_HEADS, HEAD_PAD), lambda bi, i: (bi, i, 0, 0)),
                   pl.BlockSpec((1, 1, MLA_HEADS, HEAD_PAD), lambda bi, i: (bi, i, 0, 0))],
        out_shape=[jax.ShapeDtypeStruct((b, MLA_HEADS, HEAD_PAD, lt), BF16),
                   jax.ShapeDtypeStruct((b, MLA_HEADS, lt, HEAD_PAD), BF16),
                   jax.ShapeDtypeStruct((b, MLA_HEADS, V_ROWS, lt), BF16),
                   jax.ShapeDtypeStruct((b, lt // TB, MLA_HEADS, HEAD_PAD), F32),
                   jax.ShapeDtypeStruct((b, lt // TB, MLA_HEADS, HEAD_PAD), F32)],
        compiler_params=_cparams(("parallel", "arbitrary")),
        name="mla_prep",
    )(p, cos_t, sin_t, qn, kvn, gq, gk, wq1, wq2, wk1, sk1, sk2, wv)


def _attend_bounded(qts, k_ref, vt_ref, s_scr, nkeys, tk):
    n = nkeys // tk

    def scores(kb, slot):
        for hh in range(2):
            s_scr[slot, hh, 0:tk, :] = jnp.dot(k_ref[0, hh, kb * tk:(kb + 1) * tk, :], qts[hh], preferred_element_type=F32)

    def accumulate(kb, slot, accs):
        out = []
        for hh in range(2):
            p = jnp.exp2(s_scr[slot, hh, 0:tk, :]).astype(BF16)
            pv = jnp.dot(vt_ref[0, hh, :, kb * tk:(kb + 1) * tk], p, preferred_element_type=F32)
            out.append(pv if accs is None else accs[hh] + pv)
        return out

    scores(0, 0)
    accs = None
    for kb in range(n):
        if kb + 1 < n:
            scores(kb + 1, (kb + 1) % 2)
        accs = accumulate(kb, kb % 2, accs)
    return accs


def _attend_online(qts, k_ref, vt_ref, nkeys, tk):
    def kv_step(kb, carry):
        r0 = pl.multiple_of(kb * tk, tk)
        out = []
        for hh in range(2):
            m, acc = carry[hh]
            s = jnp.dot(k_ref[0, hh, pl.ds(r0, tk), :], qts[hh], preferred_element_type=F32)
            m_new = jnp.maximum(m, jnp.max(s, axis=0, keepdims=True))
            p = jnp.exp2(s - m_new).astype(BF16)
            acc = jnp.exp2(m - m_new) * acc + jnp.dot(vt_ref[0, hh, :, pl.ds(r0, tk)], p, preferred_element_type=F32)
            out.append((m_new, acc))
        return tuple(out)

    tq = qts[0].shape[1]
    init = (jnp.full((1, tq), -jnp.inf, F32), jnp.zeros((V_ROWS, tq), F32))
    res = lax.fori_loop(0, nkeys // tk, kv_step, (init, init))
    return res[0][1], res[1][1]


def _flash_kernel(bound_ref, qt_ref, k_ref, vt_ref, o_ref, s_scr, *, nkeys, tk):
    qts = [qt_ref[0, hh] for hh in range(2)]
    worst = bound_ref[pl.program_id(0), pl.program_id(1)]

    def finish(accs):
        outs = [(acc[0:MLA_V, :] / acc[MLA_V:MLA_V + 1, :]).T for acc in accs]
        o_ref[0] = jnp.concatenate(outs, axis=-1).astype(BF16)

    @pl.when(worst <= MAX_UNSHIFTED_SCORE)
    def _():
        finish(_attend_bounded(qts, k_ref, vt_ref, s_scr, nkeys, tk))

    @pl.when(worst > MAX_UNSHIFTED_SCORE)
    def _():
        finish(_attend_online(qts, k_ref, vt_ref, nkeys, tk))


def _flash(bound, qt, k, vt, nq, nkeys, tq, q_col0, name):
    b = k.shape[0]
    tk = next(t for t in (768, 512, 256) if nkeys % t == 0)
    cb0 = q_col0 // tq
    return pl.pallas_call(
        functools.partial(_flash_kernel, nkeys=nkeys, tk=tk),
        grid=(b, MLA_HEADS // 2, nq // tq),
        in_specs=[pl.BlockSpec(memory_space=pltpu.SMEM),
                  pl.BlockSpec((1, 2, HEAD_PAD, tq), lambda bi, g, i: (bi, g, 0, cb0 + i)),
                  pl.BlockSpec((1, 2, nkeys, HEAD_PAD), lambda bi, g, i: (bi, g, 0, 0)),
                  pl.BlockSpec((1, 2, V_ROWS, nkeys), lambda bi, g, i: (bi, g, 0, 0))],
        out_specs=pl.BlockSpec((1, tq, HEAD_PAD), lambda bi, g, i: (bi, i, g)),
        out_shape=jax.ShapeDtypeStruct((b, nq, MLA_W), BF16),
        scratch_shapes=[pltpu.VMEM((2, 2, tk, tq), F32)],
        compiler_params=_cparams(("parallel", "parallel", "arbitrary")),
        name=name,
    )(bound, qt, k, vt)


def _lru_kernel(x_ref, xp_ref, xn_ref, cw_ref, cb_ref, wg_ref, bg_ref, lam_ref, o_ref, xe_scr, h_scr, *, nctx, nblk):
    d = pl.program_id(1)
    j = pl.program_id(2)
    blk = _scan_block(j, d, nctx, nblk)

    @pl.when(j == 0)
    def _():
        h_scr[...] = jnp.zeros_like(h_scr)

    keep_prev = jnp.logical_and(blk != 0, blk != nctx)
    keep_next = jnp.logical_and(blk != nctx - 1, blk != nblk - 1)
    xe_scr[0:HALO, :] = jnp.where(keep_prev, xp_ref[0].astype(F32), 0.0)
    xe_scr[HALO:HALO + TB, :] = x_ref[0].astype(F32)
    xe_scr[HALO + TB:2 * HALO + TB, :] = jnp.where(keep_next, xn_ref[0].astype(F32), 0.0)
    left = CONV_W // 2
    u = jnp.broadcast_to(cb_ref[...], (TB, LRU_W))
    for tap in range(CONV_W):
        u = u + xe_scr[pl.ds(HALO - left + tap, TB), :] * cw_ref[tap:tap + 1, :]

    g = jnp.dot(u.astype(BF16), wg_ref[0], preferred_element_type=F32) + bg_ref[0, 0:1, :]
    r = _sigmoid(g[:, 0:LRU_W])
    ig = _sigmoid(g[:, LRU_W:2 * LRU_W])
    neg_lam = -lam_ref[0, 0:1, :]
    softplus = jnp.maximum(neg_lam, 0.0) + jnp.log1p(jnp.exp(-jnp.abs(neg_lam)))
    a = jnp.exp(-LRU_C * r * softplus)
    bb = jnp.sqrt(1.0 - a * a) * (ig * u)
    def scan(forward):
        ngroups = TB // SCAN_GROUP
        aa = a.reshape(ngroups, SCAN_GROUP, LRU_W)
        hh = bb.reshape(ngroups, SCAN_GROUP, LRU_W)
        rg = lax.broadcasted_iota(jnp.int32, (1, SCAN_GROUP, 1), 1)
        k = 1
        while k < SCAN_GROUP:
            shift = k if forward else SCAN_GROUP - k
            valid = (rg >= k) if forward else (rg < SCAN_GROUP - k)
            a_sh = jnp.where(valid, pltpu.roll(aa, shift, 1), 1.0)
            h_sh = jnp.where(valid, pltpu.roll(hh, shift, 1), 0.0)
            hh = aa * h_sh + hh
            aa = aa * a_sh
            k *= 2
        carry = h_scr[...]
        last = SCAN_GROUP - 1 if forward else 0
        for g in (range(ngroups) if forward else range(ngroups - 1, -1, -1)):
            hg = hh[g] + aa[g] * carry
            o_ref[0, 0, g * SCAN_GROUP:(g + 1) * SCAN_GROUP, :] = hg
            carry = hg[last:last + 1, :]
        h_scr[...] = carry

    @pl.when(d == 0)
    def _():
        scan(True)

    @pl.when(d == 1)
    def _():
        scan(False)


def _lru(p, conv_w, conv_b, wg, bg, lam, nctx):
    b, lt, _ = p.shape
    nblk = lt // TB
    hpb = TB // HALO
    nh = lt // HALO

    def blk(d, j):
        return _scan_block(j, d, nctx, nblk)

    return pl.pallas_call(
        functools.partial(_lru_kernel, nctx=nctx, nblk=nblk),
        grid=(b, 2, nblk),
        in_specs=[pl.BlockSpec((1, TB, LRU_W), lambda bi, d, j: (bi, blk(d, j), 4)),
                  pl.BlockSpec((1, HALO, LRU_W), lambda bi, d, j: (bi, jnp.maximum(blk(d, j) * hpb - 1, 0), 4)),
                  pl.BlockSpec((1, HALO, LRU_W), lambda bi, d, j: (bi, jnp.minimum((blk(d, j) + 1) * hpb, nh - 1), 4)),
                  _const_spec((8, LRU_W)), _const_spec((1, LRU_W)),
                  pl.BlockSpec((1, LRU_W, 2 * LRU_W), lambda bi, d, j: (d, 0, 0)),
                  pl.BlockSpec((1, 8, 2 * LRU_W), lambda bi, d, j: (d, 0, 0)),
                  pl.BlockSpec((1, 8, LRU_W), lambda bi, d, j: (d, 0, 0))],
        out_specs=pl.BlockSpec((1, 1, TB, LRU_W), lambda bi, d, j: (d, bi, blk(d, j), 0)),
        out_shape=jax.ShapeDtypeStruct((2, b, lt, LRU_W), F32),
        scratch_shapes=[pltpu.VMEM((TB + 2 * HALO, LRU_W), F32), pltpu.VMEM((1, LRU_W), F32)],
        compiler_params=_cparams(("parallel", "arbitrary", "arbitrary")),
        name="rglru",
    )(p, p, p, conv_w, conv_b, wg, bg, lam)


def _gelu_tanh(t):
    return 0.5 * t * (1.0 + jnp.tanh(0.7978845608028654 * (t + 0.044715 * t * t * t)))


def _merge_kernel(*refs, ctx_len, nstream):
    x_refs = refs[:nstream]
    (mctx_ref, mb_ref, hgf_ref, hgb_ref, hgg_ref, hgn_ref, hm_ref, mlac_ref, mlax_ref,
     lf_ref, lb_ref, ly_ref, g1_ref, g2_ref, g3_ref, wbr_ref, wout_ref, o_ref) = refs[nstream:]
    i = pl.program_id(1)
    nctx = ctx_len // TB
    o_mla = jnp.where(i < nctx, mlac_ref[0], mlax_ref[0])
    o = hgf_ref[0, 0] + hgb_ref[0, 0]
    ms = jnp.dot((o * o).astype(BF16), hm_ref[...], preferred_element_type=F32)
    o_hg = o * lax.rsqrt(ms + EPS) * hgn_ref[...] * _sigmoid(hgg_ref[0].astype(F32))
    o_lru = (lf_ref[0, 0] + lb_ref[0, 0]) * _gelu_tanh(ly_ref[0].astype(F32))
    y = (_sigmoid(g1_ref[0].astype(F32)) * jnp.dot(o_hg.astype(BF16), wbr_ref[0], preferred_element_type=F32)
         + _sigmoid(g2_ref[0].astype(F32)) * jnp.dot(o_mla, wbr_ref[1], preferred_element_type=F32)
         + _sigmoid(g3_ref[0].astype(F32)) * jnp.dot(o_lru.astype(BF16), wbr_ref[2], preferred_element_type=F32))
    gate = _row_mods(i, TB, ctx_len, mctx_ref, mb_ref, 2)
    o_ref[0] = _stream_tile(x_refs, i, nctx) + gate * jnp.dot(y.astype(BF16), wout_ref[...], preferred_element_type=F32)


def _merge(xs, mctx, mb, o_hg, p, hg_gain, head_mean, o_mla_c, o_mla_x, h_lru, w_br, w_out, ctx_len):
    b, lt, _ = p.shape
    nctx = ctx_len // TB
    x_specs, x_args = _stream_specs(xs, TB, nctx)

    def pcol(width, c):
        return pl.BlockSpec((1, TB, width), lambda bi, i: (bi, i, c))

    def dirspec(d):
        return pl.BlockSpec((1, 1, TB, 512), lambda bi, i: (d, bi, i, 0))

    return pl.pallas_call(
        functools.partial(_merge_kernel, ctx_len=ctx_len, nstream=len(x_args)),
        grid=(b, lt // TB),
        in_specs=x_specs + [
                  _const_spec((6, D_MODEL)),
                  pl.BlockSpec((1, 6, D_MODEL), lambda bi, i: (bi, 0, 0)),
                  dirspec(0), dirspec(0), pcol(512, 2),
                  _const_spec((1, HG_W)), _const_spec((HG_W, HG_W)),
                  pl.BlockSpec((1, TB, MLA_W), lambda bi, i: (bi, jnp.minimum(i, nctx - 1), 0)),
                  pl.BlockSpec((1, TB, MLA_W), lambda bi, i: (bi, jnp.maximum(i - nctx, 0), 0)),
                  dirspec(0), dirspec(1), pcol(512, 5),
                  pcol(D_MODEL, 3), pcol(D_MODEL, 4), pcol(D_MODEL, 5),
                  _const_spec((3, 512, D_MODEL)), _const_spec((D_MODEL, D_MODEL))],
        out_specs=pl.BlockSpec((1, TB, D_MODEL), lambda bi, i: (bi, i, 0)),
        out_shape=jax.ShapeDtypeStruct((b, lt, D_MODEL), F32),
        compiler_params=_cparams(("parallel", "arbitrary")),
        name="merge",
    )(*x_args, mctx, mb, o_hg[0], o_hg[1], p, hg_gain, head_mean, o_mla_c, o_mla_x, h_lru, h_lru, p, p, p, p, w_br, w_out)


def _first_row_of_max(vals, row, valid):
    masked = jnp.where(valid, vals, -jnp.inf)
    m = jnp.max(masked, axis=0, keepdims=True)
    idx = jnp.min(jnp.where(masked == m, row, ROUTER_PAD), axis=0, keepdims=True)
    return m, idx


def _router(h, wr_hi, wr_lo, br):
    hi = h.astype(BF16)
    lo = (h - hi.astype(F32)).astype(BF16)
    logits = (jnp.dot(hi, wr_hi, preferred_element_type=F32) + jnp.dot(lo, wr_hi, preferred_element_type=F32)
              + jnp.dot(hi, wr_lo, preferred_element_type=F32))
    biased = (logits + br).T
    logits = logits.T
    row = lax.broadcasted_iota(jnp.int32, logits.shape, 0)
    is_group = jnp.logical_and(row >= N_EXPERTS, row < N_EXPERTS + N_GROUPS)
    _, g_row = _first_row_of_max(biased, row, is_group)
    g_max, _ = _first_row_of_max(logits, row, is_group)
    g_exp = jnp.where(is_group, jnp.exp(logits - g_max), 0.0)
    g_sel_logit = jnp.sum(jnp.where(row == g_row, logits, 0.0), axis=0, keepdims=True)
    p_g = jnp.exp(g_sel_logit - g_max) / jnp.sum(g_exp, axis=0, keepdims=True)
    in_group = jnp.right_shift(row, 2) == (g_row - N_EXPERTS)
    _, i1 = _first_row_of_max(biased, row, in_group)
    _, i2 = _first_row_of_max(biased, row, jnp.logical_and(in_group, row != i1))
    l1 = jnp.sum(jnp.where(row == i1, logits, 0.0), axis=0, keepdims=True)
    l2 = jnp.sum(jnp.where(row == i2, logits, 0.0), axis=0, keepdims=True)
    lm = jnp.maximum(l1, l2)
    e1 = jnp.exp(l1 - lm)
    e2 = jnp.exp(l2 - lm)
    inv = p_g / (e1 + e2)
    comb_t = jnp.where(row == i1, e1 * inv, 0.0) + jnp.where(row == i2, e2 * inv, 0.0)
    return comb_t.T


def _moe_kernel(x_ref, mctx_ref, mb_ref, gain_ref, wrh_ref, wrl_ref, br_ref, w1_ref, w3_ref, w2_ref, o_ref, *,
                ctx_len, tm, tile0):
    i = pl.program_id(1) + tile0
    epb = N_EXPERTS // EXPERT_STEPS
    nrows = min(MOE_RB, tm)
    for rb in range(tm // nrows):
        rows = slice(rb * nrows, (rb + 1) * nrows)
        tile = i * (tm // nrows) + rb
        shift = _row_mods(tile, nrows, ctx_len, mctx_ref, mb_ref, 3)
        scale = _row_mods(tile, nrows, ctx_len, mctx_ref, mb_ref, 4)
        gate = _row_mods(tile, nrows, ctx_len, mctx_ref, mb_ref, 5)
        x = x_ref[0, rows, :]
        h = _norm_modulate(x, gain_ref[...], shift, scale)
        comb = _router(h, wrh_ref[...], wrl_ref[...], br_ref[...])
        hb = h.astype(BF16)
        acc = None
        for es in range(EXPERT_STEPS):
            parts = []
            for e in range(es * epb, (es + 1) * epb):
                h1 = jnp.dot(hb, w1_ref[0, e], preferred_element_type=F32)
                h3 = jnp.dot(hb, w3_ref[0, e], preferred_element_type=F32)
                parts.append(h1 * _sigmoid(h1) * h3 * comb[:, e:e + 1])
            y = jnp.dot(jnp.concatenate(parts, axis=-1).astype(BF16), w2_ref[0, es], preferred_element_type=F32)
            acc = y if acc is None else acc + y
        o_ref[0, rows, :] = x + gate * acc


def _moe(xs, mctx, mb, gain, wr_hi, wr_lo, br, w1, w3, w2, layer, ctx_len, latent_only):
    b, lt, _ = xs.shape
    epb = N_EXPERTS // EXPERT_STEPS
    def layer_spec(shape):
        return pl.BlockSpec((1,) + shape, lambda bi, i: (layer,) + (0,) * len(shape), pipeline_mode=pl.Buffered(1))

    if latent_only:
        tm, tile0, rows = TB, ctx_len // TB, lt - ctx_len
    else:
        tm, tile0, rows = (MOE_TM if lt % MOE_TM == 0 else TB), 0, lt
    return pl.pallas_call(
        functools.partial(_moe_kernel, ctx_len=ctx_len, tm=tm, tile0=tile0),
        grid=(b, rows // tm),
        in_specs=[pl.BlockSpec((1, tm, D_MODEL), lambda bi, i: (bi, i + tile0, 0)),
                  _const_spec((6, D_MODEL)),
                  pl.BlockSpec((1, 6, D_MODEL), lambda bi, i: (bi, 0, 0)),
                  _const_spec((1, D_MODEL)),
                  _const_spec((D_MODEL, ROUTER_PAD)), _const_spec((D_MODEL, ROUTER_PAD)), _const_spec((1, ROUTER_PAD)),
                  layer_spec((N_EXPERTS, D_MODEL, D_EXPERT)), layer_spec((N_EXPERTS, D_MODEL, D_EXPERT)),
                  layer_spec((EXPERT_STEPS, epb * D_EXPERT, D_MODEL))],
        out_specs=pl.BlockSpec((1, tm, D_MODEL), lambda bi, i: (bi, i, 0)),
        out_shape=jax.ShapeDtypeStruct((b, rows, D_MODEL), F32),
        compiler_params=_cparams(("parallel", "arbitrary")),
        name="moe",
    )(xs, mctx, mb, gain, wr_hi, wr_lo, br, w1, w3, w2)


def _rope_tables(ctx_len, seq):
    half = MLA_ROPE // 2
    rows = seq // GRID_W
    pos_row = np.repeat(np.arange(rows, dtype=np.float32), GRID_W)
    pos_col = np.tile(np.arange(GRID_W, dtype=np.float32), rows)
    inv = (ROPE_BASE ** (-np.arange(0, half, 2, dtype=np.float32) / half)).astype(np.float32)
    ang = np.concatenate([pos_row[:, None] * inv, pos_col[:, None] * inv], axis=-1)
    cos, sin = np.cos(ang), np.sin(ang)
    cos_t = np.zeros((ctx_len + seq, HEAD_PAD), np.float32)
    sin_t = np.zeros((ctx_len + seq, HEAD_PAD), np.float32)
    cos_t[:, :MLA_NOPE] = 1.0
    cos_t[:ctx_len, MLA_NOPE:MLA_QK] = 1.0
    cos_t[ctx_len:, MLA_NOPE:MLA_NOPE + half] = cos
    cos_t[ctx_len:, MLA_NOPE + half:MLA_QK] = cos
    sin_t[ctx_len:, MLA_NOPE:MLA_NOPE + half] = -sin
    sin_t[ctx_len:, MLA_NOPE + half:MLA_QK] = sin
    return cos_t, sin_t


def _rope_key_selectors():
    half = MLA_ROPE // 2
    kr0 = MLA_Q_RANK + MLA_KV_RANK
    sk1 = np.zeros((512, MLA_HEADS * HEAD_PAD), np.float32)
    sk2 = np.zeros((512, MLA_HEADS * HEAD_PAD), np.float32)
    for h in range(MLA_HEADS):
        for i in range(MLA_ROPE):
            sk1[kr0 + i, h * HEAD_PAD + MLA_NOPE + i] = 1.0
            sk2[kr0 + (i + half) % MLA_ROPE, h * HEAD_PAD + MLA_NOPE + i] = 1.0
    return sk1.astype(BF16), sk2.astype(BF16)


def _np_block_diag(block, n):
    a, bb = block.shape
    out = np.zeros((n * a, n * bb), np.float32)
    for i in range(n):
        out[i * a:(i + 1) * a, i * bb:(i + 1) * bb] = block
    return out


def _swap_rope_halves(t):
    half = MLA_ROPE // 2
    return jnp.concatenate([jnp.zeros_like(t[..., :MLA_NOPE]), t[..., MLA_NOPE + half:], t[..., MLA_NOPE:MLA_NOPE + half]],
                           axis=-1)


def _pad_heads(t):
    pad = [(0, 0)] * (t.ndim - 1) + [(0, HEAD_PAD - t.shape[-1])]
    t = jnp.pad(t, pad)
    return t.reshape(t.shape[:-2] + (t.shape[-2] * HEAD_PAD,))


def _mla_weights(w_uq, w_ukv, gq, gk):
    wq = w_uq.reshape(MLA_Q_RANK, MLA_HEADS, MLA_QK)
    wq1 = _pad_heads(wq).astype(BF16)
    wq2 = _pad_heads(_swap_rope_halves(wq)).astype(BF16)
    wkv = w_ukv.reshape(MLA_KV_RANK, MLA_HEADS, MLA_NOPE + MLA_V)
    wk1 = _pad_heads(wkv[..., :MLA_NOPE]).astype(BF16)
    wv = _pad_heads(wkv[..., MLA_NOPE:]).astype(BF16)

    def gains(g):
        g1 = jnp.pad(g, (0, HEAD_PAD - MLA_QK))
        g2 = jnp.pad(_swap_rope_halves(g), (0, HEAD_PAD - MLA_QK))
        return jnp.stack([g1, g2], axis=0)

    return wq1, wq2, wk1, wv, gains(gq), gains(gk)


def _block_diag(w):
    n, a, bb = w.shape
    eye = jnp.eye(n, dtype=w.dtype)
    return (eye[:, None, :, None] * w[:, :, None, :]).reshape(n * a, n * bb)


def _pad_rows(t, rows=8):
    return jnp.pad(t, ((0, rows - t.shape[0]), (0, 0)))


def kernel(x, c, ctx, c_ctx, w_mod, b_mod, norm_mix, norm_ffn, w_in, hg_lb, hg_norm, mla_q_norm, mla_kv_norm, mla_w_uq, mla_w_ukv, mla_qk_gain_q, mla_qk_gain_k, lru_conv_w, lru_conv_b, lru_wa, lru_ba, lru_wx, lru_bx, lru_lambda, w_br_hg, w_br_mla, w_br_lru, w_out, moe_w_rg, moe_b_rg, moe_w_re, moe_b_re, moe_w1, moe_w3, moe_w2):
    bsz, seq, _ = x.shape
    ctx_len = ctx.shape[1]
    depth = w_in.shape[0]
    assert seq % TB == 0 and ctx_len % TB == 0 and seq % GRID_W == 0 and bsz < 8
    nctx = ctx_len // TB

    xs = (ctx, x)
    cos_t, sin_t = _rope_tables(ctx_len, seq)
    cvec = jnp.zeros((8, D_MODEL), F32).at[:bsz].set(c).at[bsz].set(c_ctx)

    lb_cs = jnp.cumsum(jax.nn.softmax(hg_lb.astype(F32), axis=0), axis=0)
    lb_all = lb_cs - lb_cs[0:1]
    tri_lo = _np_block_diag(np.tril(np.ones((HG_CHUNK, HG_CHUNK), np.float32)), TB // HG_CHUNK)
    tri = np.stack([tri_lo, tri_lo.T], axis=0).astype(BF16)
    head_ones = _np_block_diag(np.ones((HG_DK, HG_DK), np.float32), HG_HEADS)
    head_mean = (head_ones / HG_DK).astype(BF16)
    sk1, sk2 = _rope_key_selectors()

    w_all = _win_layout(w_in)
    moe_w1b, moe_w3b = moe_w1.astype(BF16), moe_w3.astype(BF16)
    moe_w2b = moe_w2.astype(BF16).reshape(depth, EXPERT_STEPS, N_EXPERTS // EXPERT_STEPS * D_EXPERT, D_MODEL)

    mods = _modulation(cvec, w_mod, b_mod).reshape(depth, 8, 6, D_MODEL)

    for l in range(depth):
        mctx, mb = mods[l, bsz], mods[l, :bsz]

        f, p = _inproj(xs, mctx, mb, norm_mix[l][None, :], w_all, l, ctx_len)

        lb = lb_all[l]
        lbc = jnp.stack([jnp.log(lb), jnp.log1p(-lb), 1.0 - lb] + [jnp.zeros_like(lb)] * 5, axis=1)
        o_hg = _hgrn(p, f, tri, lbc, head_ones, nctx)

        wq1, wq2, wk1, wv, gq, gk = _mla_weights(mla_w_uq[l], mla_w_ukv[l], mla_qk_gain_q[l], mla_qk_gain_k[l])
        qt, kh, vt, qsq, ksq = _mla_prep(p, cos_t, sin_t, mla_q_norm[l][None, :], mla_kv_norm[l][None, :], gq, gk,
                                    wq1, wq2, wk1, sk1, sk2, wv, nctx)
        bound = jnp.sqrt(jnp.max(qsq, axis=(1, 3)) * jnp.max(ksq, axis=(1, 3))).reshape(bsz, MLA_HEADS // 2, 2).max(axis=-1)
        tq = next(t for t in (512, 256) if seq % t == 0)
        o_mla_x = _flash(bound, qt, kh, vt, seq, ctx_len + seq, tq, 0, "mla_attention")
        if l < depth - 1:
            o_mla_c = _flash(bound, qt, kh, vt, ctx_len, ctx_len, TB, seq, "mla_attention_ctx")
        else:
            o_mla_c = o_mla_x[:, :ctx_len]

        wg = jnp.stack([jnp.concatenate([_block_diag(lru_wa[l, d]), _block_diag(lru_wx[l, d])], axis=-1)
                        for d in range(2)], axis=0).astype(BF16)
        bg = jnp.concatenate([lru_ba[l], lru_bx[l]], axis=-1)[:, None, :] * jnp.ones((1, 8, 1), F32)
        lam = lru_lambda[l][:, None, :] * jnp.ones((1, 8, 1), F32)
        h_lru = _lru(p, _pad_rows(lru_conv_w[l]), lru_conv_b[l][None, :], wg, bg, lam, nctx)

        w_br = jnp.stack([w_br_hg[l], w_br_mla[l], w_br_lru[l]], axis=0).astype(BF16)
        hg_gain = jnp.tile(hg_norm[l], HG_HEADS)[None, :]
        xs = _merge(xs, mctx, mb, o_hg, p, hg_gain, head_mean, o_mla_c, o_mla_x, h_lru, w_br, w_out[l].astype(BF16), ctx_len)

        wr = jnp.pad(jnp.concatenate([moe_w_re[l], moe_w_rg[l]], axis=-1), ((0, 0), (0, ROUTER_PAD - N_EXPERTS - N_GROUPS)))
        br = jnp.pad(jnp.concatenate([moe_b_re[l], moe_b_rg[l]]), (0, ROUTER_PAD - N_EXPERTS - N_GROUPS))[None, :]
        wr_hi = wr.astype(BF16)
        wr_lo = (wr - wr_hi.astype(F32)).astype(BF16)
        xs = _moe(xs, mctx, mb, norm_ffn[l][None, :], wr_hi, wr_lo, br, moe_w1b, moe_w3b, moe_w2b, l,
                  ctx_len, latent_only=(l == depth - 1))

    return xs
```

```python
import functools

import numpy as np
import jax
import jax.numpy as jnp
from jax import lax
from jax.experimental import pallas as pl
from jax.experimental.pallas import tpu as pltpu

F32 = jnp.float32
BF16 = jnp.bfloat16
HIGHEST = lax.Precision.HIGHEST

D_MODEL = 1024
GRID_W = 64
EPS = 1e-6

HG_HEADS = 8
HG_DK = 64
HG_W = 512
HG_CHUNK = 64
HG_MID = HG_CHUNK // 2
HG_MAX_LOG_RANGE = 80.0

MLA_HEADS = 8
MLA_Q_RANK = 256
MLA_KV_RANK = 128
MLA_NOPE = 64
MLA_ROPE = 32
MLA_V = 64
MLA_QK = MLA_NOPE + MLA_ROPE
MLA_W = MLA_HEADS * MLA_V
HEAD_PAD = 128
ATTN_HEADS_PER_STEP = 4
V_ROWS = 80
SCORE_SCALE = MLA_QK ** -0.5 * 1.4426950408889634
MAX_UNSHIFTED_SCORE = 57.0
ROPE_BASE = 10000.0

LRU_W = 512
LRU_BLOCKS = 8
LRU_BD = LRU_W // LRU_BLOCKS
CONV_W = 4
LRU_C = 8.0

N_GROUPS = 4
EXP_PER_GROUP = 4
N_EXPERTS = N_GROUPS * EXP_PER_GROUP
D_EXPERT = 256
EXPERT_STEPS = 4
ROUTER_PAD = 128
MOE_TM = 768
MOE_RB = 256
INPROJ_TM = 768

TB = 256
HALO = 16
SCAN_GROUP = 8
P_WIDTH = 6144
F_WIDTH = 1024
VMEM_LIMIT = 56 * 1024 * 1024

NT_DIMS = (((1,), (1,)), ((), ()))


def _cparams(sem):
    return pltpu.CompilerParams(dimension_semantics=sem, vmem_limit_bytes=VMEM_LIMIT)


def _const_spec(shape):
    nd = len(shape)
    return pl.BlockSpec(shape, lambda *_: (0,) * nd, pipeline_mode=pl.Buffered(1))


def _sigmoid(t):
    return 0.5 * jnp.tanh(0.5 * t) + 0.5


def _scan_block(j, d, nctx, nblk):
    fwd = j
    bwd = jnp.where(j < nctx, nctx - 1 - j, nblk - 1 - (j - nctx))
    return jnp.where(d == 0, fwd, bwd)


def _mod_kernel(c_ref, w_ref, b_ref, o_ref):
    c = c_ref[...]
    s = c * _sigmoid(c)
    o_ref[0] = jnp.dot(s, w_ref[0], precision=HIGHEST, preferred_element_type=F32) + b_ref[0]


def _modulation(cvec, w_mod, b_mod):
    depth, _, n = w_mod.shape
    tn = 1024
    return pl.pallas_call(
        _mod_kernel,
        grid=(depth, n // tn),
        in_specs=[pl.BlockSpec((8, D_MODEL), lambda l, j: (0, 0)),
                  pl.BlockSpec((1, D_MODEL, tn), lambda l, j: (l, 0, j)),
                  pl.BlockSpec((1, 1, tn), lambda l, j: (l, 0, j))],
        out_specs=pl.BlockSpec((1, 8, tn), lambda l, j: (l, 0, j)),
        out_shape=jax.ShapeDtypeStruct((depth, 8, n), F32),
        compiler_params=_cparams(("arbitrary", "arbitrary")),
        name="modulation",
    )(cvec, w_mod, b_mod.reshape(depth, 1, n))


def _row_mods(i, tm, ctx_len, mctx_ref, mb_ref, k):
    row = i * tm + lax.broadcasted_iota(jnp.int32, (tm, 1), 0)
    return jnp.where(row < ctx_len, mctx_ref[k:k + 1, :], mb_ref[0, k:k + 1, :])


def _norm_modulate(x, gain, shift, scale):
    ms = jnp.mean(x * x, axis=-1, keepdims=True)
    xn = x * lax.rsqrt(ms + EPS) * gain
    return xn * (1.0 + scale) + shift


def _stream_specs(xs, tm, nctx):
    if isinstance(xs, tuple):
        assert tm == TB
        return ([pl.BlockSpec((1, tm, D_MODEL), lambda bi, i: (bi, jnp.minimum(i, nctx - 1), 0)),
                 pl.BlockSpec((1, tm, D_MODEL), lambda bi, i: (bi, jnp.maximum(i - nctx, 0), 0))], list(xs))
    return [pl.BlockSpec((1, tm, D_MODEL), lambda bi, i: (bi, i, 0))], [xs]


def _stream_tile(x_refs, i, nctx):
    if len(x_refs) == 2:
        return jnp.where(i < nctx, x_refs[0][0], x_refs[1][0])
    return x_refs[0][0]


def _win_layout_kernel(w_ref, o_ref):
    def put(dst, src, width):
        o_ref[0, :, dst:dst + width] = w_ref[0, :, src:src + width].astype(BF16)

    mla_w = MLA_Q_RANK + MLA_KV_RANK + MLA_ROPE
    mla0 = 5 * HG_W
    put(0, HG_W, 2 * HG_W)
    put(F_WIDTH, 0, HG_W)
    put(F_WIDTH + HG_W, 3 * HG_W, 2 * HG_W)
    put(F_WIDTH + 3 * HG_W, mla0, mla_w)
    o_ref[0, :, F_WIDTH + 3 * HG_W + mla_w:F_WIDTH + 4 * HG_W] = jnp.zeros((o_ref.shape[1], HG_W - mla_w), BF16)
    rest = w_ref.shape[2] - mla0 - mla_w
    put(F_WIDTH + 4 * HG_W, mla0 + mla_w, rest)


def _win_layout(w_in):
    depth, d, n = w_in.shape
    tr = 256
    return pl.pallas_call(
        _win_layout_kernel,
        grid=(depth, d // tr),
        in_specs=[pl.BlockSpec((1, tr, n), lambda l, i: (l, i, 0))],
        out_specs=pl.BlockSpec((1, tr, F_WIDTH + P_WIDTH), lambda l, i: (l, i, 0)),
        out_shape=jax.ShapeDtypeStruct((depth, d, F_WIDTH + P_WIDTH), BF16),
        compiler_params=_cparams(("arbitrary", "arbitrary")),
        name="w_in_layout",
    )(w_in)


def _inproj_kernel(*refs, ctx_len, tm, nstream):
    x_refs = refs[:nstream]
    mctx_ref, mb_ref, gain_ref, w_ref, f_ref, p_ref = refs[nstream:]
    w_ref = w_ref.at[0]
    i = pl.program_id(1)
    shift = _row_mods(i, tm, ctx_len, mctx_ref, mb_ref, 0)
    scale = _row_mods(i, tm, ctx_len, mctx_ref, mb_ref, 1)
    h = _norm_modulate(_stream_tile(x_refs, i, ctx_len // TB), gain_ref[...], shift, scale).astype(BF16)
    f_ref[0] = jnp.dot(h, w_ref[:, 0:F_WIDTH], preferred_element_type=F32)
    cw = 512
    for j in range(P_WIDTH // cw):
        lo = F_WIDTH + j * cw
        p_ref[0, :, j * cw:(j + 1) * cw] = jnp.dot(
            h, w_ref[:, lo:lo + cw], preferred_element_type=F32).astype(BF16)


def _inproj(xs, mctx, mb, gain, w_all, layer, ctx_len):
    split = isinstance(xs, tuple)
    b = (xs[0] if split else xs).shape[0]
    lt = sum(t.shape[1] for t in xs) if split else xs.shape[1]
    tm = INPROJ_TM if lt % INPROJ_TM == 0 and not split else TB
    x_specs, x_args = _stream_specs(xs, tm, ctx_len // TB)
    return pl.pallas_call(
        functools.partial(_inproj_kernel, ctx_len=ctx_len, tm=tm, nstream=len(x_args)),
        grid=(b, lt // tm),
        in_specs=x_specs + [
                  _const_spec((6, D_MODEL)),
                  pl.BlockSpec((1, 6, D_MODEL), lambda bi, i: (bi, 0, 0)),
                  _const_spec((1, D_MODEL)),
                  pl.BlockSpec((1, D_MODEL, F_WIDTH + P_WIDTH), lambda bi, i: (layer, 0, 0), pipeline_mode=pl.Buffered(1))],
        out_specs=[pl.BlockSpec((1, tm, F_WIDTH), lambda bi, i: (bi, i, 0)),
                   pl.BlockSpec((1, tm, P_WIDTH), lambda bi, i: (bi, i, 0))],
        out_shape=[jax.ShapeDtypeStruct((b, lt, F_WIDTH), F32),
                   jax.ShapeDtypeStruct((b, lt, P_WIDTH), BF16)],
        compiler_params=_cparams(("parallel", "arbitrary")),
        name="inproj",
    )(*x_args, mctx, mb, gain, w_all)


def _hgrn_direction(q_ref, v_ref, f_ref, tri_ref, lbc_ref, bd_ref, o_ref, st_ref, c_scr, k_scr, v_scr, reverse):
    d = 1 if reverse else 0
    nch = TB // HG_CHUNK
    npair = HG_W // 128
    first, last = (HG_CHUNK - 1, 0) if reverse else (0, HG_CHUNK - 1)

    log_lb = lbc_ref[d, 0:1, :]
    log_1mlb = lbc_ref[d, 1:2, :]
    one_mlb = lbc_ref[d, 2:3, :]
    z = f_ref[0]
    q = q_ref[0].astype(F32)
    v = v_ref[0].astype(F32)
    e = jnp.exp(-jnp.abs(z))
    log_sig = jnp.minimum(z, 0.0) - jnp.log(1.0 + e)
    t = log_1mlb + log_sig
    logf = jnp.maximum(log_lb, t) + jnp.log(1.0 + jnp.exp(-jnp.abs(log_lb - t)))
    k = one_mlb * (jnp.where(z >= 0, e, 1.0) / (1.0 + e))
    hi = logf.astype(BF16)
    lo = (logf - hi.astype(F32)).astype(BF16)
    tri = tri_ref[d]
    c = jnp.dot(tri, hi, preferred_element_type=F32) + jnp.dot(tri, lo, preferred_element_type=F32)

    span = jnp.zeros((1, HG_W), F32)
    for ci in range(nch):
        r0 = ci * HG_CHUNK
        c_mid = c[r0 + HG_MID:r0 + HG_MID + 1, :]
        span = jnp.maximum(span, jnp.maximum(c[r0 + first:r0 + first + 1, :] - c_mid, c_mid - c[r0 + last:r0 + last + 1, :]))
    qmax = jnp.max(jnp.abs(q), axis=0, keepdims=True)
    span = jnp.max(span + jnp.log(jnp.maximum(qmax, 1.0)))

    lane = lax.broadcasted_iota(jnp.int32, (HG_CHUNK, HG_W), 1)
    low_head = (lane & (HG_DK * 2 - 1)) < HG_DK
    ri = lax.broadcasted_iota(jnp.int32, (HG_CHUNK, 128), 0)
    si = lax.broadcasted_iota(jnp.int32, (HG_CHUNK, 128), 1) & (HG_DK - 1)
    visited = (si >= ri) if reverse else (si <= ri)
    r2 = lax.broadcasted_iota(jnp.int32, (128, 128), 0)
    l2 = lax.broadcasted_iota(jnp.int32, (128, 128), 1)
    same_head = (r2 < HG_DK) == (l2 < HG_DK)
    rowi = lax.broadcasted_iota(jnp.int32, (HG_CHUNK, 1), 0)

    def split_heads(t, transpose=False):
        top = jnp.where(low_head, t, 0.0)
        bot = jnp.where(low_head, 0.0, t)
        blocks = [jnp.concatenate([top[:, 128 * g:128 * (g + 1)], bot[:, 128 * g:128 * (g + 1)]], axis=0) for g in range(npair)]
        return [(blk.T if transpose else blk).astype(BF16) for blk in blocks]

    def exact_intra(qq, kk, vv, cc):
        c_scr[...] = cc
        k_scr[...] = kk
        v_scr[...] = vv
        bd = bd_ref[...]

        def key_row(s, acc):
            c_s = c_scr[pl.ds(s, 1), :]
            w = qq * jnp.exp(jnp.minimum(cc - c_s, 0.0)) * k_scr[pl.ds(s, 1), :]
            ws = jnp.dot(w, bd, precision=HIGHEST, preferred_element_type=F32)
            ok = (rowi <= s) if reverse else (rowi >= s)
            return acc + jnp.where(ok, ws, 0.0) * v_scr[pl.ds(s, 1), :]

        return lax.fori_loop(0, HG_CHUNK, key_row, jnp.zeros((HG_CHUNK, HG_W), F32))

    def chunk(ci, exact):
        r0 = ci * HG_CHUNK
        rows = slice(r0, r0 + HG_CHUNK)
        cc, qq, kk, vv = c[rows], q[rows], k[rows], v[rows]
        c_mid = cc[HG_MID:HG_MID + 1, :]
        c_end = cc[last:last + 1, :]
        qs = (qq * jnp.exp(cc)).astype(BF16)
        ks = (kk * jnp.exp(c_end - cc)).astype(BF16)
        dec = jnp.exp(c_end)
        if exact:
            intra = exact_intra(qq, kk, vv, cc)
        else:
            qd = (qq * jnp.exp(cc - c_mid)).astype(BF16)
            kd_blocks = split_heads(kk * jnp.exp(c_mid - cc), transpose=True)
            v_blocks = split_heads(vv)
        pieces = []
        for g in range(npair):
            ls = slice(128 * g, 128 * (g + 1))
            st = st_ref[d, g]
            o_g = jnp.dot(qs[:, ls], st.T.astype(BF16), preferred_element_type=F32)
            if not exact:
                a = jnp.dot(qd[:, ls], kd_blocks[g], preferred_element_type=F32)
                a = jnp.where(visited, a, 0.0).astype(BF16)
                o_g = o_g + jnp.dot(a, v_blocks[g], preferred_element_type=F32)
            vt = vv[:, ls].T.astype(BF16)
            kv = jnp.dot(vt, ks[:, ls], preferred_element_type=F32)
            st_ref[d, g] = st * dec[:, ls] + jnp.where(same_head, kv, 0.0)
            pieces.append(o_g)
        o = jnp.concatenate(pieces, axis=-1)
        o_ref[0, 0, rows, :] = o + intra if exact else o

    order = list(range(nch - 1, -1, -1) if reverse else range(nch))
    return span, lambda n, exact: chunk(order[n], exact)


def _hgrn_kernel(qf_ref, vf_ref, ff_ref, qb_ref, vb_ref, fb_ref, tri_ref, lbc_ref, bd_ref, of_ref, ob_ref,
                 st_ref, c_scr, k_scr, v_scr):
    @pl.when(pl.program_id(1) == 0)
    def _():
        st_ref[...] = jnp.zeros_like(st_ref)

    span_f, chunk_f = _hgrn_direction(qf_ref, vf_ref, ff_ref, tri_ref, lbc_ref, bd_ref, of_ref, st_ref,
                                      c_scr, k_scr, v_scr, False)
    span_b, chunk_b = _hgrn_direction(qb_ref, vb_ref, fb_ref, tri_ref, lbc_ref, bd_ref, ob_ref, st_ref,
                                      c_scr, k_scr, v_scr, True)
    span = jnp.maximum(span_f, span_b)

    def run(exact):
        for n in range(TB // HG_CHUNK):
            chunk_f(n, exact)
            chunk_b(n, exact)

    @pl.when(span <= HG_MAX_LOG_RANGE)
    def _matmul_form():
        run(False)

    @pl.when(span > HG_MAX_LOG_RANGE)
    def _exact_form():
        run(True)


def _hgrn(p, f, tri, lbc, bd, nctx):
    b, lt, _ = p.shape
    nblk = lt // TB

    def spec(d, col):
        return pl.BlockSpec((1, TB, HG_W), lambda bi, j: (bi, _scan_block(j, d, nctx, nblk), col))

    def out_spec(d):
        return pl.BlockSpec((1, 1, TB, HG_W), lambda bi, j: (0, bi, _scan_block(j, d, nctx, nblk), 0))

    out = jax.ShapeDtypeStruct((1, b, lt, HG_W), F32)
    return pl.pallas_call(
        _hgrn_kernel,
        grid=(b, nblk),
        in_specs=[spec(0, 0), spec(0, 1), spec(0, 0),
                  spec(1, 0), spec(1, 1), spec(1, 1),
                  _const_spec((2, TB, TB)), _const_spec((2, 8, HG_W)), _const_spec((HG_W, HG_W))],
        out_specs=[out_spec(0), out_spec(1)],
        out_shape=[out, out],
        scratch_shapes=[pltpu.VMEM((2, HG_W // 128, 128, 128), F32),
                        pltpu.VMEM((HG_CHUNK, HG_W), F32),
                        pltpu.VMEM((HG_CHUNK, HG_W), F32),
                        pltpu.VMEM((HG_CHUNK, HG_W), F32)],
        compiler_params=_cparams(("parallel", "arbitrary")),
        name="hgrn2",
    )(p, p, f, p, p, f, tri, lbc, bd)


def _rms_rows(t, width):
    return lax.rsqrt(jnp.sum(t * t, axis=-1, keepdims=True) * (1.0 / width) + EPS)


def _mla_prep_kernel(pm_ref, cos_ref, sin_ref, qn_ref, kvn_ref, gq_ref, gk_ref,
                     wq1_ref, wq2_ref, wk1_ref, sk1_ref, sk2_ref, wv_ref,
                     qt_ref, k_ref, vt_ref, qsq_ref, ksq_ref):
    pm = pm_ref[0]
    dq = pm[:, 0:MLA_Q_RANK].astype(F32)
    dkv = pm[:, MLA_Q_RANK:MLA_Q_RANK + MLA_KV_RANK].astype(F32)
    dqn = (dq * _rms_rows(dq, MLA_Q_RANK) * qn_ref[...]).astype(BF16)
    dkvn = (dkv * _rms_rows(dkv, MLA_KV_RANK) * kvn_ref[...]).astype(BF16)
    q1 = jnp.dot(dqn, wq1_ref[...], preferred_element_type=F32)
    q2 = jnp.dot(dqn, wq2_ref[...], preferred_element_type=F32)
    k1 = (jnp.dot(dkvn, wk1_ref[...], preferred_element_type=F32)
          + jnp.dot(pm, sk1_ref[...], preferred_element_type=F32))
    k2 = jnp.dot(pm, sk2_ref[...], preferred_element_type=F32)
    vv = jnp.dot(dkvn, wv_ref[...], preferred_element_type=F32)
    cos = cos_ref[...]
    sin = sin_ref[...]
    gq_c = gq_ref[0:1, :] * cos
    gq_s = gq_ref[1:2, :] * sin
    gk_c = gk_ref[0:1, :] * cos
    gk_s = gk_ref[1:2, :] * sin
    vrow = lax.broadcasted_iota(jnp.int32, (V_ROWS, TB), 0)
    qn, kn = [], []

    def max_sq_norm(t):
        return jnp.broadcast_to(jnp.max(jnp.sum(t * t, axis=-1, keepdims=True), axis=0, keepdims=True), (1, HEAD_PAD))

    for h in range(MLA_HEADS):
        sl = slice(HEAD_PAD * h, HEAD_PAD * (h + 1))
        qh = q1[:, sl]
        qo = (qh * gq_c + q2[:, sl] * gq_s) * (_rms_rows(qh, MLA_QK) * SCORE_SCALE)
        qt_ref[0, h] = qo.T.astype(BF16)
        qn.append(max_sq_norm(qo))
        kh = k1[:, sl]
        ko = (kh * gk_c + k2[:, sl] * gk_s) * _rms_rows(kh, MLA_QK)
        k_ref[0, h] = ko.astype(BF16)
        kn.append(max_sq_norm(ko))
        vt = vv[:, sl].T[0:V_ROWS, :]
        vt_ref[0, h] = jnp.where(vrow == MLA_V, 1.0, vt).astype(BF16)
    qsq_ref[0, 0] = jnp.concatenate(qn, axis=0)
    ksq_ref[0, 0] = jnp.concatenate(kn, axis=0)


def _mla_prep(p, cos_t, sin_t, qn, kvn, gq, gk, wq1, wq2, wk1, sk1, sk2, wv, nctx):
    b, lt, _ = p.shape
    hw = MLA_HEADS * HEAD_PAD
    nblk = lt // TB
    return pl.pallas_call(
        _mla_prep_kernel,
        grid=(b, lt // TB),
        in_specs=[pl.BlockSpec((1, TB, 512), lambda bi, i: (bi, i, 3)),
                  pl.BlockSpec((TB, HEAD_PAD), lambda bi, i: (i, 0)),
                  pl.BlockSpec((TB, HEAD_PAD), lambda bi, i: (i, 0)),
                  _const_spec((1, MLA_Q_RANK)), _const_spec((1, MLA_KV_RANK)),
                  _const_spec((2, HEAD_PAD)), _const_spec((2, HEAD_PAD)),
                  _const_spec((MLA_Q_RANK, hw)), _const_spec((MLA_Q_RANK, hw)),
                  _const_spec((MLA_KV_RANK, hw)), _const_spec((512, hw)), _const_spec((512, hw)),
                  _const_spec((MLA_KV_RANK, hw))],
        out_specs=[pl.BlockSpec((1, MLA_HEADS, HEAD_PAD, TB), lambda bi, i: (bi, 0, 0, (i + nblk - nctx) % nblk)),
                   pl.BlockSpec((1, MLA_HEADS, TB, HEAD_PAD), lambda bi, i: (bi, 0, i, 0)),
                   pl.BlockSpec((1, MLA_HEADS, V_ROWS, TB), lambda bi, i: (bi, 0, 0, i)),
                   pl.BlockSpec((1, 1, MLA_HEADS, HEAD_PAD), lambda bi, i: (bi, i, 0, 0)),
                   pl.BlockSpec((1, 1, MLA_HEADS, HEAD_PAD), lambda bi, i: (bi, i, 0, 0))],
        out_shape=[jax.ShapeDtypeStruct((b, MLA_HEADS, HEAD_PAD, lt), BF16),
                   jax.ShapeDtypeStruct((b, MLA_HEADS, lt, HEAD_PAD), BF16),
                   jax.ShapeDtypeStruct((b, MLA_HEADS, V_ROWS, lt), BF16),
                   jax.ShapeDtypeStruct((b, lt // TB, MLA_HEADS, HEAD_PAD), F32),
                   jax.ShapeDtypeStruct((b, lt // TB, MLA_HEADS, HEAD_PAD), F32)],
        compiler_params=_cparams(("parallel", "arbitrary")),
        name="mla_prep",
    )(p, cos_t, sin_t, qn, kvn, gq, gk, wq1, wq2, wk1, sk1, sk2, wv)


def _attend_bounded(qts, k_ref, vt_ref, s_scr, nkeys, tk):
    n = nkeys // tk

    def scores(kb, slot):
        for hh in range(len(qts)):
            s_scr[slot, hh, 0:tk, :] = jnp.dot(k_ref[0, hh, kb * tk:(kb + 1) * tk, :], qts[hh], preferred_element_type=F32)

    def accumulate(kb, slot, accs):
        out = []
        for hh in range(len(qts)):
            p = jnp.exp2(s_scr[slot, hh, 0:tk, :]).astype(BF16)
            pv = jnp.dot(vt_ref[0, hh, :, kb * tk:(kb + 1) * tk], p, preferred_element_type=F32)
            out.append(pv if accs is None else accs[hh] + pv)
        return out

    scores(0, 0)
    accs = None
    for kb in range(n):
        if kb + 1 < n:
            scores(kb + 1, (kb + 1) % 2)
        accs = accumulate(kb, kb % 2, accs)
    return accs


def _attend_online(qts, k_ref, vt_ref, nkeys, tk):
    def kv_step(kb, carry):
        r0 = pl.multiple_of(kb * tk, tk)
        out = []
        for hh in range(len(qts)):
            m, acc = carry[hh]
            s = jnp.dot(k_ref[0, hh, pl.ds(r0, tk), :], qts[hh], preferred_element_type=F32)
            m_new = jnp.maximum(m, jnp.max(s, axis=0, keepdims=True))
            p = jnp.exp2(s - m_new).astype(BF16)
            acc = jnp.exp2(m - m_new) * acc + jnp.dot(vt_ref[0, hh, :, pl.ds(r0, tk)], p, preferred_element_type=F32)
            out.append((m_new, acc))
        return tuple(out)

    tq = qts[0].shape[1]
    init = (jnp.full((1, tq), -jnp.inf, F32), jnp.zeros((V_ROWS, tq), F32))
    res = lax.fori_loop(0, nkeys // tk, kv_step, (init,) * len(qts))
    return [r[1] for r in res]


def _flash_kernel(bound_ref, qt_ref, k_ref, vt_ref, o_ref, s_scr, *, nkeys, tk):
    qts = [qt_ref[0, hh] for hh in range(ATTN_HEADS_PER_STEP)]
    worst = bound_ref[pl.program_id(0), pl.program_id(1)]

    def finish(accs):
        outs = [(acc[0:MLA_V, :] / acc[MLA_V:MLA_V + 1, :]).T for acc in accs]
        o_ref[0] = jnp.concatenate(outs, axis=-1).astype(BF16)

    @pl.when(worst <= MAX_UNSHIFTED_SCORE)
    def _():
        finish(_attend_bounded(qts, k_ref, vt_ref, s_scr, nkeys, tk))

    @pl.when(worst > MAX_UNSHIFTED_SCORE)
    def _():
        finish(_attend_online(qts, k_ref, vt_ref, nkeys, tk))


def _flash(bound, qt, k, vt, nq, nkeys, tq, q_col0, name):
    b = k.shape[0]
    tk = next(t for t in (768, 512, 256) if nkeys % t == 0)
    cb0 = q_col0 // tq
    hps = ATTN_HEADS_PER_STEP
    return pl.pallas_call(
        functools.partial(_flash_kernel, nkeys=nkeys, tk=tk),
        grid=(b, MLA_HEADS // hps, nq // tq),
        in_specs=[pl.BlockSpec(memory_space=pltpu.SMEM),
                  pl.BlockSpec((1, hps, HEAD_PAD, tq), lambda bi, g, i: (bi, g, 0, cb0 + i)),
                  pl.BlockSpec((1, hps, nkeys, HEAD_PAD), lambda bi, g, i: (bi, g, 0, 0)),
                  pl.BlockSpec((1, hps, V_ROWS, nkeys), lambda bi, g, i: (bi, g, 0, 0))],
        out_specs=pl.BlockSpec((1, tq, hps * MLA_V), lambda bi, g, i: (bi, i, g)),
        out_shape=jax.ShapeDtypeStruct((b, nq, MLA_W), BF16),
        scratch_shapes=[pltpu.VMEM((2, hps, tk, tq), F32)],
        compiler_params=_cparams(("parallel", "parallel", "arbitrary")),
        name=name,
    )(bound, qt, k, vt)


def _lru_kernel(x_ref, xp_ref, xn_ref, cw_ref, cb_ref, wg_ref, bg_ref, lam_ref, o_ref, xe_scr, h_scr, *, nctx, nblk):
    d = pl.program_id(1)
    j = pl.program_id(2)
    blk = _scan_block(j, d, nctx, nblk)

    @pl.when(j == 0)
    def _():
        h_scr[...] = jnp.zeros_like(h_scr)

    keep_prev = jnp.logical_and(blk != 0, blk != nctx)
    keep_next = jnp.logical_and(blk != nctx - 1, blk != nblk - 1)
    xe_scr[0:HALO, :] = jnp.where(keep_prev, xp_ref[0].astype(F32), 0.0)
    xe_scr[HALO:HALO + TB, :] = x_ref[0].astype(F32)
    xe_scr[HALO + TB:2 * HALO + TB, :] = jnp.where(keep_next, xn_ref[0].astype(F32), 0.0)
    left = CONV_W // 2
    u = jnp.broadcast_to(cb_ref[...], (TB, LRU_W))
    for tap in range(CONV_W):
        u = u + xe_scr[pl.ds(HALO - left + tap, TB), :] * cw_ref[tap:tap + 1, :]

    g = jnp.dot(u.astype(BF16), wg_ref[0], preferred_element_type=F32) + bg_ref[0, 0:1, :]
    r = _sigmoid(g[:, 0:LRU_W])
    ig = _sigmoid(g[:, LRU_W:2 * LRU_W])
    neg_lam = -lam_ref[0, 0:1, :]
    softplus = jnp.maximum(neg_lam, 0.0) + jnp.log1p(jnp.exp(-jnp.abs(neg_lam)))
    a = jnp.exp(-LRU_C * r * softplus)
    bb = jnp.sqrt(1.0 - a * a) * (ig * u)
    def scan(forward):
        ngroups = TB // SCAN_GROUP
        aa = a.reshape(ngroups, SCAN_GROUP, LRU_W)
        hh = bb.reshape(ngroups, SCAN_GROUP, LRU_W)
        rg = lax.broadcasted_iota(jnp.int32, (1, SCAN_GROUP, 1), 1)
        k = 1
        while k < SCAN_GROUP:
            shift = k if forward else SCAN_GROUP - k
            valid = (rg >= k) if forward else (rg < SCAN_GROUP - k)
            a_sh = jnp.where(valid, pltpu.roll(aa, shift, 1), 1.0)
            h_sh = jnp.where(valid, pltpu.roll(hh, shift, 1), 0.0)
            hh = aa * h_sh + hh
            aa = aa * a_sh
            k *= 2
        carry = h_scr[...]
        last = SCAN_GROUP - 1 if forward else 0
        for g in (range(ngroups) if forward else range(ngroups - 1, -1, -1)):
            hg = hh[g] + aa[g] * carry
            o_ref[0, 0, g * SCAN_GROUP:(g + 1) * SCAN_GROUP, :] = hg
            carry = hg[last:last + 1, :]
        h_scr[...] = carry

    @pl.when(d == 0)
    def _():
        scan(True)

    @pl.when(d == 1)
    def _():
        scan(False)


def _lru(p, conv_w, conv_b, wg, bg, lam, nctx):
    b, lt, _ = p.shape
    nblk = lt // TB
    hpb = TB // HALO
    nh = lt // HALO

    def blk(d, j):
        return _scan_block(j, d, nctx, nblk)

    return pl.pallas_call(
        functools.partial(_lru_kernel, nctx=nctx, nblk=nblk),
        grid=(b, 2, nblk),
        in_specs=[pl.BlockSpec((1, TB, LRU_W), lambda bi, d, j: (bi, blk(d, j), 4)),
                  pl.BlockSpec((1, HALO, LRU_W), lambda bi, d, j: (bi, jnp.maximum(blk(d, j) * hpb - 1, 0), 4)),
                  pl.BlockSpec((1, HALO, LRU_W), lambda bi, d, j: (bi, jnp.minimum((blk(d, j) + 1) * hpb, nh - 1), 4)),
                  _const_spec((8, LRU_W)), _const_spec((1, LRU_W)),
                  pl.BlockSpec((1, LRU_W, 2 * LRU_W), lambda bi, d, j: (d, 0, 0)),
                  pl.BlockSpec((1, 8, 2 * LRU_W), lambda bi, d, j: (d, 0, 0)),
                  pl.BlockSpec((1, 8, LRU_W), lambda bi, d, j: (d, 0, 0))],
        out_specs=pl.BlockSpec((1, 1, TB, LRU_W), lambda bi, d, j: (d, bi, blk(d, j), 0)),
        out_shape=jax.ShapeDtypeStruct((2, b, lt, LRU_W), F32),
        scratch_shapes=[pltpu.VMEM((TB + 2 * HALO, LRU_W), F32), pltpu.VMEM((1, LRU_W), F32)],
        compiler_params=_cparams(("parallel", "arbitrary", "arbitrary")),
        name="rglru",
    )(p, p, p, conv_w, conv_b, wg, bg, lam)


def _gelu_tanh(t):
    return 0.5 * t * (1.0 + jnp.tanh(0.7978845608028654 * (t + 0.044715 * t * t * t)))


def _merge_kernel(*refs, ctx_len, nstream):
    x_refs = refs[:nstream]
    (mctx_ref, mb_ref, hgf_ref, hgb_ref, hgg_ref, hgn_ref, hm_ref, mlac_ref, mlax_ref,
     lf_ref, lb_ref, ly_ref, g1_ref, g2_ref, g3_ref, wbr_ref, wout_ref, o_ref) = refs[nstream:]
    i = pl.program_id(1)
    nctx = ctx_len // TB
    o_mla = jnp.where(i < nctx, mlac_ref[0], mlax_ref[0])
    o = hgf_ref[0, 0] + hgb_ref[0, 0]
    ms = jnp.dot((o * o).astype(BF16), hm_ref[...], preferred_element_type=F32)
    o_hg = o * lax.rsqrt(ms + EPS) * hgn_ref[...] * _sigmoid(hgg_ref[0].astype(F32))
    o_lru = (lf_ref[0, 0] + lb_ref[0, 0]) * _gelu_tanh(ly_ref[0].astype(F32))
    y = (_sigmoid(g1_ref[0].astype(F32)) * jnp.dot(o_hg.astype(BF16), wbr_ref[0], preferred_element_type=F32)
         + _sigmoid(g2_ref[0].astype(F32)) * jnp.dot(o_mla, wbr_ref[1], preferred_element_type=F32)
         + _sigmoid(g3_ref[0].astype(F32)) * jnp.dot(o_lru.astype(BF16), wbr_ref[2], preferred_element_type=F32))
    gate = _row_mods(i, TB, ctx_len, mctx_ref, mb_ref, 2)
    o_ref[0] = _stream_tile(x_refs, i, nctx) + gate * jnp.dot(y.astype(BF16), wout_ref[...], preferred_element_type=F32)


def _merge(xs, mctx, mb, o_hg, p, hg_gain, head_mean, o_mla_c, o_mla_x, h_lru, w_br, w_out, ctx_len):
    b, lt, _ = p.shape
    nctx = ctx_len // TB
    x_specs, x_args = _stream_specs(xs, TB, nctx)

    def pcol(width, c):
        return pl.BlockSpec((1, TB, width), lambda bi, i: (bi, i, c))

    def dirspec(d):
        return pl.BlockSpec((1, 1, TB, 512), lambda bi, i: (d, bi, i, 0))

    return pl.pallas_call(
        functools.partial(_merge_kernel, ctx_len=ctx_len, nstream=len(x_args)),
        grid=(b, lt // TB),
        in_specs=x_specs + [
                  _const_spec((6, D_MODEL)),
                  pl.BlockSpec((1, 6, D_MODEL), lambda bi, i: (bi, 0, 0)),
                  dirspec(0), dirspec(0), pcol(512, 2),
                  _const_spec((1, HG_W)), _const_spec((HG_W, HG_W)),
                  pl.BlockSpec((1, TB, MLA_W), lambda bi, i: (bi, jnp.minimum(i, nctx - 1), 0)),
                  pl.BlockSpec((1, TB, MLA_W), lambda bi, i: (bi, jnp.maximum(i - nctx, 0), 0)),
                  dirspec(0), dirspec(1), pcol(512, 5),
                  pcol(D_MODEL, 3), pcol(D_MODEL, 4), pcol(D_MODEL, 5),
                  _const_spec((3, 512, D_MODEL)), _const_spec((D_MODEL, D_MODEL))],
        out_specs=pl.BlockSpec((1, TB, D_MODEL), lambda bi, i: (bi, i, 0)),
        out_shape=jax.ShapeDtypeStruct((b, lt, D_MODEL), F32),
        compiler_params=_cparams(("parallel", "arbitrary")),
        name="merge",
    )(*x_args, mctx, mb, o_hg[0], o_hg[1], p, hg_gain, head_mean, o_mla_c, o_mla_x, h_lru, h_lru, p, p, p, p, w_br, w_out)


def _first_row_of_max(vals, row, valid):
    masked = jnp.where(valid, vals, -jnp.inf)
    m = jnp.max(masked, axis=0, keepdims=True)
    idx = jnp.min(jnp.where(masked == m, row, ROUTER_PAD), axis=0, keepdims=True)
    return m, idx


def _router(h, wr_hi, wr_lo, br):
    hi = h.astype(BF16)
    lo = (h - hi.astype(F32)).astype(BF16)
    logits = (jnp.dot(hi, wr_hi, preferred_element_type=F32) + jnp.dot(lo, wr_hi, preferred_element_type=F32)
              + jnp.dot(hi, wr_lo, preferred_element_type=F32))
    biased = (logits + br).T
    logits = logits.T
    row = lax.broadcasted_iota(jnp.int32, logits.shape, 0)
    is_group = jnp.logical_and(row >= N_EXPERTS, row < N_EXPERTS + N_GROUPS)
    _, g_row = _first_row_of_max(biased, row, is_group)
    g_max, _ = _first_row_of_max(logits, row, is_group)
    g_exp = jnp.where(is_group, jnp.exp(logits - g_max), 0.0)
    g_sel_logit = jnp.sum(jnp.where(row == g_row, logits, 0.0), axis=0, keepdims=True)
    p_g = jnp.exp(g_sel_logit - g_max) / jnp.sum(g_exp, axis=0, keepdims=True)
    in_group = jnp.right_shift(row, 2) == (g_row - N_EXPERTS)
    _, i1 = _first_row_of_max(biased, row, in_group)
    _, i2 = _first_row_of_max(biased, row, jnp.logical_and(in_group, row != i1))
    l1 = jnp.sum(jnp.where(row == i1, logits, 0.0), axis=0, keepdims=True)
    l2 = jnp.sum(jnp.where(row == i2, logits, 0.0), axis=0, keepdims=True)
    lm = jnp.maximum(l1, l2)
    e1 = jnp.exp(l1 - lm)
    e2 = jnp.exp(l2 - lm)
    inv = p_g / (e1 + e2)
    comb_t = jnp.where(row == i1, e1 * inv, 0.0) + jnp.where(row == i2, e2 * inv, 0.0)
    return comb_t.T


def _moe_kernel(x_ref, mctx_ref, mb_ref, gain_ref, wrh_ref, wrl_ref, br_ref, w1_ref, w3_ref, w2_ref, o_ref, *,
                ctx_len, tm, tile0):
    i = pl.program_id(1) + tile0
    epb = N_EXPERTS // EXPERT_STEPS
    nrows = min(MOE_RB, tm)
    for rb in range(tm // nrows):
        rows = slice(rb * nrows, (rb + 1) * nrows)
        tile = i * (tm // nrows) + rb
        shift = _row_mods(tile, nrows, ctx_len, mctx_ref, mb_ref, 3)
        scale = _row_mods(tile, nrows, ctx_len, mctx_ref, mb_ref, 4)
        gate = _row_mods(tile, nrows, ctx_len, mctx_ref, mb_ref, 5)
        x = x_ref[0, rows, :]
        h = _norm_modulate(x, gain_ref[...], shift, scale)
        comb = _router(h, wrh_ref[...], wrl_ref[...], br_ref[...])
        hb = h.astype(BF16)
        acc = None
        for es in range(EXPERT_STEPS):
            parts = []
            for e in range(es * epb, (es + 1) * epb):
                h1 = jnp.dot(hb, w1_ref[0, e], preferred_element_type=F32)
                h3 = jnp.dot(hb, w3_ref[0, e], preferred_element_type=F32)
                parts.append(h1 * _sigmoid(h1) * h3 * comb[:, e:e + 1])
            y = jnp.dot(jnp.concatenate(parts, axis=-1).astype(BF16), w2_ref[0, es], preferred_element_type=F32)
            acc = y if acc is None else acc + y
        o_ref[0, rows, :] = x + gate * acc


def _moe(xs, mctx, mb, gain, wr_hi, wr_lo, br, w1, w3, w2, layer, ctx_len, latent_only):
    b, lt, _ = xs.shape
    epb = N_EXPERTS // EXPERT_STEPS
    def layer_spec(shape):
        return pl.BlockSpec((1,) + shape, lambda bi, i: (layer,) + (0,) * len(shape), pipeline_mode=pl.Buffered(1))

    if latent_only:
        tm, tile0, rows = TB, ctx_len // TB, lt - ctx_len
    else:
        tm, tile0, rows = (MOE_TM if lt % MOE_TM == 0 else TB), 0, lt
    return pl.pallas_call(
        functools.partial(_moe_kernel, ctx_len=ctx_len, tm=tm, tile0=tile0),
        grid=(b, rows // tm),
        in_specs=[pl.BlockSpec((1, tm, D_MODEL), lambda bi, i: (bi, i + tile0, 0)),
                  _const_spec((6, D_MODEL)),
                  pl.BlockSpec((1, 6, D_MODEL), lambda bi, i: (bi, 0, 0)),
                  _const_spec((1, D_MODEL)),
                  _const_spec((D_MODEL, ROUTER_PAD)), _const_spec((D_MODEL, ROUTER_PAD)), _const_spec((1, ROUTER_PAD)),
                  layer_spec((N_EXPERTS, D_MODEL, D_EXPERT)), layer_spec((N_EXPERTS, D_MODEL, D_EXPERT)),
                  layer_spec((EXPERT_STEPS, epb * D_EXPERT, D_MODEL))],
        out_specs=pl.BlockSpec((1, tm, D_MODEL), lambda bi, i: (bi, i, 0)),
        out_shape=jax.ShapeDtypeStruct((b, rows, D_MODEL), F32),
        compiler_params=_cparams(("parallel", "arbitrary")),
        name="moe",
    )(xs, mctx, mb, gain, wr_hi, wr_lo, br, w1, w3, w2)


def _rope_tables(ctx_len, seq):
    half = MLA_ROPE // 2
    rows = seq // GRID_W
    pos_row = np.repeat(np.arange(rows, dtype=np.float32), GRID_W)
    pos_col = np.tile(np.arange(GRID_W, dtype=np.float32), rows)
    inv = (ROPE_BASE ** (-np.arange(0, half, 2, dtype=np.float32) / half)).astype(np.float32)
    ang = np.concatenate([pos_row[:, None] * inv, pos_col[:, None] * inv], axis=-1)
    cos, sin = np.cos(ang), np.sin(ang)
    cos_t = np.zeros((ctx_len + seq, HEAD_PAD), np.float32)
    sin_t = np.zeros((ctx_len + seq, HEAD_PAD), np.float32)
    cos_t[:, :MLA_NOPE] = 1.0
    cos_t[:ctx_len, MLA_NOPE:MLA_QK] = 1.0
    cos_t[ctx_len:, MLA_NOPE:MLA_NOPE + half] = cos
    cos_t[ctx_len:, MLA_NOPE + half:MLA_QK] = cos
    sin_t[ctx_len:, MLA_NOPE:MLA_NOPE + half] = -sin
    sin_t[ctx_len:, MLA_NOPE + half:MLA_QK] = sin
    return cos_t, sin_t


def _rope_key_selectors():
    half = MLA_ROPE // 2
    kr0 = MLA_Q_RANK + MLA_KV_RANK
    sk1 = np.zeros((512, MLA_HEADS * HEAD_PAD), np.float32)
    sk2 = np.zeros((512, MLA_HEADS * HEAD_PAD), np.float32)
    for h in range(MLA_HEADS):
        for i in range(MLA_ROPE):
            sk1[kr0 + i, h * HEAD_PAD + MLA_NOPE + i] = 1.0
            sk2[kr0 + (i + half) % MLA_ROPE, h * HEAD_PAD + MLA_NOPE + i] = 1.0
    return sk1.astype(BF16), sk2.astype(BF16)


def _np_block_diag(block, n):
    a, bb = block.shape
    out = np.zeros((n * a, n * bb), np.float32)
    for i in range(n):
        out[i * a:(i + 1) * a, i * bb:(i + 1) * bb] = block
    return out


def _swap_rope_halves(t):
    half = MLA_ROPE // 2
    return jnp.concatenate([jnp.zeros_like(t[..., :MLA_NOPE]), t[..., MLA_NOPE + half:], t[..., MLA_NOPE:MLA_NOPE + half]],
                           axis=-1)


def _pad_heads(t):
    pad = [(0, 0)] * (t.ndim - 1) + [(0, HEAD_PAD - t.shape[-1])]
    t = jnp.pad(t, pad)
    return t.reshape(t.shape[:-2] + (t.shape[-2] * HEAD_PAD,))


def _mla_weights(w_uq, w_ukv, gq, gk):
    wq = w_uq.reshape(MLA_Q_RANK, MLA_HEADS, MLA_QK)
    wq1 = _pad_heads(wq).astype(BF16)
    wq2 = _pad_heads(_swap_rope_halves(wq)).astype(BF16)
    wkv = w_ukv.reshape(MLA_KV_RANK, MLA_HEADS, MLA_NOPE + MLA_V)
    wk1 = _pad_heads(wkv[..., :MLA_NOPE]).astype(BF16)
    wv = _pad_heads(wkv[..., MLA_NOPE:]).astype(BF16)

    def gains(g):
        g1 = jnp.pad(g, (0, HEAD_PAD - MLA_QK))
        g2 = jnp.pad(_swap_rope_halves(g), (0, HEAD_PAD - MLA_QK))
        return jnp.stack([g1, g2], axis=0)

    return wq1, wq2, wk1, wv, gains(gq), gains(gk)


def _block_diag(w):
    n, a, bb = w.shape
    eye = jnp.eye(n, dtype=w.dtype)
    return (eye[:, None, :, None] * w[:, :, None, :]).reshape(n * a, n * bb)


def _pad_rows(t, rows=8):
    return jnp.pad(t, ((0, rows - t.shape[0]), (0, 0)))


def kernel(x, c, ctx, c_ctx, w_mod, b_mod, norm_mix, norm_ffn, w_in, hg_lb, hg_norm, mla_q_norm, mla_kv_norm, mla_w_uq, mla_w_ukv, mla_qk_gain_q, mla_qk_gain_k, lru_conv_w, lru_conv_b, lru_wa, lru_ba, lru_wx, lru_bx, lru_lambda, w_br_hg, w_br_mla, w_br_lru, w_out, moe_w_rg, moe_b_rg, moe_w_re, moe_b_re, moe_w1, moe_w3, moe_w2):
    bsz, seq, _ = x.shape
    ctx_len = ctx.shape[1]
    depth = w_in.shape[0]
    assert seq % TB == 0 and ctx_len % TB == 0 and seq % GRID_W == 0 and bsz < 8
    nctx = ctx_len // TB

    xs = (ctx, x)
    cos_t, sin_t = _rope_tables(ctx_len, seq)
    cvec = jnp.zeros((8, D_MODEL), F32).at[:bsz].set(c).at[bsz].set(c_ctx)

    lb_cs = jnp.cumsum(jax.nn.softmax(hg_lb.astype(F32), axis=0), axis=0)
    lb_all = lb_cs - lb_cs[0:1]
    tri_lo = _np_block_diag(np.tril(np.ones((HG_CHUNK, HG_CHUNK), np.float32)), TB // HG_CHUNK)
    tri = np.stack([tri_lo, tri_lo.T], axis=0).astype(BF16)
    head_ones = _np_block_diag(np.ones((HG_DK, HG_DK), np.float32), HG_HEADS)
    head_mean = (head_ones / HG_DK).astype(BF16)
    sk1, sk2 = _rope_key_selectors()

    w_all = _win_layout(w_in)
    moe_w1b, moe_w3b = moe_w1.astype(BF16), moe_w3.astype(BF16)
    moe_w2b = moe_w2.astype(BF16).reshape(depth, EXPERT_STEPS, N_EXPERTS // EXPERT_STEPS * D_EXPERT, D_MODEL)

    mods = _modulation(cvec, w_mod, b_mod).reshape(depth, 8, 6, D_MODEL)

    for l in range(depth):
        mctx, mb = mods[l, bsz], mods[l, :bsz]

        f, p = _inproj(xs, mctx, mb, norm_mix[l][None, :], w_all, l, ctx_len)

        lb = lb_all[l]
        lbc = jnp.stack([jnp.log(lb), jnp.log1p(-lb), 1.0 - lb] + [jnp.zeros_like(lb)] * 5, axis=1)
        o_hg = _hgrn(p, f, tri, lbc, head_ones, nctx)

        wq1, wq2, wk1, wv, gq, gk = _mla_weights(mla_w_uq[l], mla_w_ukv[l], mla_qk_gain_q[l], mla_qk_gain_k[l])
        qt, kh, vt, qsq, ksq = _mla_prep(p, cos_t, sin_t, mla_q_norm[l][None, :], mla_kv_norm[l][None, :], gq, gk,
                                    wq1, wq2, wk1, sk1, sk2, wv, nctx)
        bound = jnp.sqrt(jnp.max(qsq, axis=(1, 3)) * jnp.max(ksq, axis=(1, 3))).reshape(bsz, MLA_HEADS // ATTN_HEADS_PER_STEP, ATTN_HEADS_PER_STEP).max(axis=-1)
        tq = next(t for t in (512, 256) if seq % t == 0)
        o_mla_x = _flash(bound, qt, kh, vt, seq, ctx_len + seq, tq, 0, "mla_attention")
        if l < depth - 1:
            o_mla_c = _flash(bound, qt, kh, vt, ctx_len, ctx_len, TB, seq, "mla_attention_ctx")
        else:
            o_mla_c = o_mla_x[:, :ctx_len]

        wg = jnp.stack([jnp.concatenate([_block_diag(lru_wa[l, d]), _block_diag(lru_wx[l, d])], axis=-1)
                        for d in range(2)], axis=0).astype(BF16)
        bg = jnp.concatenate([lru_ba[l], lru_bx[l]], axis=-1)[:, None, :] * jnp.ones((1, 8, 1), F32)
        lam = lru_lambda[l][:, None, :] * jnp.ones((1, 8, 1), F32)
        h_lru = _lru(p, _pad_rows(lru_conv_w[l]), lru_conv_b[l][None, :], wg, bg, lam, nctx)

        w_br = jnp.stack([w_br_hg[l], w_br_mla[l], w_br_lru[l]], axis=0).astype(BF16)
        hg_gain = jnp.tile(hg_norm[l], HG_HEADS)[None, :]
        xs = _merge(xs, mctx, mb, o_hg, p, hg_gain, head_mean, o_mla_c, o_mla_x, h_lru, w_br, w_out[l].astype(BF16), ctx_len)

        wr = jnp.pad(jnp.concatenate([moe_w_re[l], moe_w_rg[l]], axis=-1), ((0, 0), (0, ROUTER_PAD - N_EXPERTS - N_GROUPS)))
        br = jnp.pad(jnp.concatenate([moe_b_re[l], moe_b_rg[l]]), (0, ROUTER_PAD - N_EXPERTS - N_GROUPS))[None, :]
        wr_hi = wr.astype(BF16)
        wr_lo = (wr - wr_hi.astype(F32)).astype(BF16)
        xs = _moe(xs, mctx, mb, norm_ffn[l][None, :], wr_hi, wr_lo, br, moe_w1b, moe_w3b, moe_w2b, l,
                  ctx_len, latent_only=(l == depth - 1))

    return xs
```
